```python
import math
import jax, jax.numpy as jnp
from jax import lax
import numpy as np

D_MODEL = 1024
BATCH = 2
SEQ = 8192
DEPTH = 2

N_MIXERS = 2
N_ATTN_LAYERS = (DEPTH + N_MIXERS - 1) // N_MIXERS
N_FOURIER_LAYERS = DEPTH // N_MIXERS
N_DIFF_HEADS = 8
DIFF_HEAD_DIM = D_MODEL // N_DIFF_HEADS // 2
DIFF_V_DIM = 2 * DIFF_HEAD_DIM
Q_BLOCK = 128
N_FOURIER_GROUPS = 4
FOURIER_GROUP_DIM = D_MODEL // N_FOURIER_GROUPS
N_EXPERT_GROUPS = 4
EXPERTS_PER_GROUP = 4
N_EXPERTS = N_EXPERT_GROUPS * EXPERTS_PER_GROUP
EXPERT_TOP_K = 2
D_EXPERT = D_MODEL // 2
N_ADA = 6
ADA_INIT_SCALE = 0.5
EPS = 1e-6

kernel_name = "hybrid_diffattn_fnet_hmoe_adaln"


def rmsnorm(x, g):
    xf = x.astype(jnp.float32)
    y = xf * lax.rsqrt(jnp.mean(xf * xf, axis=-1, keepdims=True) + EPS)
    return (y * g.astype(jnp.float32)).astype(x.dtype)


def modulate(h, shift, scale):
    return h * (1.0 + scale[:, None, :]) + shift[:, None, :]


def alibi_slopes(n_heads):
    return 2.0 ** (-8.0 * jnp.arange(1, n_heads + 1, dtype=jnp.float32) / n_heads)


def diff_attention(h, w_in, lam_q1, lam_k1, lam_q2, lam_k2, subln_g, w_out, layer_idx):
    B, S, D = h.shape
    H, dh, dv = N_DIFF_HEADS, DIFF_HEAD_DIM, DIFF_V_DIM
    qkv = h @ w_in
    q, k, v = jnp.split(qkv, 3, axis=-1)
    q = q.reshape(B, S, H, 2, dh) * (dh ** -0.5)
    k = k.reshape(B, S, H, 2, dh)
    v = v.reshape(B, S, H, dv)
    q1, q2 = q[..., 0, :], q[..., 1, :]
    k1, k2 = k[..., 0, :], k[..., 1, :]
    lam_init = 0.8 - 0.6 * math.exp(-0.3 * layer_idx)
    lam = (jnp.exp(jnp.sum(lam_q1.astype(jnp.float32) * lam_k1.astype(jnp.float32)))
           - jnp.exp(jnp.sum(lam_q2.astype(jnp.float32) * lam_k2.astype(jnp.float32)))
           + lam_init)
    slopes = alibi_slopes(H)
    pos = jnp.arange(S, dtype=jnp.float32)
    nb = S // Q_BLOCK
    q1b = q1.reshape(B, nb, Q_BLOCK, H, dh).transpose(1, 0, 3, 2, 4)
    q2b = q2.reshape(B, nb, Q_BLOCK, H, dh).transpose(1, 0, 3, 2, 4)
    posb = pos.reshape(nb, Q_BLOCK)

    def block(args):
        qa, qb, pq = args
        bias = -slopes[:, None, None] * jnp.abs(pq[:, None] - pos[None, :])
        s1 = jnp.einsum('bhqd,bkhd->bhqk', qa, k1, preferred_element_type=jnp.float32) + bias
        s2 = jnp.einsum('bhqd,bkhd->bhqk', qb, k2, preferred_element_type=jnp.float32) + bias
        a = jax.nn.softmax(s1, axis=-1) - lam * jax.nn.softmax(s2, axis=-1)
        return jnp.einsum('bhqk,bkhd->bqhd', a.astype(v.dtype), v)

    o = lax.map(block, (q1b, q2b, posb))
    o = o.transpose(1, 0, 2, 3, 4).reshape(B, S, H, dv)
    o = rmsnorm(o, subln_g) * (1.0 - lam_init)
    return o.reshape(B, S, H * dv) @ w_out


def fourier_mix(h, w_in, w_out):
    B, S, D = h.shape
    u = (h @ w_in).reshape(B, S, N_FOURIER_GROUPS, FOURIER_GROUP_DIM).astype(jnp.float32)
    f = jnp.fft.fft2(u, axes=(1, 3), norm="ortho").real
    return f.reshape(B, S, D).astype(h.dtype) @ w_out


def hier_moe(h, w_rg, b_rg, w_re, b_re, w_gate, w_up, w_down):
    B, S, D = h.shape
    t = h.reshape(B * S, D)
    T = t.shape[0]
    g_logits = (t @ w_rg).astype(jnp.float32) + b_rg.astype(jnp.float32)
    g_prob = jax.nn.softmax(g_logits, axis=-1)
    g_idx = jnp.argmax(g_logits, axis=-1)
    g_w = jnp.take_along_axis(g_prob, g_idx[:, None], axis=-1)
    e_logits = jnp.einsum('td,dge->tge', t, w_re).astype(jnp.float32) + b_re.astype(jnp.float32)
    e_sel = jnp.take_along_axis(e_logits, g_idx[:, None, None], axis=1)[:, 0]
    top_v, top_i = lax.top_k(e_sel, EXPERT_TOP_K)
    top_w = jax.nn.softmax(top_v, axis=-1)
    e_w = jnp.sum(jax.nn.one_hot(top_i, EXPERTS_PER_GROUP, dtype=jnp.float32) * top_w[..., None], axis=1)
    gates = (jax.nn.one_hot(g_idx, N_EXPERT_GROUPS, dtype=jnp.float32)[:, :, None]
             * (g_w[:, :, None] * e_w[:, None, :])).reshape(T, N_EXPERTS)
    hid = jax.nn.silu(jnp.einsum('td,edf->tef', t, w_gate)) * jnp.einsum('td,edf->tef', t, w_up)
    hid = hid * gates.astype(hid.dtype)[:, :, None]
    y = jnp.einsum('tef,efd->td', hid, w_down)
    return y.reshape(B, S, D)


def setup_inputs(seed: int = 0) -> dict:
    key = jax.random.key(seed)
    ks = jax.random.split(key, 24)
    D = D_MODEL

    def nrm(k, shape, scale):
        return jax.random.normal(k, shape, jnp.float32) * scale

    return {
        "x": nrm(ks[0], (BATCH, SEQ, D), 1.0),
        "c": nrm(ks[1], (BATCH, D), 1.0),
        "ada_w": nrm(ks[2], (DEPTH, D, N_ADA * D), ADA_INIT_SCALE * D ** -0.5),
        "ada_b": nrm(ks[3], (DEPTH, N_ADA * D), 0.02),
        "norm_mix": 1.0 + nrm(ks[4], (DEPTH, D), 0.02),
        "norm_ffn": 1.0 + nrm(ks[5], (DEPTH, D), 0.02),
        "attn_w_in": nrm(ks[6], (N_ATTN_LAYERS, D, 3 * D), D ** -0.5),
        "attn_lam_q1": nrm(ks[7], (N_ATTN_LAYERS, DIFF_HEAD_DIM), 0.1),
        "attn_lam_k1": nrm(ks[8], (N_ATTN_LAYERS, DIFF_HEAD_DIM), 0.1),
        "attn_lam_q2": nrm(ks[9], (N_ATTN_LAYERS, DIFF_HEAD_DIM), 0.1),
        "attn_lam_k2": nrm(ks[10], (N_ATTN_LAYERS, DIFF_HEAD_DIM), 0.1),
        "attn_subln": 1.0 + nrm(ks[11], (N_ATTN_LAYERS, DIFF_V_DIM), 0.02),
        "attn_w_out": nrm(ks[12], (N_ATTN_LAYERS, N_DIFF_HEADS * DIFF_V_DIM, D), (N_DIFF_HEADS * DIFF_V_DIM) ** -0.5),
        "fourier_w_in": nrm(ks[13], (N_FOURIER_LAYERS, D, D), D ** -0.5),
        "fourier_w_out": nrm(ks[14], (N_FOURIER_LAYERS, D, D), D ** -0.5),
        "router_group_w": nrm(ks[15], (DEPTH, D, N_EXPERT_GROUPS), D ** -0.5),
        "router_group_b": nrm(ks[16], (DEPTH, N_EXPERT_GROUPS), 0.01),
        "router_expert_w": nrm(ks[17], (DEPTH, D, N_EXPERT_GROUPS, EXPERTS_PER_GROUP), D ** -0.5),
        "router_expert_b": nrm(ks[18], (DEPTH, N_EXPERT_GROUPS, EXPERTS_PER_GROUP), 0.01),
        "expert_w_gate": nrm(ks[19], (DEPTH, N_EXPERTS, D, D_EXPERT), D ** -0.5),
        "expert_w_up": nrm(ks[20], (DEPTH, N_EXPERTS, D, D_EXPERT), D ** -0.5),
        "expert_w_down": nrm(ks[21], (DEPTH, N_EXPERTS, D_EXPERT, D), D_EXPERT ** -0.5),
        "norm_final": 1.0 + nrm(ks[22], (D,), 0.02),
    }


def reference(x, c, ada_w, ada_b, norm_mix, norm_ffn, attn_w_in, attn_lam_q1, attn_lam_k1,
              attn_lam_q2, attn_lam_k2, attn_subln, attn_w_out, fourier_w_in, fourier_w_out,
              router_group_w, router_group_b, router_expert_w, router_expert_b,
              expert_w_gate, expert_w_up, expert_w_down, norm_final):
    cond = jax.nn.silu(c)
    for i in range(DEPTH):
        mod = cond @ ada_w[i] + ada_b[i]
        sh1, sc1, g1, sh2, sc2, g2 = jnp.split(mod, N_ADA, axis=-1)
        h = modulate(rmsnorm(x, norm_mix[i]), sh1, sc1)
        j = i // N_MIXERS
        if i % N_MIXERS == 0:
            y = diff_attention(h, attn_w_in[j], attn_lam_q1[j], attn_lam_k1[j], attn_lam_q2[j],
                               attn_lam_k2[j], attn_subln[j], attn_w_out[j], i)
        else:
            y = fourier_mix(h, fourier_w_in[j], fourier_w_out[j])
        x = x + g1[:, None, :] * y
        h = modulate(rmsnorm(x, norm_ffn[i]), sh2, sc2)
        y = hier_moe(h, router_group_w[i], router_group_b[i], router_expert_w[i], router_expert_b[i],
                     expert_w_gate[i], expert_w_up[i], expert_w_down[i])
        x = x + g2[:, None, :] * y
    return rmsnorm(x, norm_final)
```

```python
import functools
import math

import numpy as np
import jax
import jax.numpy as jnp
from jax import lax
from jax.experimental import pallas as pl
from jax.experimental.pallas import tpu as pltpu

F32 = jnp.float32
BF16 = jnp.bfloat16

EPS = 1e-6
N_DIFF_HEADS = 8
DIFF_HEAD_DIM = 64
DIFF_V_DIM = 128
N_FOURIER_GROUPS = 4
N_EXPERT_GROUPS = 4
EXPERTS_PER_GROUP = 4
N_EXPERTS = 16
N_ADA = 6

LANES = 128
VMEM_LIMIT_BYTES = 56 * 1024 * 1024

FFT_N1 = 128
FFT_N2 = 64
FFT_K1_PER_STEP = 8


def _cparams(n_axes):
    return pltpu.CompilerParams(
        dimension_semantics=("arbitrary",) * n_axes,
        vmem_limit_bytes=VMEM_LIMIT_BYTES,
    )


def _rms_mod(x, g, shift, scale):
    y = x * lax.rsqrt(jnp.mean(x * x, axis=-1, keepdims=True) + EPS)
    return (y * g) * (1.0 + scale) + shift


def _ada_kernel(c_ref, w_ref, b_ref, o_ref):
    c = c_ref[...]
    cond = c / (1.0 + jnp.exp(-c))
    o_ref[0] = jnp.dot(cond, w_ref[0], precision=lax.Precision.HIGHEST,
                       preferred_element_type=F32) + b_ref[0]


def _ada(c8, ada_w, ada_b3):
    depth, d, n = ada_w.shape
    tn = 1536
    return pl.pallas_call(
        _ada_kernel,
        grid=(depth, n // tn),
        in_specs=[
            pl.BlockSpec((8, d), lambda i, j: (0, 0)),
            pl.BlockSpec((1, d, tn), lambda i, j: (i, 0, j)),
            pl.BlockSpec((1, 1, tn), lambda i, j: (i, 0, j)),
        ],
        out_specs=pl.BlockSpec((1, 8, tn), lambda i, j: (i, 0, j)),
        out_shape=jax.ShapeDtypeStruct((depth, 8, n), F32),
        compiler_params=_cparams(2),
        name="ada_mod",
    )(c8, ada_w, ada_b3)


def _nmm_kernel(x_ref, g_ref, sh_ref, sc_ref, w_ref, o_ref):
    h = _rms_mod(x_ref[0], g_ref[...], sh_ref[0], sc_ref[0])
    o_ref[0] = jnp.dot(h.astype(BF16), w_ref[...],
                       preferred_element_type=F32).astype(o_ref.dtype)


def _norm_mod_matmul(x, g, shift, scale, w_bf16, tm):
    b, s, d = x.shape
    n = w_bf16.shape[1]
    return pl.pallas_call(
        _nmm_kernel,
        grid=(b, s // tm),
        in_specs=[
            pl.BlockSpec((1, tm, d), lambda i, j: (i, j, 0)),
            pl.BlockSpec((1, d), lambda i, j: (0, 0)),
            pl.BlockSpec((1, 1, d), lambda i, j: (i, 0, 0)),
            pl.BlockSpec((1, 1, d), lambda i, j: (i, 0, 0)),
            pl.BlockSpec((d, n), lambda i, j: (0, 0)),
        ],
        out_specs=pl.BlockSpec((1, tm, n), lambda i, j: (i, j, 0)),
        out_shape=jax.ShapeDtypeStruct((b, s, n), BF16),
        compiler_params=_cparams(2),
        name="norm_mod_matmul",
    )(x, g, shift, scale, w_bf16)


ATT_TQ = 256
ATT_TK = 256


def _attn_kernel(q_ref, k_ref, v_ref, slope_ref, lq1_ref, lk1_ref, lq2_ref, lk2_ref, sub_ref,
                 o_ref, vt_ref, rc_ref, m_ref, l_ref, acc_ref, *, lam_init, seq):
    tq, tk = ATT_TQ, ATT_TK
    dh = DIFF_HEAD_DIM
    qi = pl.program_id(2)
    n_kc = seq // tk

    @pl.when(qi == 0)
    def _():
        for c in range(seq // 512):
            vt_ref[:, c * 512:(c + 1) * 512] = v_ref[0, c * 512:(c + 1) * 512, :].T
        r = lax.broadcasted_iota(jnp.int32, (tk, 2 * tq), 0)
        c = lax.broadcasted_iota(jnp.int32, (tk, 2 * tq), 1)
        c = jnp.where(c >= tq, c - tq, c)
        rc_ref[...] = (r - c).astype(F32)

    slope = slope_ref[0]
    qt = q_ref[0].T
    z = jnp.zeros((dh, tq), BF16)
    qbd = jnp.concatenate([jnp.concatenate([qt[:dh], z], axis=1),
                           jnp.concatenate([z, qt[dh:]], axis=1)], axis=0)

    m_ref[...] = jnp.full(m_ref.shape, -jnp.inf, F32)
    l_ref[...] = jnp.zeros(l_ref.shape, F32)
    acc_ref[...] = jnp.zeros(acc_ref.shape, F32)

    def body(kc, carry):
        k0 = pl.multiple_of(kc * tk, tk)
        kblk = k_ref[0, pl.ds(k0, tk), :]
        u = jnp.dot(kblk, qbd, preferred_element_type=F32)
        delta = (kc * tk - qi * tq).astype(F32)
        t = u - slope * jnp.abs(rc_ref[...] + delta)
        m_old = m_ref[...]
        m_new = jnp.maximum(m_old, jnp.max(t, axis=0, keepdims=True))
        alpha = jnp.exp(m_old - m_new)
        p = jnp.exp(t - m_new)
        l_ref[...] = alpha * l_ref[...] + jnp.sum(p, axis=0, keepdims=True)
        m_ref[...] = m_new
        vt = vt_ref[:, pl.ds(k0, tk)]
        acc_ref[...] = alpha * acc_ref[...] + jnp.dot(vt, p.astype(BF16),
                                                      preferred_element_type=F32)
        return carry

    lax.fori_loop(0, n_kc, body, 0)

    lam = (jnp.exp(jnp.sum(lq1_ref[...] * lk1_ref[...], axis=-1, keepdims=True))
           - jnp.exp(jnp.sum(lq2_ref[...] * lk2_ref[...], axis=-1, keepdims=True))
           + lam_init)
    acc = acc_ref[...]
    l = l_ref[...]
    ot = acc[:, :tq] / l[:, :tq] - lam * (acc[:, tq:] / l[:, tq:])
    ms = jnp.mean(ot * ot, axis=0, keepdims=True)
    ot = ot * lax.rsqrt(ms + EPS) * sub_ref[...] * (1.0 - lam_init)
    o_ref[0] = ot.T.astype(o_ref.dtype)


def _diff_attention(qkv, slopes, lq1, lk1, lq2, lk2, subln_col, lam_init):
    b, s, _ = qkv.shape
    h, dv = N_DIFF_HEADS, DIFF_V_DIM
    tq, tk = ATT_TQ, ATT_TK
    kern = functools.partial(_attn_kernel, lam_init=lam_init, seq=s)
    vec = pl.BlockSpec((1, DIFF_HEAD_DIM), lambda bi, hi, qi: (0, 0))
    return pl.pallas_call(
        kern,
        grid=(b, h, s // tq),
        in_specs=[
            pl.BlockSpec((1, tq, dv), lambda bi, hi, qi: (bi, qi, hi)),
            pl.BlockSpec((1, s, dv), lambda bi, hi, qi: (bi, 0, h + hi)),
            pl.BlockSpec((1, s, dv), lambda bi, hi, qi: (bi, 0, 2 * h + hi)),
            pl.BlockSpec((1, 1, 1), lambda bi, hi, qi: (hi, 0, 0)),
            vec, vec, vec, vec,
            pl.BlockSpec((dv, 1), lambda bi, hi, qi: (0, 0)),
        ],
        out_specs=pl.BlockSpec((1, tq, dv), lambda bi, hi, qi: (bi, qi, hi)),
        out_shape=jax.ShapeDtypeStruct((b, s, h * dv), BF16),
        scratch_shapes=[
            pltpu.VMEM((dv, s), BF16),
            pltpu.VMEM((tk, 2 * tq), F32),
            pltpu.VMEM((1, 2 * tq), F32),
            pltpu.VMEM((1, 2 * tq), F32),
            pltpu.VMEM((dv, 2 * tq), F32),
        ],
        compiler_params=_cparams(3),
        name="diff_attention",
    )(qkv, qkv, qkv, slopes, lq1, lk1, lq2, lk2, subln_col)


def _proj_res_kernel(x_ref, a_ref, w_ref, gate_ref, o_ref):
    y = jnp.dot(a_ref[0], w_ref[...], preferred_element_type=F32)
    o_ref[0] = x_ref[0] + gate_ref[0] * y


def _proj_residual(x, a, w_bf16, gate, tm):
    b, s, d = x.shape
    kdim = a.shape[-1]
    return pl.pallas_call(
        _proj_res_kernel,
        grid=(b, s // tm),
        in_specs=[
            pl.BlockSpec((1, tm, d), lambda i, j: (i, j, 0)),
            pl.BlockSpec((1, tm, kdim), lambda i, j: (i, j, 0)),
            pl.BlockSpec((kdim, d), lambda i, j: (0, 0)),
            pl.BlockSpec((1, 1, d), lambda i, j: (i, 0, 0)),
        ],
        out_specs=pl.BlockSpec((1, tm, d), lambda i, j: (i, j, 0)),
        out_shape=jax.ShapeDtypeStruct((b, s, d), F32),
        compiler_params=_cparams(2),
        name="proj_residual",
    )(x, a, w_bf16, gate)


def _routing_gates(logits):
    ng, ne = N_EXPERT_GROUPS, EXPERTS_PER_GROUP
    gl = [logits[:, g:g + 1] for g in range(ng)]
    gmax = functools.reduce(jnp.maximum, gl)
    den = functools.reduce(lambda a, b: a + b, [jnp.exp(v - gmax) for v in gl])
    g_w = 1.0 / den
    sel = []
    taken = jnp.zeros_like(gmax, dtype=jnp.bool_)
    for g in range(ng):
        s = jnp.logical_and(gl[g] == gmax, jnp.logical_not(taken))
        taken = jnp.logical_or(taken, s)
        sel.append(s)
    ev = []
    for e in range(ne):
        v = jnp.zeros_like(gmax)
        for g in range(ng):
            col = ng + ne * g + e
            v = jnp.where(sel[g], logits[:, col:col + 1], v)
        ev.append(v)
    top1 = functools.reduce(jnp.maximum, ev)
    is1 = []
    taken = jnp.zeros_like(gmax, dtype=jnp.bool_)
    for e in range(ne):
        s = jnp.logical_and(ev[e] == top1, jnp.logical_not(taken))
        taken = jnp.logical_or(taken, s)
        is1.append(s)
    rest = [jnp.where(is1[e], -jnp.inf, ev[e]) for e in range(ne)]
    top2 = functools.reduce(jnp.maximum, rest)
    is2 = []
    taken = jnp.zeros_like(gmax, dtype=jnp.bool_)
    for e in range(ne):
        s = jnp.logical_and(jnp.logical_and(rest[e] == top2, jnp.logical_not(is1[e])),
                            jnp.logical_not(taken))
        taken = jnp.logical_or(taken, s)
        is2.append(s)
    ex = jnp.exp(top2 - top1)
    w1 = 1.0 / (1.0 + ex)
    w2 = ex / (1.0 + ex)
    lane = lax.broadcasted_iota(jnp.int32, logits.shape, 1)
    gates = jnp.zeros(logits.shape, F32)
    for g in range(ng):
        for e in range(ne):
            ew = jnp.where(is1[e], w1, jnp.where(is2[e], w2, 0.0))
            val = jnp.where(sel[g], g_w * ew, 0.0)
            gates = jnp.where(lane == ne * g + e, val, gates)
    return gates


def _moe_kernel(x_ref, g_ref, sh_ref, sc_ref, gate_ref, wr_ref, br_ref, wg_ref, wu_ref, wd_ref,
                gf_ref, o_ref, h_ref, gates_ref, acc_ref, *, final_norm):
    e = pl.program_id(2)

    @pl.when(e == 0)
    def _():
        h32 = _rms_mod(x_ref[0], g_ref[...], sh_ref[0], sc_ref[0])
        h_ref[...] = h32.astype(BF16)
        logits = jnp.dot(h32, wr_ref[...], precision=lax.Precision.HIGHEST,
                         preferred_element_type=F32) + br_ref[...]
        gates_ref[...] = _routing_gates(logits)
        acc_ref[...] = jnp.zeros(acc_ref.shape, F32)

    h = h_ref[...]
    lane = lax.broadcasted_iota(jnp.int32, gates_ref.shape, 1)
    gcol = jnp.sum(jnp.where(lane == e, gates_ref[...], 0.0), axis=1, keepdims=True)
    a = jnp.dot(h, wg_ref[0], preferred_element_type=F32)
    u = jnp.dot(h, wu_ref[0], preferred_element_type=F32)
    hid = (a / (1.0 + jnp.exp(-a))) * u * gcol
    acc_ref[...] += jnp.dot(hid.astype(BF16), wd_ref[0], preferred_element_type=F32)

    @pl.when(e == N_EXPERTS - 1)
    def _():
        y = x_ref[0] + gate_ref[0] * acc_ref[...]
        if final_norm:
            y = y * lax.rsqrt(jnp.mean(y * y, axis=-1, keepdims=True) + EPS) * gf_ref[...]
        o_ref[0] = y


def _moe(x, g, shift, scale, gate, wr, br, wg, wu, wd, gfinal, final_norm, tm):
    b, s, d = x.shape
    f = wg.shape[-1]
    kern = functools.partial(_moe_kernel, final_norm=final_norm)
    row = lambda i, j, e: (i, j, 0)
    bvec = lambda i, j, e: (i, 0, 0)
    const2 = lambda i, j, e: (0, 0)
    return pl.pallas_call(
        kern,
        grid=(b, s // tm, N_EXPERTS),
        in_specs=[
            pl.BlockSpec((1, tm, d), row),
            pl.BlockSpec((1, d), const2),
            pl.BlockSpec((1, 1, d), bvec),
            pl.BlockSpec((1, 1, d), bvec),
            pl.BlockSpec((1, 1, d), bvec),
            pl.BlockSpec((d, LANES), const2),
            pl.BlockSpec((1, LANES), const2),
            pl.BlockSpec((1, d, f), lambda i, j, e: (e, 0, 0)),
            pl.BlockSpec((1, d, f), lambda i, j, e: (e, 0, 0)),
            pl.BlockSpec((1, f, d), lambda i, j, e: (e, 0, 0)),
            pl.BlockSpec((1, d), const2),
        ],
        out_specs=pl.BlockSpec((1, tm, d), row),
        out_shape=jax.ShapeDtypeStruct((b, s, d), F32),
        scratch_shapes=[
            pltpu.VMEM((tm, d), BF16),
            pltpu.VMEM((tm, LANES), F32),
            pltpu.VMEM((tm, d), F32),
        ],
        compiler_params=_cparams(3),
        name="hier_moe",
    )(x, g, shift, scale, gate, wr, br, wg, wu, wd, gfinal)


def _fft1_kernel(u_ref, w_ref, twr_ref, twi_ref, o_ref, *, n_sub, width):
    n1 = FFT_N1
    y = jnp.dot(w_ref[...], u_ref[0], preferred_element_type=F32)
    for j in range(n_sub):
        cols = slice(j * width, (j + 1) * width)
        yr, yi = y[:n1, cols], y[n1:, cols]
        twr, twi = twr_ref[j], twi_ref[j]
        o_ref[0, :n1, cols] = (yr * twr - yi * twi).astype(o_ref.dtype)
        o_ref[0, n1:, cols] = (yr * twi + yi * twr).astype(o_ref.dtype)


def _fft_stage1(u3, w1, twr, twi, width):
    b, n1, cols = u3.shape
    n_sub = 8
    tn = n_sub * width
    kern = functools.partial(_fft1_kernel, n_sub=n_sub, width=width)
    return pl.pallas_call(
        kern,
        grid=(b, cols // tn),
        in_specs=[
            pl.BlockSpec((1, n1, tn), lambda i, j: (i, 0, j)),
            pl.BlockSpec((2 * n1, n1), lambda i, j: (0, 0)),
            pl.BlockSpec((n_sub, n1, 1), lambda i, j: (j, 0, 0)),
            pl.BlockSpec((n_sub, n1, 1), lambda i, j: (j, 0, 0)),
        ],
        out_specs=pl.BlockSpec((1, 2 * n1, tn), lambda i, j: (i, 0, j)),
        out_shape=jax.ShapeDtypeStruct((b, 2 * n1, cols), BF16),
        compiler_params=_cparams(2),
        name="fft_stage1",
    )(u3, w1, twr, twi)


def _fft2_kernel(y_ref, g_ref, o_ref):
    kp, n2 = FFT_K1_PER_STEP, FFT_N2
    width = y_ref.shape[-1]
    y = y_ref[0].reshape(2 * kp * n2, width)
    r = jnp.dot(g_ref[...], y, preferred_element_type=F32)
    o_ref[0] = r.reshape(2, n2, kp, width).astype(o_ref.dtype)


def _fft_stage2(y5, gmat):
    b, _, n1, n2, width = y5.shape
    kp = FFT_K1_PER_STEP
    return pl.pallas_call(
        _fft2_kernel,
        grid=(b, n1 // kp),
        in_specs=[
            pl.BlockSpec((1, 2, kp, n2, width), lambda i, j: (i, 0, j, 0, 0)),
            pl.BlockSpec((2 * kp * n2, 2 * kp * n2), lambda i, j: (0, 0)),
        ],
        out_specs=pl.BlockSpec((1, 2, n2, kp, width), lambda i, j: (i, 0, 0, j, 0)),
        out_shape=jax.ShapeDtypeStruct((b, 2, n2, n1, width), F32),
        compiler_params=_cparams(2),
        name="fft_stage2",
    )(y5, gmat)


def _fmix_out_kernel(x_ref, a_ref, cc_ref, sc_ref, w_ref, gate_ref, o_ref):
    gd = cc_ref.shape[0]
    ar, ai = a_ref[0, 0].astype(BF16), a_ref[0, 1].astype(BF16)
    parts = []
    for g in range(N_FOURIER_GROUPS):
        cols = slice(g * gd, (g + 1) * gd)
        parts.append(jnp.dot(ar[:, cols], cc_ref[...], preferred_element_type=F32)
                     + jnp.dot(ai[:, cols], sc_ref[...], preferred_element_type=F32))
    f = jnp.concatenate(parts, axis=1).astype(BF16)
    y = jnp.dot(f, w_ref[...], preferred_element_type=F32)
    o_ref[0] = x_ref[0] + gate_ref[0] * y


def _fmix_out(x, a4, cc, sc, w_bf16, gate, tm):
    b, s, d = x.shape
    gd = cc.shape[0]
    return pl.pallas_call(
        _fmix_out_kernel,
        grid=(b, s // tm),
        in_specs=[
            pl.BlockSpec((1, tm, d), lambda i, j: (i, j, 0)),
            pl.BlockSpec((1, 2, tm, d), lambda i, j: (i, 0, j, 0)),
            pl.BlockSpec((gd, gd), lambda i, j: (0, 0)),
            pl.BlockSpec((gd, gd), lambda i, j: (0, 0)),
            pl.BlockSpec((d, d), lambda i, j: (0, 0)),
            pl.BlockSpec((1, 1, d), lambda i, j: (i, 0, 0)),
        ],
        out_specs=pl.BlockSpec((1, tm, d), lambda i, j: (i, j, 0)),
        out_shape=jax.ShapeDtypeStruct((b, s, d), F32),
        compiler_params=_cparams(2),
        name="fourier_out",
    )(x, a4, cc, sc, w_bf16, gate)


def _dft_tables(seq, gd):
    n1, n2, kp = FFT_N1, FFT_N2, FFT_K1_PER_STEP

    def cs(num, den):
        ang = (2.0 * math.pi / den) * (num % den).astype(np.float64)
        return np.cos(ang), np.sin(ang)

    i1 = np.arange(n1)
    c1, s1 = cs(np.outer(i1, i1), n1)
    w1 = np.concatenate([c1, -s1], axis=0)
    ct, st = cs(np.outer(np.arange(n2), i1), seq)
    twr, twi = ct[:, :, None], -st[:, :, None]
    i2 = np.arange(n2)
    c2, s2 = cs(np.outer(i2, i2), n2)
    m3 = np.block([[c2, s2], [-s2, c2]])
    m3 = m3.reshape(2, n2, 2, n2)
    gmat = np.zeros((2, n2, kp, 2, kp, n2))
    for j in range(kp):
        gmat[:, :, j, :, j, :] = m3
    gmat = gmat.reshape(2 * n2 * kp, 2 * kp * n2)
    ic = np.arange(gd)
    cc, sc = cs(np.outer(ic, ic), gd)
    norm = 1.0 / math.sqrt(seq * gd)
    return tuple(jnp.asarray(t, F32) for t in (w1, twr, twi, gmat, cc * norm, sc * norm))


def kernel(x, c, ada_w, ada_b, norm_mix, norm_ffn, attn_w_in, attn_lam_q1, attn_lam_k1, attn_lam_q2, attn_lam_k2, attn_subln, attn_w_out, fourier_w_in, fourier_w_out, router_group_w, router_group_b, router_expert_w, router_expert_b, expert_w_gate, expert_w_up, expert_w_down, norm_final):
    b, s, d = x.shape
    depth = ada_w.shape[0]
    assert s == FFT_N1 * FFT_N2 and depth == 2

    c8 = jnp.zeros((8, d), F32).at[:b].set(c)
    mod = _ada(c8, ada_w, ada_b.reshape(depth, 1, N_ADA * d))[:, :b]
    mod = mod.reshape(depth, b, N_ADA, 1, d)

    slopes = (2.0 ** (-8.0 * jnp.arange(1, N_DIFF_HEADS + 1, dtype=F32) / N_DIFF_HEADS)
              ).reshape(N_DIFF_HEADS, 1, 1)
    gd = d // N_FOURIER_GROUPS
    w1, twr, twi, gmat, cc, sc = _dft_tables(s, gd)
    w1, gmat, cc, sc = (t.astype(BF16) for t in (w1, gmat, cc, sc))

    for i in range(depth):
        sh1, sc1, g1, sh2, sc2, g2 = [mod[i, :, k] for k in range(N_ADA)]
        j = i // 2
        if i % 2 == 0:
            qscale = jnp.concatenate([jnp.full((d,), DIFF_HEAD_DIM ** -0.5, F32),
                                      jnp.ones((2 * d,), F32)])
            w_in = (attn_w_in[j] * qscale).astype(BF16)
            qkv = _norm_mod_matmul(x, norm_mix[i].reshape(1, d), sh1, sc1, w_in, tm=512)
            lam_init = 0.8 - 0.6 * math.exp(-0.3 * i)
            o = _diff_attention(qkv, slopes,
                                attn_lam_q1[j].reshape(1, -1), attn_lam_k1[j].reshape(1, -1),
                                attn_lam_q2[j].reshape(1, -1), attn_lam_k2[j].reshape(1, -1),
                                attn_subln[j].reshape(-1, 1), lam_init)
            x = _proj_residual(x, o, attn_w_out[j].astype(BF16), g1, tm=512)
        else:
            u = _norm_mod_matmul(x, norm_mix[i].reshape(1, d), sh1, sc1,
                                 fourier_w_in[j].astype(BF16), tm=512)
            y = _fft_stage1(u.reshape(b, FFT_N1, FFT_N2 * d), w1, twr, twi, d)
            a = _fft_stage2(y.reshape(b, 2, FFT_N1, FFT_N2, d), gmat)
            x = _fmix_out(x, a.reshape(b, 2, s, d), cc, sc, fourier_w_out[j].astype(BF16), g1, tm=512)

        wr = jnp.zeros((d, LANES), F32)
        wr = wr.at[:, :N_EXPERT_GROUPS].set(router_group_w[i])
        wr = wr.at[:, N_EXPERT_GROUPS:N_EXPERT_GROUPS + N_EXPERTS].set(
            router_expert_w[i].reshape(d, N_EXPERTS))
        br = jnp.zeros((1, LANES), F32)
        br = br.at[0, :N_EXPERT_GROUPS].set(router_group_b[i])
        br = br.at[0, N_EXPERT_GROUPS:N_EXPERT_GROUPS + N_EXPERTS].set(
            router_expert_b[i].reshape(N_EXPERTS))
        x = _moe(x, norm_ffn[i].reshape(1, d), sh2, sc2, g2, wr, br,
                 expert_w_gate[i].astype(BF16), expert_w_up[i].astype(BF16),
                 expert_w_down[i].astype(BF16), norm_final.reshape(1, d),
                 final_norm=(i == depth - 1), tm=1024)
    return x
```

```python
import functools
import math

import numpy as np
import jax
import jax.numpy as jnp
from jax import lax
from jax.experimental import pallas as pl
from jax.experimental.pallas import tpu as pltpu

F32 = jnp.float32
BF16 = jnp.bfloat16

EPS = 1e-6
N_DIFF_HEADS = 8
DIFF_HEAD_DIM = 64
DIFF_V_DIM = 128
N_FOURIER_GROUPS = 4
N_EXPERT_GROUPS = 4
EXPERTS_PER_GROUP = 4
N_EXPERTS = 16
N_ADA = 6

LANES = 128
VMEM_LIMIT_BYTES = 56 * 1024 * 1024

FFT_N1 = 128
FFT_N2 = 64
FFT_K1_PER_STEP = 8


def _cparams(n_axes):
    return pltpu.CompilerParams(
        dimension_semantics=("arbitrary",) * n_axes,
        vmem_limit_bytes=VMEM_LIMIT_BYTES,
    )


def _rms_mod(x, g, shift, scale):
    y = x * lax.rsqrt(jnp.mean(x * x, axis=-1, keepdims=True) + EPS)
    return (y * g) * (1.0 + scale) + shift


def _ada_kernel(c_ref, w_ref, b_ref, o_ref):
    c = c_ref[...]
    cond = c / (1.0 + jnp.exp(-c))
    o_ref[0] = jnp.dot(cond, w_ref[0], precision=lax.Precision.HIGHEST,
                       preferred_element_type=F32) + b_ref[0]


def _ada(c8, ada_w, ada_b3):
    depth, d, n = ada_w.shape
    tn = 1536
    return pl.pallas_call(
        _ada_kernel,
        grid=(depth, n // tn),
        in_specs=[
            pl.BlockSpec((8, d), lambda i, j: (0, 0)),
            pl.BlockSpec((1, d, tn), lambda i, j: (i, 0, j)),
            pl.BlockSpec((1, 1, tn), lambda i, j: (i, 0, j)),
        ],
        out_specs=pl.BlockSpec((1, 8, tn), lambda i, j: (i, 0, j)),
        out_shape=jax.ShapeDtypeStruct((depth, 8, n), F32),
        compiler_params=_cparams(2),
        name="ada_mod",
    )(c8, ada_w, ada_b3)


def _nmm_kernel(x_ref, g_ref, sh_ref, sc_ref, w_ref, o_ref):
    h = _rms_mod(x_ref[0], g_ref[...], sh_ref[0], sc_ref[0])
    o_ref[0] = jnp.dot(h.astype(BF16), w_ref[...],
                       preferred_element_type=F32).astype(o_ref.dtype)


def _norm_mod_matmul(x, g, shift, scale, w_bf16, tm):
    b, s, d = x.shape
    n = w_bf16.shape[1]
    return pl.pallas_call(
        _nmm_kernel,
        grid=(b, s // tm),
        in_specs=[
            pl.BlockSpec((1, tm, d), lambda i, j: (i, j, 0)),
            pl.BlockSpec((1, d), lambda i, j: (0, 0)),
            pl.BlockSpec((1, 1, d), lambda i, j: (i, 0, 0)),
            pl.BlockSpec((1, 1, d), lambda i, j: (i, 0, 0)),
            pl.BlockSpec((d, n), lambda i, j: (0, 0)),
        ],
        out_specs=pl.BlockSpec((1, tm, n), lambda i, j: (i, j, 0)),
        out_shape=jax.ShapeDtypeStruct((b, s, n), BF16),
        compiler_params=_cparams(2),
        name="norm_mod_matmul",
    )(x, g, shift, scale, w_bf16)


ATT_TQ = 256
ATT_TK = 512
ATT_PAIR = 2 * ATT_TK
ATT_NQP = ATT_PAIR // ATT_TQ
LOG2E = 1.4426950408889634


def _attn_kernel(q_ref, k_ref, v_ref, slope_ref, lq1_ref, lk1_ref, lq2_ref, lk2_ref, sub_ref,
                 o_ref, vt_ref, bias_ref, u_ref, m_ref, l_ref, acc_ref, *, lam_init, seq):
    tq, tk, pair, nqp = ATT_TQ, ATT_TK, ATT_PAIR, ATT_NQP
    dh = DIFF_HEAD_DIM
    qi = pl.program_id(2)
    n_pairs = seq // pair
    slope = slope_ref[0] * LOG2E

    @pl.when(qi == 0)
    def _():
        for c in range(seq // 512):
            vt_ref[:, c * 512:(c + 1) * 512] = v_ref[0, c * 512:(c + 1) * 512, :].T
        for s in range(2):
            rows = slice(s * tk, (s + 1) * tk)
            r = lax.broadcasted_iota(jnp.int32, (tk, 2 * tq), 0) + s * tk
            c = lax.broadcasted_iota(jnp.int32, (tk, 2 * tq), 1)
            c = jnp.where(c >= tq, c - tq, c)
            rc = (r - c).astype(F32)
            bias_ref[0, rows, :] = slope * rc
            bias_ref[1, rows, :] = -(slope * rc)
            for w in range(nqp):
                bias_ref[2 + w, rows, :] = -(slope * jnp.abs(rc - float(w * tq)))

    qt = q_ref[0].T
    z = jnp.zeros((dh, tq), BF16)
    qbd = jnp.concatenate([jnp.concatenate([qt[:dh], z], axis=1),
                           jnp.concatenate([z, qt[dh:]], axis=1)], axis=0)

    m_ref[...] = jnp.full(m_ref.shape, -jnp.inf, F32)
    l_ref[...] = jnp.zeros(l_ref.shape, F32)
    acc_ref[...] = jnp.zeros(acc_ref.shape, F32)

    kd = qi // nqp
    w = qi % nqp

    def mode(kp):
        delta = (kp * pair - qi * tq).astype(F32)
        idx = jnp.where(kp < kd, 0, jnp.where(kp > kd, 1, 2 + w))
        sgn = jnp.where(kp < kd, delta, jnp.where(kp > kd, -delta, 0.0))
        return idx, slope * sgn

    def qk_phase(kp, idx, s):
        k0 = pl.multiple_of(kp * pair + s * tk, tk)
        kblk = k_ref[0, pl.ds(k0, tk), :]
        u = jnp.dot(kblk, qbd, preferred_element_type=F32)
        v = u + bias_ref[idx, s * tk:(s + 1) * tk, :]
        u_ref[s] = v
        return jnp.max(v, axis=0, keepdims=True)

    def sm_phase(kp, cst, s, mx):
        k0 = pl.multiple_of(kp * pair + s * tk, tk)
        m_old = m_ref[...]
        m_new = jnp.maximum(m_old, mx + cst)
        alpha = jnp.exp2(m_old - m_new)
        p = jnp.exp2(u_ref[s] - (m_new - cst))
        l_ref[...] = alpha * l_ref[...] + jnp.sum(p, axis=0, keepdims=True)
        m_ref[...] = m_new
        vt = vt_ref[:, pl.ds(k0, tk)]
        acc_ref[...] = alpha * acc_ref[...] + jnp.dot(vt, p.astype(BF16),
                                                      preferred_element_type=F32)

    lo, hi = 0, n_pairs
    idx0, _ = mode(lo)
    mx_a0 = qk_phase(lo, idx0, 0)

    def body(kp, mx_a):
        idx, cst = mode(kp)
        mx_b = qk_phase(kp, idx, 1)
        sm_phase(kp, cst, 0, mx_a)
        idx_n, _ = mode(kp + 1)
        mx_a_next = qk_phase(kp + 1, idx_n, 0)
        sm_phase(kp, cst, 1, mx_b)
        return mx_a_next

    mx_a = lax.fori_loop(lo, hi - 1, body, mx_a0)
    idx_l, cst_l = mode(hi - 1)
    mx_b = qk_phase(hi - 1, idx_l, 1)
    sm_phase(hi - 1, cst_l, 0, mx_a)
    sm_phase(hi - 1, cst_l, 1, mx_b)

    lam = (jnp.exp(jnp.sum(lq1_ref[...] * lk1_ref[...], axis=-1, keepdims=True))
           - jnp.exp(jnp.sum(lq2_ref[...] * lk2_ref[...], axis=-1, keepdims=True))
           + lam_init)
    acc = acc_ref[...]
    l = l_ref[...]
    ot = acc[:, :tq] / l[:, :tq] - lam * (acc[:, tq:] / l[:, tq:])
    ms = jnp.mean(ot * ot, axis=0, keepdims=True)
    ot = ot * lax.rsqrt(ms + EPS) * sub_ref[...] * (1.0 - lam_init)
    o_ref[0] = ot.T.astype(o_ref.dtype)


def _diff_attention(qkv, slopes, lq1, lk1, lq2, lk2, subln_col, lam_init):
    b, s, _ = qkv.shape
    h, dv = N_DIFF_HEADS, DIFF_V_DIM
    tq, tk = ATT_TQ, ATT_TK
    assert s % ATT_PAIR == 0
    kern = functools.partial(_attn_kernel, lam_init=lam_init, seq=s)
    vec = pl.BlockSpec((1, DIFF_HEAD_DIM), lambda bi, hi, qi: (0, 0))
    return pl.pallas_call(
        kern,
        grid=(b, h, s // tq),
        in_specs=[
            pl.BlockSpec((1, tq, dv), lambda bi, hi, qi: (bi, qi, hi)),
            pl.BlockSpec((1, s, dv), lambda bi, hi, qi: (bi, 0, h + hi)),
            pl.BlockSpec((1, s, dv), lambda bi, hi, qi: (bi, 0, 2 * h + hi)),
            pl.BlockSpec((1, 1, 1), lambda bi, hi, qi: (hi, 0, 0)),
            vec, vec, vec, vec,
            pl.BlockSpec((dv, 1), lambda bi, hi, qi: (0, 0)),
        ],
        out_specs=pl.BlockSpec((1, tq, dv), lambda bi, hi, qi: (bi, qi, hi)),
        out_shape=jax.ShapeDtypeStruct((b, s, h * dv), BF16),
        scratch_shapes=[
            pltpu.VMEM((dv, s), BF16),
            pltpu.VMEM((2 + ATT_NQP, ATT_PAIR, 2 * tq), F32),
            pltpu.VMEM((2, tk, 2 * tq), F32),
            pltpu.VMEM((1, 2 * tq), F32),
            pltpu.VMEM((1, 2 * tq), F32),
            pltpu.VMEM((dv, 2 * tq), F32),
        ],
        compiler_params=_cparams(3),
        name="diff_attention",
    )(qkv, qkv, qkv, slopes, lq1, lk1, lq2, lk2, subln_col)


def _proj_res_kernel(x_ref, a_ref, w_ref, gate_ref, o_ref):
    y = jnp.dot(a_ref[0], w_ref[...], preferred_element_type=F32)
    o_ref[0] = x_ref[0] + gate_ref[0] * y


def _proj_residual(x, a, w_bf16, gate, tm):
    b, s, d = x.shape
    kdim = a.shape[-1]
    return pl.pallas_call(
        _proj_res_kernel,
        grid=(b, s // tm),
        in_specs=[
            pl.BlockSpec((1, tm, d), lambda i, j: (i, j, 0)),
            pl.BlockSpec((1, tm, kdim), lambda i, j: (i, j, 0)),
            pl.BlockSpec((kdim, d), lambda i, j: (0, 0)),
            pl.BlockSpec((1, 1, d), lambda i, j: (i, 0, 0)),
        ],
        out_specs=pl.BlockSpec((1, tm, d), lambda i, j: (i, j, 0)),
        out_shape=jax.ShapeDtypeStruct((b, s, d), F32),
        compiler_params=_cparams(2),
        name="proj_residual",
    )(x, a, w_bf16, gate)


def _routing_gates(logits):
    ng, ne = N_EXPERT_GROUPS, EXPERTS_PER_GROUP
    gl = [logits[:, g:g + 1] for g in range(ng)]
    gmax = functools.reduce(jnp.maximum, gl)
    den = functools.reduce(lambda a, b: a + b, [jnp.exp(v - gmax) for v in gl])
    g_w = 1.0 / den
    sel = []
    taken = jnp.zeros_like(gmax, dtype=jnp.bool_)
    for g in range(ng):
        s = jnp.logical_and(gl[g] == gmax, jnp.logical_not(taken))
        taken = jnp.logical_or(taken, s)
        sel.append(s)
    ev = []
    for e in range(ne):
        v = jnp.zeros_like(gmax)
        for g in range(ng):
            col = ng + ne * g + e
            v = jnp.where(sel[g], logits[:, col:col + 1], v)
        ev.append(v)
    top1 = functools.reduce(jnp.maximum, ev)
    is1 = []
    taken = jnp.zeros_like(gmax, dtype=jnp.bool_)
    for e in range(ne):
        s = jnp.logical_and(ev[e] == top1, jnp.logical_not(taken))
        taken = jnp.logical_or(taken, s)
        is1.append(s)
    rest = [jnp.where(is1[e], -jnp.inf, ev[e]) for e in range(ne)]
    top2 = functools.reduce(jnp.maximum, rest)
    is2 = []
    taken = jnp.zeros_like(gmax, dtype=jnp.bool_)
    for e in range(ne):
        s = jnp.logical_and(jnp.logical_and(rest[e] == top2, jnp.logical_not(is1[e])),
                            jnp.logical_not(taken))
        taken = jnp.logical_or(taken, s)
        is2.append(s)
    ex = jnp.exp(top2 - top1)
    w1 = 1.0 / (1.0 + ex)
    w2 = ex / (1.0 + ex)
    lane = lax.broadcasted_iota(jnp.int32, logits.shape, 1)
    gates = jnp.zeros(logits.shape, F32)
    for g in range(ng):
        for e in range(ne):
            ew = jnp.where(is1[e], w1, jnp.where(is2[e], w2, 0.0))
            val = jnp.where(sel[g], g_w * ew, 0.0)
            gates = jnp.where(lane == ne * g + e, val, gates)
    return gates


def _moe_kernel(x_ref, g_ref, sh_ref, sc_ref, gate_ref, wr_ref, br_ref, wg_ref, wu_ref, wd_ref,
                gf_ref, o_ref, h_ref, gates_ref, acc_ref, *, final_norm):
    e = pl.program_id(2)

    @pl.when(e == 0)
    def _():
        h32 = _rms_mod(x_ref[0], g_ref[...], sh_ref[0], sc_ref[0])
        h_ref[...] = h32.astype(BF16)
        logits = jnp.dot(h32, wr_ref[...], precision=lax.Precision.HIGHEST,
                         preferred_element_type=F32) + br_ref[...]
        gates_ref[...] = _routing_gates(logits)
        acc_ref[...] = jnp.zeros(acc_ref.shape, F32)

    h = h_ref[...]
    lane = lax.broadcasted_iota(jnp.int32, gates_ref.shape, 1)
    gcol = jnp.sum(jnp.where(lane == e, gates_ref[...], 0.0), axis=1, keepdims=True)
    a = jnp.dot(h, wg_ref[0], preferred_element_type=F32)
    u = jnp.dot(h, wu_ref[0], preferred_element_type=F32)
    hid = (a / (1.0 + jnp.exp(-a))) * u * gcol
    acc_ref[...] += jnp.dot(hid.astype(BF16), wd_ref[0], preferred_element_type=F32)

    @pl.when(e == N_EXPERTS - 1)
    def _():
        y = x_ref[0] + gate_ref[0] * acc_ref[...]
        if final_norm:
            y = y * lax.rsqrt(jnp.mean(y * y, axis=-1, keepdims=True) + EPS) * gf_ref[...]
        o_ref[0] = y


def _moe(x, g, shift, scale, gate, wr, br, wg, wu, wd, gfinal, final_norm, tm):
    b, s, d = x.shape
    f = wg.shape[-1]
    kern = functools.partial(_moe_kernel, final_norm=final_norm)
    row = lambda i, j, e: (i, j, 0)
    bvec = lambda i, j, e: (i, 0, 0)
    const2 = lambda i, j, e: (0, 0)
    return pl.pallas_call(
        kern,
        grid=(b, s // tm, N_EXPERTS),
        in_specs=[
            pl.BlockSpec((1, tm, d), row),
            pl.BlockSpec((1, d), const2),
            pl.BlockSpec((1, 1, d), bvec),
            pl.BlockSpec((1, 1, d), bvec),
            pl.BlockSpec((1, 1, d), bvec),
            pl.BlockSpec((d, LANES), const2),
            pl.BlockSpec((1, LANES), const2),
            pl.BlockSpec((1, d, f), lambda i, j, e: (e, 0, 0)),
            pl.BlockSpec((1, d, f), lambda i, j, e: (e, 0, 0)),
            pl.BlockSpec((1, f, d), lambda i, j, e: (e, 0, 0)),
            pl.BlockSpec((1, d), const2),
        ],
        out_specs=pl.BlockSpec((1, tm, d), row),
        out_shape=jax.ShapeDtypeStruct((b, s, d), F32),
        scratch_shapes=[
            pltpu.VMEM((tm, d), BF16),
            pltpu.VMEM((tm, LANES), F32),
            pltpu.VMEM((tm, d), F32),
        ],
        compiler_params=_cparams(3),
        name="hier_moe",
    )(x, g, shift, scale, gate, wr, br, wg, wu, wd, gfinal)


def _fft1_kernel(u_ref, w_ref, twr_ref, twi_ref, o_ref, *, n_sub, width):
    n1 = FFT_N1
    y = jnp.dot(w_ref[...], u_ref[0], preferred_element_type=F32)
    for j in range(n_sub):
        cols = slice(j * width, (j + 1) * width)
        yr, yi = y[:n1, cols], y[n1:, cols]
        twr, twi = twr_ref[j], twi_ref[j]
        o_ref[0, :n1, cols] = (yr * twr - yi * twi).astype(o_ref.dtype)
        o_ref[0, n1:, cols] = (yr * twi + yi * twr).astype(o_ref.dtype)


def _fft_stage1(u3, w1, twr, twi, width):
    b, n1, cols = u3.shape
    n_sub = 8
    tn = n_sub * width
    kern = functools.partial(_fft1_kernel, n_sub=n_sub, width=width)
    return pl.pallas_call(
        kern,
        grid=(b, cols // tn),
        in_specs=[
            pl.BlockSpec((1, n1, tn), lambda i, j: (i, 0, j)),
            pl.BlockSpec((2 * n1, n1), lambda i, j: (0, 0)),
            pl.BlockSpec((n_sub, n1, 1), lambda i, j: (j, 0, 0)),
            pl.BlockSpec((n_sub, n1, 1), lambda i, j: (j, 0, 0)),
        ],
        out_specs=pl.BlockSpec((1, 2 * n1, tn), lambda i, j: (i, 0, j)),
        out_shape=jax.ShapeDtypeStruct((b, 2 * n1, cols), BF16),
        compiler_params=_cparams(2),
        name="fft_stage1",
    )(u3, w1, twr, twi)


def _fft2_kernel(y_ref, g_ref, o_ref):
    kp, n2 = FFT_K1_PER_STEP, FFT_N2
    width = y_ref.shape[-1]
    y = y_ref[0].reshape(2 * kp * n2, width)
    r = jnp.dot(g_ref[...], y, preferred_element_type=F32)
    o_ref[0] = r.reshape(2, n2, kp, width).astype(o_ref.dtype)


def _fft_stage2(y5, gmat):
    b, _, n1, n2, width = y5.shape
    kp = FFT_K1_PER_STEP
    return pl.pallas_call(
        _fft2_kernel,
        grid=(b, n1 // kp),
        in_specs=[
            pl.BlockSpec((1, 2, kp, n2, width), lambda i, j: (i, 0, j, 0, 0)),
            pl.BlockSpec((2 * kp * n2, 2 * kp * n2), lambda i, j: (0, 0)),
        ],
        out_specs=pl.BlockSpec((1, 2, n2, kp, width), lambda i, j: (i, 0, 0, j, 0)),
        out_shape=jax.ShapeDtypeStruct((b, 2, n2, n1, width), F32),
        compiler_params=_cparams(2),
        name="fft_stage2",
    )(y5, gmat)


def _fmix_out_kernel(x_ref, a_ref, cc_ref, sc_ref, w_ref, gate_ref, o_ref):
    gd = cc_ref.shape[0]
    ar, ai = a_ref[0, 0].astype(BF16), a_ref[0, 1].astype(BF16)
    parts = []
    for g in range(N_FOURIER_GROUPS):
        cols = slice(g * gd, (g + 1) * gd)
        parts.append(jnp.dot(ar[:, cols], cc_ref[...], preferred_element_type=F32)
                     + jnp.dot(ai[:, cols], sc_ref[...], preferred_element_type=F32))
    f = jnp.concatenate(parts, axis=1).astype(BF16)
    y = jnp.dot(f, w_ref[...], preferred_element_type=F32)
    o_ref[0] = x_ref[0] + gate_ref[0] * y


def _fmix_out(x, a4, cc, sc, w_bf16, gate, tm):
    b, s, d = x.shape
    gd = cc.shape[0]
    return pl.pallas_call(
        _fmix_out_kernel,
        grid=(b, s // tm),
        in_specs=[
            pl.BlockSpec((1, tm, d), lambda i, j: (i, j, 0)),
            pl.BlockSpec((1, 2, tm, d), lambda i, j: (i, 0, j, 0)),
            pl.BlockSpec((gd, gd), lambda i, j: (0, 0)),
            pl.BlockSpec((gd, gd), lambda i, j: (0, 0)),
            pl.BlockSpec((d, d), lambda i, j: (0, 0)),
            pl.BlockSpec((1, 1, d), lambda i, j: (i, 0, 0)),
        ],
        out_specs=pl.BlockSpec((1, tm, d), lambda i, j: (i, j, 0)),
        out_shape=jax.ShapeDtypeStruct((b, s, d), F32),
        compiler_params=_cparams(2),
        name="fourier_out",
    )(x, a4, cc, sc, w_bf16, gate)


def _dft_tables(seq, gd):
    n1, n2, kp = FFT_N1, FFT_N2, FFT_K1_PER_STEP

    def cs(num, den):
        ang = (2.0 * math.pi / den) * (num % den).astype(np.float64)
        return np.cos(ang), np.sin(ang)

    i1 = np.arange(n1)
    c1, s1 = cs(np.outer(i1, i1), n1)
    w1 = np.concatenate([c1, -s1], axis=0)
    ct, st = cs(np.outer(np.arange(n2), i1), seq)
    twr, twi = ct[:, :, None], -st[:, :, None]
    i2 = np.arange(n2)
    c2, s2 = cs(np.outer(i2, i2), n2)
    m3 = np.block([[c2, s2], [-s2, c2]])
    m3 = m3.reshape(2, n2, 2, n2)
    gmat = np.zeros((2, n2, kp, 2, kp, n2))
    for j in range(kp):
        gmat[:, :, j, :, j, :] = m3
    gmat = gmat.reshape(2 * n2 * kp, 2 * kp * n2)
    ic = np.arange(gd)
    cc, sc = cs(np.outer(ic, ic), gd)
    norm = 1.0 / math.sqrt(seq * gd)
    return tuple(jnp.asarray(t, F32) for t in (w1, twr, twi, gmat, cc * norm, sc * norm))


def kernel(x, c, ada_w, ada_b, norm_mix, norm_ffn, attn_w_in, attn_lam_q1, attn_lam_k1, attn_lam_q2, attn_lam_k2, attn_subln, attn_w_out, fourier_w_in, fourier_w_out, router_group_w, router_group_b, router_expert_w, router_expert_b, expert_w_gate, expert_w_up, expert_w_down, norm_final):
    b, s, d = x.shape
    depth = ada_w.shape[0]
    assert s == FFT_N1 * FFT_N2 and depth == 2

    c8 = jnp.zeros((8, d), F32).at[:b].set(c)
    mod = _ada(c8, ada_w, ada_b.reshape(depth, 1, N_ADA * d))[:, :b]
    mod = mod.reshape(depth, b, N_ADA, 1, d)

    slopes = (2.0 ** (-8.0 * jnp.arange(1, N_DIFF_HEADS + 1, dtype=F32) / N_DIFF_HEADS)
              ).reshape(N_DIFF_HEADS, 1, 1)
    gd = d // N_FOURIER_GROUPS
    w1, twr, twi, gmat, cc, sc = _dft_tables(s, gd)
    w1, gmat, cc, sc = (t.astype(BF16) for t in (w1, gmat, cc, sc))

    for i in range(depth):
        sh1, sc1, g1, sh2, sc2, g2 = [mod[i, :, k] for k in range(N_ADA)]
        j = i // 2
        if i % 2 == 0:
            qscale = jnp.concatenate([jnp.full((d,), LOG2E * DIFF_HEAD_DIM ** -0.5, F32),
                                      jnp.ones((2 * d,), F32)])
            w_in = (attn_w_in[j] * qscale).astype(BF16)
            qkv = _norm_mod_matmul(x, norm_mix[i].reshape(1, d), sh1, sc1, w_in, tm=512)
            lam_init = 0.8 - 0.6 * math.exp(-0.3 * i)
            o = _diff_attention(qkv, slopes,
                                attn_lam_q1[j].reshape(1, -1), attn_lam_k1[j].reshape(1, -1),
                                attn_lam_q2[j].reshape(1, -1), attn_lam_k2[j].reshape(1, -1),
                                attn_subln[j].reshape(-1, 1), lam_init)
            x = _proj_residual(x, o, attn_w_out[j].astype(BF16), g1, tm=512)
        else:
            u = _norm_mod_matmul(x, norm_mix[i].reshape(1, d), sh1, sc1,
                                 fourier_w_in[j].astype(BF16), tm=512)
            y = _fft_stage1(u.reshape(b, FFT_N1, FFT_N2 * d), w1, twr, twi, d)
            a = _fft_stage2(y.reshape(b, 2, FFT_N1, FFT_N2, d), gmat)
            x = _fmix_out(x, a.reshape(b, 2, s, d), cc, sc, fourier_w_out[j].astype(BF16), g1, tm=512)

        wr = jnp.zeros((d, LANES), F32)
        wr = wr.at[:, :N_EXPERT_GROUPS].set(router_group_w[i])
        wr = wr.at[:, N_EXPERT_GROUPS:N_EXPERT_GROUPS + N_EXPERTS].set(
            router_expert_w[i].reshape(d, N_EXPERTS))
        br = jnp.zeros((1, LANES), F32)
        br = br.at[0, :N_EXPERT_GROUPS].set(router_group_b[i])
        br = br.at[0, N_EXPERT_GROUPS:N_EXPERT_GROUPS + N_EXPERTS].set(
            router_expert_b[i].reshape(N_EXPERTS))
        x = _moe(x, norm_ffn[i].reshape(1, d), sh2, sc2, g2, wr, br,
                 expert_w_gate[i].astype(BF16), expert_w_up[i].astype(BF16),
                 expert_w_down[i].astype(BF16), norm_final.reshape(1, d),
                 final_norm=(i == depth - 1), tm=1024)
    return x
```

```python
import functools
import math

import numpy as np
import jax
import jax.numpy as jnp
from jax import lax
from jax.experimental import pallas as pl
from jax.experimental.pallas import tpu as pltpu

F32 = jnp.float32
BF16 = jnp.bfloat16

EPS = 1e-6
N_DIFF_HEADS = 8
DIFF_HEAD_DIM = 64
DIFF_V_DIM = 128
N_FOURIER_GROUPS = 4
N_EXPERT_GROUPS = 4
EXPERTS_PER_GROUP = 4
N_EXPERTS = 16
N_ADA = 6

LANES = 128
VMEM_LIMIT_BYTES = 56 * 1024 * 1024

FFT_N1 = 128
FFT_N2 = 64
FFT_K1_PER_STEP = 8


def _cparams(n_axes):
    return pltpu.CompilerParams(
        dimension_semantics=("arbitrary",) * n_axes,
        vmem_limit_bytes=VMEM_LIMIT_BYTES,
    )


def _rms_mod(x, g, shift, scale):
    y = x * lax.rsqrt(jnp.mean(x * x, axis=-1, keepdims=True) + EPS)
    return (y * g) * (1.0 + scale) + shift


def _ada_kernel(c_ref, w_ref, b_ref, o_ref):
    c = c_ref[...]
    cond = c / (1.0 + jnp.exp(-c))
    o_ref[0] = jnp.dot(cond, w_ref[0], precision=lax.Precision.HIGHEST,
                       preferred_element_type=F32) + b_ref[0]


def _ada(c8, ada_w, ada_b3):
    depth, d, n = ada_w.shape
    tn = 1536
    return pl.pallas_call(
        _ada_kernel,
        grid=(depth, n // tn),
        in_specs=[
            pl.BlockSpec((8, d), lambda i, j: (0, 0)),
            pl.BlockSpec((1, d, tn), lambda i, j: (i, 0, j)),
            pl.BlockSpec((1, 1, tn), lambda i, j: (i, 0, j)),
        ],
        out_specs=pl.BlockSpec((1, 8, tn), lambda i, j: (i, 0, j)),
        out_shape=jax.ShapeDtypeStruct((depth, 8, n), F32),
        compiler_params=_cparams(2),
        name="ada_mod",
    )(c8, ada_w, ada_b3)


def _nmm_kernel(x_ref, g_ref, sh_ref, sc_ref, w_ref, o_ref):
    h = _rms_mod(x_ref[0], g_ref[...], sh_ref[0], sc_ref[0])
    o_ref[0] = jnp.dot(h.astype(BF16), w_ref[...],
                       preferred_element_type=F32).astype(o_ref.dtype)


def _norm_mod_matmul(x, g, shift, scale, w_bf16, tm):
    b, s, d = x.shape
    n = w_bf16.shape[1]
    return pl.pallas_call(
        _nmm_kernel,
        grid=(b, s // tm),
        in_specs=[
            pl.BlockSpec((1, tm, d), lambda i, j: (i, j, 0)),
            pl.BlockSpec((1, d), lambda i, j: (0, 0)),
            pl.BlockSpec((1, 1, d), lambda i, j: (i, 0, 0)),
            pl.BlockSpec((1, 1, d), lambda i, j: (i, 0, 0)),
            pl.BlockSpec((d, n), lambda i, j: (0, 0)),
        ],
        out_specs=pl.BlockSpec((1, tm, n), lambda i, j: (i, j, 0)),
        out_shape=jax.ShapeDtypeStruct((b, s, n), BF16),
        compiler_params=_cparams(2),
        name="norm_mod_matmul",
    )(x, g, shift, scale, w_bf16)


ATT_TQ = 256
ATT_TK = 512
ATT_PAIR = 2 * ATT_TK
ATT_NQP = ATT_PAIR // ATT_TQ
LOG2E = 1.4426950408889634
ATT_SKIP_BITS = 40.0
ATT_FAST_MAX_U = 36.0


def _attn_bounds_kernel(q_ref, k_ref, slope_ref, lo_ref, hi_ref, fast_ref, ub_ref, *, seq):
    tq, pair = ATT_TQ, ATT_PAIR
    nq = seq // tq
    lane = lax.broadcasted_iota(jnp.int32, (seq, 2 * DIFF_HEAD_DIM), 1)
    first = lane < DIFF_HEAD_DIM

    def max_half_norm2(ref):
        x = ref[0].astype(F32)
        xx = x * x
        n1 = jnp.sum(jnp.where(first, xx, 0.0), axis=1, keepdims=True)
        n2 = jnp.sum(jnp.where(first, 0.0, xx), axis=1, keepdims=True)
        return jnp.maximum(n1, n2)

    qn2 = jnp.max(max_half_norm2(q_ref).reshape(nq, tq, 1), axis=1)
    kn2 = jnp.max(max_half_norm2(k_ref), axis=0, keepdims=True)
    ub = jnp.sqrt(qn2 * kn2) * (1.0 + 2.0 ** -6)
    slope = slope_ref[0] * LOG2E
    dist = (2.0 * ub + (ATT_SKIP_BITS + math.log2(seq))) / slope
    dist = jnp.ceil(jnp.minimum(dist, float(seq))).astype(jnp.int32)
    q0 = lax.broadcasted_iota(jnp.int32, (nq, 1), 0) * tq
    jmin = jnp.maximum(q0 - dist + 1, 0)
    jmax = jnp.minimum(q0 + (tq - 1) + dist - 1, seq - 1)
    shift = jnp.full((nq, 1), int(math.log2(pair)), jnp.int32)
    lo_ref[0, 0] = lax.shift_right_logical(jmin, shift)
    hi_ref[0, 0] = lax.shift_right_logical(jmax, shift) + 1
    fast_ref[0, 0] = (ub <= ATT_FAST_MAX_U).astype(jnp.int32)
    ub_ref[0, 0] = ub


def _attn_bounds(qkv, slopes):
    b, s, _ = qkv.shape
    h, dv = N_DIFF_HEADS, DIFF_V_DIM
    nq = s // ATT_TQ
    kern = functools.partial(_attn_bounds_kernel, seq=s)
    oblk = pl.BlockSpec((1, 1, nq, 1), lambda bi, hi: (bi, hi, 0, 0))
    ishape = jax.ShapeDtypeStruct((b, h, nq, 1), jnp.int32)
    return pl.pallas_call(
        kern,
        grid=(b, h),
        in_specs=[
            pl.BlockSpec((1, s, dv), lambda bi, hi: (bi, 0, hi)),
            pl.BlockSpec((1, s, dv), lambda bi, hi: (bi, 0, h + hi)),
            pl.BlockSpec((1, 1, 1), lambda bi, hi: (hi, 0, 0)),
        ],
        out_specs=[oblk, oblk, oblk, oblk],
        out_shape=[ishape, ishape, ishape, jax.ShapeDtypeStruct((b, h, nq, 1), F32)],
        compiler_params=_cparams(2),
        name="attn_bounds",
    )(qkv, qkv, slopes)


def _attn_kernel(lo_ref, hi_ref, fast_ref, q_ref, k_ref, v_ref, ub_ref, slope_ref,
                 lq1_ref, lk1_ref, lq2_ref, lk2_ref, sub_ref,
                 o_ref, vt_ref, bias_ref, u_ref, m_ref, l_ref, acc_ref, *, lam_init, seq):
    tq, tk, pair, nqp = ATT_TQ, ATT_TK, ATT_PAIR, ATT_NQP
    dh = DIFF_HEAD_DIM
    qi = pl.program_id(2)
    tile = (pl.program_id(0) * N_DIFF_HEADS + pl.program_id(1)) * (seq // tq) + qi
    lo, hi = lo_ref[tile], hi_ref[tile]
    slope = slope_ref[0] * LOG2E

    @pl.when(qi == 0)
    def _():
        for c in range(seq // 512):
            vt_ref[:, c * 512:(c + 1) * 512] = v_ref[0, c * 512:(c + 1) * 512, :].T
        for s in range(2):
            rows = slice(s * tk, (s + 1) * tk)
            r = lax.broadcasted_iota(jnp.int32, (tk, 2 * tq), 0) + s * tk
            c = lax.broadcasted_iota(jnp.int32, (tk, 2 * tq), 1)
            c = jnp.where(c >= tq, c - tq, c)
            rc = (r - c).astype(F32)
            bias_ref[0, rows, :] = slope * rc
            bias_ref[1, rows, :] = -(slope * rc)
            for w in range(nqp):
                bias_ref[2 + w, rows, :] = -(slope * jnp.abs(rc - float(w * tq)))

    qt = q_ref[0].T
    z = jnp.zeros((dh, tq), BF16)
    qbd = jnp.concatenate([jnp.concatenate([qt[:dh], z], axis=1),
                           jnp.concatenate([z, qt[dh:]], axis=1)], axis=0)

    l_ref[...] = jnp.zeros(l_ref.shape, F32)
    acc_ref[...] = jnp.zeros(acc_ref.shape, F32)

    kd = qi // nqp
    w = qi % nqp

    def mode(kp):
        delta = (kp * pair - qi * tq).astype(F32)
        idx = jnp.where(kp < kd, 0, jnp.where(kp > kd, 1, 2 + w))
        sgn = jnp.where(kp < kd, delta, jnp.where(kp > kd, -delta, 0.0))
        return idx, slope * sgn

    def scores(kp, idx, s):
        k0 = pl.multiple_of(kp * pair + s * tk, tk)
        kblk = k_ref[0, pl.ds(k0, tk), :]
        u = jnp.dot(kblk, qbd, preferred_element_type=F32)
        return u + bias_ref[idx, s * tk:(s + 1) * tk, :]

    def values(kp, s):
        k0 = pl.multiple_of(kp * pair + s * tk, tk)
        return vt_ref[:, pl.ds(k0, tk)]

    def pipelined(qk_phase, sm_phase):
        idx0, _ = mode(lo)
        carry0 = qk_phase(lo, idx0, 0)

        def body(kp, carry_a):
            idx, cst = mode(kp)
            carry_b = qk_phase(kp, idx, 1)
            sm_phase(kp, cst, 0, carry_a)
            idx_n, _ = mode(kp + 1)
            carry_a_next = qk_phase(kp + 1, idx_n, 0)
            sm_phase(kp, cst, 1, carry_b)
            return carry_a_next

        carry_a = lax.fori_loop(lo, hi - 1, body, carry0)
        idx_l, cst_l = mode(hi - 1)
        carry_b = qk_phase(hi - 1, idx_l, 1)
        sm_phase(hi - 1, cst_l, 0, carry_a)
        sm_phase(hi - 1, cst_l, 1, carry_b)

    def fixed_reference():
        ub = ub_ref[0]

        def qk_phase(kp, idx, s):
            u_ref[s] = scores(kp, idx, s)
            return 0

        def sm_phase(kp, cst, s, _):
            p = jnp.exp2(u_ref[s] + (cst - ub))
            l_ref[...] += jnp.sum(p, axis=0, keepdims=True)
            acc_ref[...] += jnp.dot(values(kp, s), p.astype(BF16), preferred_element_type=F32)

        pipelined(qk_phase, sm_phase)

    def running_max():
        m_ref[...] = jnp.full(m_ref.shape, -jnp.inf, F32)

        def qk_phase(kp, idx, s):
            v = scores(kp, idx, s)
            u_ref[s] = v
            return jnp.max(v, axis=0, keepdims=True)

        def sm_phase(kp, cst, s, mx):
            m_old = m_ref[...]
            m_new = jnp.maximum(m_old, mx + cst)
            alpha = jnp.exp2(m_old - m_new)
            p = jnp.exp2(u_ref[s] - (m_new - cst))
            l_ref[...] = alpha * l_ref[...] + jnp.sum(p, axis=0, keepdims=True)
            m_ref[...] = m_new
            acc_ref[...] = alpha * acc_ref[...] + jnp.dot(values(kp, s), p.astype(BF16),
                                                          preferred_element_type=F32)

        pipelined(qk_phase, sm_phase)

    lax.cond(fast_ref[tile] == 1, fixed_reference, running_max)

    lam = (jnp.exp(jnp.sum(lq1_ref[...] * lk1_ref[...], axis=-1, keepdims=True))
           - jnp.exp(jnp.sum(lq2_ref[...] * lk2_ref[...], axis=-1, keepdims=True))
           + lam_init)
    acc = acc_ref[...]
    l = l_ref[...]
    ot = acc[:, :tq] / l[:, :tq] - lam * (acc[:, tq:] / l[:, tq:])
    ms = jnp.mean(ot * ot, axis=0, keepdims=True)
    ot = ot * lax.rsqrt(ms + EPS) * sub_ref[...] * (1.0 - lam_init)
    o_ref[0] = ot.T.astype(o_ref.dtype)


def _diff_attention(qkv, slopes, lq1, lk1, lq2, lk2, subln_col, lam_init):
    b, s, _ = qkv.shape
    h, dv = N_DIFF_HEADS, DIFF_V_DIM
    tq, tk = ATT_TQ, ATT_TK
    nq = s // tq
    assert s % ATT_PAIR == 0
    lo, hi, fast, ub = _attn_bounds(qkv, slopes)
    kern = functools.partial(_attn_kernel, lam_init=lam_init, seq=s)
    vec = pl.BlockSpec((1, DIFF_HEAD_DIM), lambda bi, hi, qi, *_: (0, 0))
    grid_spec = pltpu.PrefetchScalarGridSpec(
        num_scalar_prefetch=3,
        grid=(b, h, nq),
        in_specs=[
            pl.BlockSpec((1, tq, dv), lambda bi, hi, qi, *_: (bi, qi, hi)),
            pl.BlockSpec((1, s, dv), lambda bi, hi, qi, *_: (bi, 0, h + hi)),
            pl.BlockSpec((1, s, dv), lambda bi, hi, qi, *_: (bi, 0, 2 * h + hi)),
            pl.BlockSpec((1, 1, 1), lambda bi, hi, qi, *_: ((bi * h + hi) * nq + qi, 0, 0)),
            pl.BlockSpec((1, 1, 1), lambda bi, hi, qi, *_: (hi, 0, 0)),
            vec, vec, vec, vec,
            pl.BlockSpec((dv, 1), lambda bi, hi, qi, *_: (0, 0)),
        ],
        out_specs=pl.BlockSpec((1, tq, dv), lambda bi, hi, qi, *_: (bi, qi, hi)),
        scratch_shapes=[
            pltpu.VMEM((dv, s), BF16),
            pltpu.VMEM((2 + ATT_NQP, ATT_PAIR, 2 * tq), F32),
            pltpu.VMEM((2, tk, 2 * tq), F32),
            pltpu.VMEM((1, 2 * tq), F32),
            pltpu.VMEM((1, 2 * tq), F32),
            pltpu.VMEM((dv, 2 * tq), F32),
        ],
    )
    return pl.pallas_call(
        kern,
        grid_spec=grid_spec,
        out_shape=jax.ShapeDtypeStruct((b, s, h * dv), BF16),
        compiler_params=_cparams(3),
        name="diff_attention",
    )(lo.reshape(-1), hi.reshape(-1), fast.reshape(-1),
      qkv, qkv, qkv, ub.reshape(-1, 1, 1), slopes, lq1, lk1, lq2, lk2, subln_col)


def _proj_res_kernel(x_ref, a_ref, w_ref, gate_ref, o_ref):
    y = jnp.dot(a_ref[0], w_ref[...], preferred_element_type=F32)
    o_ref[0] = x_ref[0] + gate_ref[0] * y


def _proj_residual(x, a, w_bf16, gate, tm):
    b, s, d = x.shape
    kdim = a.shape[-1]
    return pl.pallas_call(
        _proj_res_kernel,
        grid=(b, s // tm),
        in_specs=[
            pl.BlockSpec((1, tm, d), lambda i, j: (i, j, 0)),
            pl.BlockSpec((1, tm, kdim), lambda i, j: (i, j, 0)),
            pl.BlockSpec((kdim, d), lambda i, j: (0, 0)),
            pl.BlockSpec((1, 1, d), lambda i, j: (i, 0, 0)),
        ],
        out_specs=pl.BlockSpec((1, tm, d), lambda i, j: (i, j, 0)),
        out_shape=jax.ShapeDtypeStruct((b, s, d), F32),
        compiler_params=_cparams(2),
        name="proj_residual",
    )(x, a, w_bf16, gate)


def _routing_gates(logits):
    ng, ne = N_EXPERT_GROUPS, EXPERTS_PER_GROUP
    gl = [logits[:, g:g + 1] for g in range(ng)]
    gmax = functools.reduce(jnp.maximum, gl)
    den = functools.reduce(lambda a, b: a + b, [jnp.exp(v - gmax) for v in gl])
    g_w = 1.0 / den
    sel = []
    taken = jnp.zeros_like(gmax, dtype=jnp.bool_)
    for g in range(ng):
        s = jnp.logical_and(gl[g] == gmax, jnp.logical_not(taken))
        taken = jnp.logical_or(taken, s)
        sel.append(s)
    ev = []
    for e in range(ne):
        v = jnp.zeros_like(gmax)
        for g in range(ng):
            col = ng + ne * g + e
            v = jnp.where(sel[g], logits[:, col:col + 1], v)
        ev.append(v)
    top1 = functools.reduce(jnp.maximum, ev)
    is1 = []
    taken = jnp.zeros_like(gmax, dtype=jnp.bool_)
    for e in range(ne):
        s = jnp.logical_and(ev[e] == top1, jnp.logical_not(taken))
        taken = jnp.logical_or(taken, s)
        is1.append(s)
    rest = [jnp.where(is1[e], -jnp.inf, ev[e]) for e in range(ne)]
    top2 = functools.reduce(jnp.maximum, rest)
    is2 = []
    taken = jnp.zeros_like(gmax, dtype=jnp.bool_)
    for e in range(ne):
        s = jnp.logical_and(jnp.logical_and(rest[e] == top2, jnp.logical_not(is1[e])),
                            jnp.logical_not(taken))
        taken = jnp.logical_or(taken, s)
        is2.append(s)
    ex = jnp.exp(top2 - top1)
    w1 = 1.0 / (1.0 + ex)
    w2 = ex / (1.0 + ex)
    lane = lax.broadcasted_iota(jnp.int32, logits.shape, 1)
    gates = jnp.zeros(logits.shape, F32)
    for g in range(ng):
        for e in range(ne):
            ew = jnp.where(is1[e], w1, jnp.where(is2[e], w2, 0.0))
            val = jnp.where(sel[g], g_w * ew, 0.0)
            gates = jnp.where(lane == ne * g + e, val, gates)
    return gates


def _moe_kernel(x_ref, g_ref, sh_ref, sc_ref, gate_ref, wr_ref, br_ref, wg_ref, wu_ref, wd_ref,
                gf_ref, o_ref, h_ref, gates_ref, acc_ref, *, final_norm):
    e = pl.program_id(2)

    @pl.when(e == 0)
    def _():
        h32 = _rms_mod(x_ref[0], g_ref[...], sh_ref[0], sc_ref[0])
        h_ref[...] = h32.astype(BF16)
        logits = jnp.dot(h32, wr_ref[...], precision=lax.Precision.HIGHEST,
                         preferred_element_type=F32) + br_ref[...]
        gates_ref[...] = _routing_gates(logits)
        acc_ref[...] = jnp.zeros(acc_ref.shape, F32)

    h = h_ref[...]
    lane = lax.broadcasted_iota(jnp.int32, gates_ref.shape, 1)
    gcol = jnp.sum(jnp.where(lane == e, gates_ref[...], 0.0), axis=1, keepdims=True)
    a = jnp.dot(h, wg_ref[0], preferred_element_type=F32)
    u = jnp.dot(h, wu_ref[0], preferred_element_type=F32)
    hid = (a / (1.0 + jnp.exp(-a))) * u * gcol
    acc_ref[...] += jnp.dot(hid.astype(BF16), wd_ref[0], preferred_element_type=F32)

    @pl.when(e == N_EXPERTS - 1)
    def _():
        y = x_ref[0] + gate_ref[0] * acc_ref[...]
        if final_norm:
            y = y * lax.rsqrt(jnp.mean(y * y, axis=-1, keepdims=True) + EPS) * gf_ref[...]
        o_ref[0] = y


def _moe(x, g, shift, scale, gate, wr, br, wg, wu, wd, gfinal, final_norm, tm):
    b, s, d = x.shape
    f = wg.shape[-1]
    kern = functools.partial(_moe_kernel, final_norm=final_norm)
    row = lambda i, j, e: (i, j, 0)
    bvec = lambda i, j, e: (i, 0, 0)
    const2 = lambda i, j, e: (0, 0)
    return pl.pallas_call(
        kern,
        grid=(b, s // tm, N_EXPERTS),
        in_specs=[
            pl.BlockSpec((1, tm, d), row),
            pl.BlockSpec((1, d), const2),
            pl.BlockSpec((1, 1, d), bvec),
            pl.BlockSpec((1, 1, d), bvec),
            pl.BlockSpec((1, 1, d), bvec),
            pl.BlockSpec((d, LANES), const2),
            pl.BlockSpec((1, LANES), const2),
            pl.BlockSpec((1, d, f), lambda i, j, e: (e, 0, 0)),
            pl.BlockSpec((1, d, f), lambda i, j, e: (e, 0, 0)),
            pl.BlockSpec((1, f, d), lambda i, j, e: (e, 0, 0)),
            pl.BlockSpec((1, d), const2),
        ],
        out_specs=pl.BlockSpec((1, tm, d), row),
        out_shape=jax.ShapeDtypeStruct((b, s, d), F32),
        scratch_shapes=[
            pltpu.VMEM((tm, d), BF16),
            pltpu.VMEM((tm, LANES), F32),
            pltpu.VMEM((tm, d), F32),
        ],
        compiler_params=_cparams(3),
        name="hier_moe",
    )(x, g, shift, scale, gate, wr, br, wg, wu, wd, gfinal)


def _fft1_kernel(u_ref, w_ref, twr_ref, twi_ref, o_ref, *, n_sub, width):
    n1 = FFT_N1
    y = jnp.dot(w_ref[...], u_ref[0], preferred_element_type=F32)
    for j in range(n_sub):
        cols = slice(j * width, (j + 1) * width)
        yr, yi = y[:n1, cols], y[n1:, cols]
        twr, twi = twr_ref[j], twi_ref[j]
        o_ref[0, :n1, cols] = (yr * twr - yi * twi).astype(o_ref.dtype)
        o_ref[0, n1:, cols] = (yr * twi + yi * twr).astype(o_ref.dtype)


def _fft_stage1(u3, w1, twr, twi, width):
    b, n1, cols = u3.shape
    n_sub = 8
    tn = n_sub * width
    kern = functools.partial(_fft1_kernel, n_sub=n_sub, width=width)
    return pl.pallas_call(
        kern,
        grid=(b, cols // tn),
        in_specs=[
            pl.BlockSpec((1, n1, tn), lambda i, j: (i, 0, j)),
            pl.BlockSpec((2 * n1, n1), lambda i, j: (0, 0)),
            pl.BlockSpec((n_sub, n1, 1), lambda i, j: (j, 0, 0)),
            pl.BlockSpec((n_sub, n1, 1), lambda i, j: (j, 0, 0)),
        ],
        out_specs=pl.BlockSpec((1, 2 * n1, tn), lambda i, j: (i, 0, j)),
        out_shape=jax.ShapeDtypeStruct((b, 2 * n1, cols), BF16),
        compiler_params=_cparams(2),
        name="fft_stage1",
    )(u3, w1, twr, twi)


def _fft2_kernel(y_ref, g_ref, o_ref):
    kp, n2 = FFT_K1_PER_STEP, FFT_N2
    width = y_ref.shape[-1]
    y = y_ref[0].reshape(2 * kp * n2, width)
    r = jnp.dot(g_ref[...], y, preferred_element_type=F32)
    o_ref[0] = r.reshape(2, n2, kp, width).astype(o_ref.dtype)


def _fft_stage2(y5, gmat):
    b, _, n1, n2, width = y5.shape
    kp = FFT_K1_PER_STEP
    return pl.pallas_call(
        _fft2_kernel,
        grid=(b, n1 // kp),
        in_specs=[
            pl.BlockSpec((1, 2, kp, n2, width), lambda i, j: (i, 0, j, 0, 0)),
            pl.BlockSpec((2 * kp * n2, 2 * kp * n2), lambda i, j: (0, 0)),
        ],
        out_specs=pl.BlockSpec((1, 2, n2, kp, width), lambda i, j: (i, 0, 0, j, 0)),
        out_shape=jax.ShapeDtypeStruct((b, 2, n2, n1, width), F32),
        compiler_params=_cparams(2),
        name="fft_stage2",
    )(y5, gmat)


def _fmix_out_kernel(x_ref, a_ref, cc_ref, sc_ref, w_ref, gate_ref, o_ref):
    gd = cc_ref.shape[0]
    ar, ai = a_ref[0, 0].astype(BF16), a_ref[0, 1].astype(BF16)
    parts = []
    for g in range(N_FOURIER_GROUPS):
        cols = slice(g * gd, (g + 1) * gd)
        parts.append(jnp.dot(ar[:, cols], cc_ref[...], preferred_element_type=F32)
                     + jnp.dot(ai[:, cols], sc_ref[...], preferred_element_type=F32))
    f = jnp.concatenate(parts, axis=1).astype(BF16)
    y = jnp.dot(f, w_ref[...], preferred_element_type=F32)
    o_ref[0] = x_ref[0] + gate_ref[0] * y


def _fmix_out(x, a4, cc, sc, w_bf16, gate, tm):
    b, s, d = x.shape
    gd = cc.shape[0]
    return pl.pallas_call(
        _fmix_out_kernel,
        grid=(b, s // tm),
        in_specs=[
            pl.BlockSpec((1, tm, d), lambda i, j: (i, j, 0)),
            pl.BlockSpec((1, 2, tm, d), lambda i, j: (i, 0, j, 0)),
            pl.BlockSpec((gd, gd), lambda i, j: (0, 0)),
            pl.BlockSpec((gd, gd), lambda i, j: (0, 0)),
            pl.BlockSpec((d, d), lambda i, j: (0, 0)),
            pl.BlockSpec((1, 1, d), lambda i, j: (i, 0, 0)),
        ],
        out_specs=pl.BlockSpec((1, tm, d), lambda i, j: (i, j, 0)),
        out_shape=jax.ShapeDtypeStruct((b, s, d), F32),
        compiler_params=_cparams(2),
        name="fourier_out",
    )(x, a4, cc, sc, w_bf16, gate)


def _dft_tables(seq, gd):
    n1, n2, kp = FFT_N1, FFT_N2, FFT_K1_PER_STEP

    def cs(num, den):
        ang = (2.0 * math.pi / den) * (num % den).astype(np.float64)
        return np.cos(ang), np.sin(ang)

    i1 = np.arange(n1)
    c1, s1 = cs(np.outer(i1, i1), n1)
    w1 = np.concatenate([c1, -s1], axis=0)
    ct, st = cs(np.outer(np.arange(n2), i1), seq)
    twr, twi = ct[:, :, None], -st[:, :, None]
    i2 = np.arange(n2)
    c2, s2 = cs(np.outer(i2, i2), n2)
    m3 = np.block([[c2, s2], [-s2, c2]])
    m3 = m3.reshape(2, n2, 2, n2)
    gmat = np.zeros((2, n2, kp, 2, kp, n2))
    for j in range(kp):
        gmat[:, :, j, :, j, :] = m3
    gmat = gmat.reshape(2 * n2 * kp, 2 * kp * n2)
    ic = np.arange(gd)
    cc, sc = cs(np.outer(ic, ic), gd)
    norm = 1.0 / math.sqrt(seq * gd)
    return tuple(jnp.asarray(t, F32) for t in (w1, twr, twi, gmat, cc * norm, sc * norm))


def kernel(x, c, ada_w, ada_b, norm_mix, norm_ffn, attn_w_in, attn_lam_q1, attn_lam_k1, attn_lam_q2, attn_lam_k2, attn_subln, attn_w_out, fourier_w_in, fourier_w_out, router_group_w, router_group_b, router_expert_w, router_expert_b, expert_w_gate, expert_w_up, expert_w_down, norm_final):
    b, s, d = x.shape
    depth = ada_w.shape[0]
    assert s == FFT_N1 * FFT_N2 and depth == 2

    c8 = jnp.zeros((8, d), F32).at[:b].set(c)
    mod = _ada(c8, ada_w, ada_b.reshape(depth, 1, N_ADA * d))[:, :b]
    mod = mod.reshape(depth, b, N_ADA, 1, d)

    slopes = (2.0 ** (-8.0 * jnp.arange(1, N_DIFF_HEADS + 1, dtype=F32) / N_DIFF_HEADS)
              ).reshape(N_DIFF_HEADS, 1, 1)
    gd = d // N_FOURIER_GROUPS
    w1, twr, twi, gmat, cc, sc = _dft_tables(s, gd)
    w1, gmat, cc, sc = (t.astype(BF16) for t in (w1, gmat, cc, sc))

    for i in range(depth):
        sh1, sc1, g1, sh2, sc2, g2 = [mod[i, :, k] for k in range(N_ADA)]
        j = i // 2
        if i % 2 == 0:
            qscale = jnp.concatenate([jnp.full((d,), LOG2E * DIFF_HEAD_DIM ** -0.5, F32),
                                      jnp.ones((2 * d,), F32)])
            w_in = (attn_w_in[j] * qscale).astype(BF16)
            qkv = _norm_mod_matmul(x, norm_mix[i].reshape(1, d), sh1, sc1, w_in, tm=512)
            lam_init = 0.8 - 0.6 * math.exp(-0.3 * i)
            o = _diff_attention(qkv, slopes,
                                attn_lam_q1[j].reshape(1, -1), attn_lam_k1[j].reshape(1, -1),
                                attn_lam_q2[j].reshape(1, -1), attn_lam_k2[j].reshape(1, -1),
                                attn_subln[j].reshape(-1, 1), lam_init)
            x = _proj_residual(x, o, attn_w_out[j].astype(BF16), g1, tm=512)
        else:
            u = _norm_mod_matmul(x, norm_mix[i].reshape(1, d), sh1, sc1,
                                 fourier_w_in[j].astype(BF16), tm=512)
            y = _fft_stage1(u.reshape(b, FFT_N1, FFT_N2 * d), w1, twr, twi, d)
            a = _fft_stage2(y.reshape(b, 2, FFT_N1, FFT_N2, d), gmat)
            x = _fmix_out(x, a.reshape(b, 2, s, d), cc, sc, fourier_w_out[j].astype(BF16), g1, tm=512)

        wr = jnp.zeros((d, LANES), F32)
        wr = wr.at[:, :N_EXPERT_GROUPS].set(router_group_w[i])
        wr = wr.at[:, N_EXPERT_GROUPS:N_EXPERT_GROUPS + N_EXPERTS].set(
            router_expert_w[i].reshape(d, N_EXPERTS))
        br = jnp.zeros((1, LANES), F32)
        br = br.at[0, :N_EXPERT_GROUPS].set(router_group_b[i])
        br = br.at[0, N_EXPERT_GROUPS:N_EXPERT_GROUPS + N_EXPERTS].set(
            router_expert_b[i].reshape(N_EXPERTS))
        x = _moe(x, norm_ffn[i].reshape(1, d), sh2, sc2, g2, wr, br,
                 expert_w_gate[i].astype(BF16), expert_w_up[i].astype(BF16),
                 expert_w_down[i].astype(BF16), norm_final.reshape(1, d),
                 final_norm=(i == depth - 1), tm=1024)
    return x
```

```python
import functools
import math

import numpy as np
import jax
import jax.numpy as jnp
from jax import lax
from jax.experimental import pallas as pl
from jax.experimental.pallas import tpu as pltpu

F32 = jnp.float32
BF16 = jnp.bfloat16

EPS = 1e-6
N_DIFF_HEADS = 8
DIFF_HEAD_DIM = 64
DIFF_V_DIM = 128
N_FOURIER_GROUPS = 4
N_EXPERT_GROUPS = 4
EXPERTS_PER_GROUP = 4
N_EXPERTS = 16
N_ADA = 6

LANES = 128
VMEM_LIMIT_BYTES = 56 * 1024 * 1024

FFT_N1 = 128
FFT_N2 = 64
FFT_K1_PER_STEP = 8


def _cparams(n_axes):
    return pltpu.CompilerParams(
        dimension_semantics=("arbitrary",) * n_axes,
        vmem_limit_bytes=VMEM_LIMIT_BYTES,
    )


def _rms_mod(x, g, shift, scale):
    y = x * lax.rsqrt(jnp.mean(x * x, axis=-1, keepdims=True) + EPS)
    return (y * g) * (1.0 + scale) + shift


def _ada_kernel(c_ref, w_ref, b_ref, o_ref):
    c = c_ref[...]
    cond = c / (1.0 + jnp.exp(-c))
    o_ref[0] = jnp.dot(cond, w_ref[0], precision=lax.Precision.HIGHEST,
                       preferred_element_type=F32) + b_ref[0]


def _ada(c8, ada_w, ada_b3):
    depth, d, n = ada_w.shape
    tn = 1536
    return pl.pallas_call(
        _ada_kernel,
        grid=(depth, n // tn),
        in_specs=[
            pl.BlockSpec((8, d), lambda i, j: (0, 0)),
            pl.BlockSpec((1, d, tn), lambda i, j: (i, 0, j)),
            pl.BlockSpec((1, 1, tn), lambda i, j: (i, 0, j)),
        ],
        out_specs=pl.BlockSpec((1, 8, tn), lambda i, j: (i, 0, j)),
        out_shape=jax.ShapeDtypeStruct((depth, 8, n), F32),
        compiler_params=_cparams(2),
        name="ada_mod",
    )(c8, ada_w, ada_b3)


def _nmm_kernel(x_ref, g_ref, sh_ref, sc_ref, w_ref, o_ref):
    h = _rms_mod(x_ref[0], g_ref[...], sh_ref[0], sc_ref[0])
    o_ref[0] = jnp.dot(h.astype(BF16), w_ref[...],
                       preferred_element_type=F32).astype(o_ref.dtype)


def _norm_mod_matmul(x, g, shift, scale, w_bf16, tm):
    b, s, d = x.shape
    n = w_bf16.shape[1]
    return pl.pallas_call(
        _nmm_kernel,
        grid=(b, s // tm),
        in_specs=[
            pl.BlockSpec((1, tm, d), lambda i, j: (i, j, 0)),
            pl.BlockSpec((1, d), lambda i, j: (0, 0)),
            pl.BlockSpec((1, 1, d), lambda i, j: (i, 0, 0)),
            pl.BlockSpec((1, 1, d), lambda i, j: (i, 0, 0)),
            pl.BlockSpec((d, n), lambda i, j: (0, 0)),
        ],
        out_specs=pl.BlockSpec((1, tm, n), lambda i, j: (i, j, 0)),
        out_shape=jax.ShapeDtypeStruct((b, s, n), BF16),
        compiler_params=_cparams(2),
        name="norm_mod_matmul",
    )(x, g, shift, scale, w_bf16)


ATT_TQ = 256
ATT_TK = 512
ATT_PAIR = 2 * ATT_TK
ATT_NQP = ATT_PAIR // ATT_TQ
LOG2E = 1.4426950408889634
ATT_SKIP_BITS = 40.0
ATT_FAST_MAX_U = 36.0


def _attn_bounds_kernel(q_ref, k_ref, slope_ref, lo_ref, hi_ref, fast_ref, ub_ref, *, seq):
    tq, pair = ATT_TQ, ATT_PAIR
    nq = seq // tq
    lane = lax.broadcasted_iota(jnp.int32, (seq, 2 * DIFF_HEAD_DIM), 1)
    first = lane < DIFF_HEAD_DIM

    def max_half_norm2(ref):
        x = ref[0].astype(F32)
        xx = x * x
        n1 = jnp.sum(jnp.where(first, xx, 0.0), axis=1, keepdims=True)
        n2 = jnp.sum(jnp.where(first, 0.0, xx), axis=1, keepdims=True)
        return jnp.maximum(n1, n2)

    qn2 = jnp.max(max_half_norm2(q_ref).reshape(nq, tq, 1), axis=1)
    kn2 = jnp.max(max_half_norm2(k_ref), axis=0, keepdims=True)
    ub = jnp.sqrt(qn2 * kn2) * (1.0 + 2.0 ** -6)
    slope = slope_ref[0] * LOG2E
    dist = (2.0 * ub + (ATT_SKIP_BITS + math.log2(seq))) / slope
    dist = jnp.ceil(jnp.minimum(dist, float(seq))).astype(jnp.int32)
    q0 = lax.broadcasted_iota(jnp.int32, (nq, 1), 0) * tq
    jmin = jnp.maximum(q0 - dist + 1, 0)
    jmax = jnp.minimum(q0 + (tq - 1) + dist - 1, seq - 1)
    shift = jnp.full((nq, 1), int(math.log2(pair)), jnp.int32)
    lo_ref[0, 0] = lax.shift_right_logical(jmin, shift)
    hi_ref[0, 0] = lax.shift_right_logical(jmax, shift) + 1
    fast_ref[0, 0] = (ub <= ATT_FAST_MAX_U).astype(jnp.int32)
    ub_ref[0, 0] = ub


def _attn_bounds(qkv, slopes):
    b, s, _ = qkv.shape
    h, dv = N_DIFF_HEADS, DIFF_V_DIM
    nq = s // ATT_TQ
    kern = functools.partial(_attn_bounds_kernel, seq=s)
    oblk = pl.BlockSpec((1, 1, nq, 1), lambda bi, hi: (bi, hi, 0, 0))
    ishape = jax.ShapeDtypeStruct((b, h, nq, 1), jnp.int32)
    return pl.pallas_call(
        kern,
        grid=(b, h),
        in_specs=[
            pl.BlockSpec((1, s, dv), lambda bi, hi: (bi, 0, hi)),
            pl.BlockSpec((1, s, dv), lambda bi, hi: (bi, 0, h + hi)),
            pl.BlockSpec((1, 1, 1), lambda bi, hi: (hi, 0, 0)),
        ],
        out_specs=[oblk, oblk, oblk, oblk],
        out_shape=[ishape, ishape, ishape, jax.ShapeDtypeStruct((b, h, nq, 1), F32)],
        compiler_params=_cparams(2),
        name="attn_bounds",
    )(qkv, qkv, slopes)


def _attn_kernel(lo_ref, hi_ref, fast_ref, q_ref, k_ref, v_ref, ub_ref, slope_ref,
                 lq1_ref, lk1_ref, lq2_ref, lk2_ref, sub_ref,
                 o_ref, vt_ref, bias_ref, u_ref, m_ref, l_ref, acc_ref, *, lam_init, seq):
    tq, tk, pair, nqp = ATT_TQ, ATT_TK, ATT_PAIR, ATT_NQP
    dh = DIFF_HEAD_DIM
    qi = pl.program_id(2)
    tile = (pl.program_id(0) * N_DIFF_HEADS + pl.program_id(1)) * (seq // tq) + qi
    lo, hi = lo_ref[tile], hi_ref[tile]
    slope = slope_ref[0] * LOG2E

    @pl.when(qi == 0)
    def _():
        for c in range(seq // 512):
            vt_ref[:, c * 512:(c + 1) * 512] = v_ref[0, c * 512:(c + 1) * 512, :].T
        for s in range(2):
            rows = slice(s * tk, (s + 1) * tk)
            r = lax.broadcasted_iota(jnp.int32, (tk, 2 * tq), 0) + s * tk
            c = lax.broadcasted_iota(jnp.int32, (tk, 2 * tq), 1)
            c = jnp.where(c >= tq, c - tq, c)
            rc = (r - c).astype(F32)
            bias_ref[0, rows, :] = slope * rc
            bias_ref[1, rows, :] = -(slope * rc)
            for w in range(nqp):
                bias_ref[2 + w, rows, :] = -(slope * jnp.abs(rc - float(w * tq)))

    qt = q_ref[0].T
    z = jnp.zeros((dh, tq), BF16)
    qbd = jnp.concatenate([jnp.concatenate([qt[:dh], z], axis=1),
                           jnp.concatenate([z, qt[dh:]], axis=1)], axis=0)

    l_ref[...] = jnp.zeros(l_ref.shape, F32)
    acc_ref[...] = jnp.zeros(acc_ref.shape, F32)

    kd = qi // nqp
    w = qi % nqp

    def mode(kp):
        delta = (kp * pair - qi * tq).astype(F32)
        idx = jnp.where(kp < kd, 0, jnp.where(kp > kd, 1, 2 + w))
        sgn = jnp.where(kp < kd, delta, jnp.where(kp > kd, -delta, 0.0))
        return idx, slope * sgn

    def scores(kp, idx, s):
        k0 = pl.multiple_of(kp * pair + s * tk, tk)
        kblk = k_ref[0, pl.ds(k0, tk), :]
        u = jnp.dot(kblk, qbd, preferred_element_type=F32)
        return u + bias_ref[idx, s * tk:(s + 1) * tk, :]

    def values(kp, s):
        k0 = pl.multiple_of(kp * pair + s * tk, tk)
        return vt_ref[:, pl.ds(k0, tk)]

    def pipelined(qk_phase, sm_phase):
        idx0, _ = mode(lo)
        carry0 = qk_phase(lo, idx0, 0)

        def body(kp, carry_a):
            idx, cst = mode(kp)
            carry_b = qk_phase(kp, idx, 1)
            sm_phase(kp, cst, 0, carry_a)
            idx_n, _ = mode(kp + 1)
            carry_a_next = qk_phase(kp + 1, idx_n, 0)
            sm_phase(kp, cst, 1, carry_b)
            return carry_a_next

        carry_a = lax.fori_loop(lo, hi - 1, body, carry0)
        idx_l, cst_l = mode(hi - 1)
        carry_b = qk_phase(hi - 1, idx_l, 1)
        sm_phase(hi - 1, cst_l, 0, carry_a)
        sm_phase(hi - 1, cst_l, 1, carry_b)

    def fixed_reference():
        ub = ub_ref[0]

        def qk_phase(kp, idx, s):
            u_ref[s] = scores(kp, idx, s)
            return 0

        def sm_phase(kp, cst, s, _):
            p = jnp.exp2(u_ref[s] + (cst - ub))
            l_ref[...] += jnp.sum(p, axis=0, keepdims=True)
            acc_ref[...] += jnp.dot(values(kp, s), p.astype(BF16), preferred_element_type=F32)

        pipelined(qk_phase, sm_phase)

    def running_max():
        m_ref[...] = jnp.full(m_ref.shape, -jnp.inf, F32)

        def qk_phase(kp, idx, s):
            v = scores(kp, idx, s)
            u_ref[s] = v
            return jnp.max(v, axis=0, keepdims=True)

        def sm_phase(kp, cst, s, mx):
            m_old = m_ref[...]
            m_new = jnp.maximum(m_old, mx + cst)
            alpha = jnp.exp2(m_old - m_new)
            p = jnp.exp2(u_ref[s] - (m_new - cst))
            l_ref[...] = alpha * l_ref[...] + jnp.sum(p, axis=0, keepdims=True)
            m_ref[...] = m_new
            acc_ref[...] = alpha * acc_ref[...] + jnp.dot(values(kp, s), p.astype(BF16),
                                                          preferred_element_type=F32)

        pipelined(qk_phase, sm_phase)

    lax.cond(fast_ref[tile] == 1, fixed_reference, running_max)

    lam = (jnp.exp(jnp.sum(lq1_ref[...] * lk1_ref[...], axis=-1, keepdims=True))
           - jnp.exp(jnp.sum(lq2_ref[...] * lk2_ref[...], axis=-1, keepdims=True))
           + lam_init)
    acc = acc_ref[...]
    l = l_ref[...]
    ot = acc[:, :tq] / l[:, :tq] - lam * (acc[:, tq:] / l[:, tq:])
    ms = jnp.mean(ot * ot, axis=0, keepdims=True)
    ot = ot * lax.rsqrt(ms + EPS) * sub_ref[...] * (1.0 - lam_init)
    o_ref[0] = ot.T.astype(o_ref.dtype)


def _diff_attention(qkv, slopes, lq1, lk1, lq2, lk2, subln_col, lam_init):
    b, s, _ = qkv.shape
    h, dv = N_DIFF_HEADS, DIFF_V_DIM
    tq, tk = ATT_TQ, ATT_TK
    nq = s // tq
    assert s % ATT_PAIR == 0
    lo, hi, fast, ub = _attn_bounds(qkv, slopes)
    kern = functools.partial(_attn_kernel, lam_init=lam_init, seq=s)
    vec = pl.BlockSpec((1, DIFF_HEAD_DIM), lambda bi, hi, qi, *_: (0, 0))
    grid_spec = pltpu.PrefetchScalarGridSpec(
        num_scalar_prefetch=3,
        grid=(b, h, nq),
        in_specs=[
            pl.BlockSpec((1, tq, dv), lambda bi, hi, qi, *_: (bi, qi, hi)),
            pl.BlockSpec((1, s, dv), lambda bi, hi, qi, *_: (bi, 0, h + hi)),
            pl.BlockSpec((1, s, dv), lambda bi, hi, qi, *_: (bi, 0, 2 * h + hi)),
            pl.BlockSpec((1, 1, 1), lambda bi, hi, qi, *_: ((bi * h + hi) * nq + qi, 0, 0)),
            pl.BlockSpec((1, 1, 1), lambda bi, hi, qi, *_: (hi, 0, 0)),
            vec, vec, vec, vec,
            pl.BlockSpec((dv, 1), lambda bi, hi, qi, *_: (0, 0)),
        ],
        out_specs=pl.BlockSpec((1, tq, dv), lambda bi, hi, qi, *_: (bi, qi, hi)),
        scratch_shapes=[
            pltpu.VMEM((dv, s), BF16),
            pltpu.VMEM((2 + ATT_NQP, ATT_PAIR, 2 * tq), F32),
            pltpu.VMEM((2, tk, 2 * tq), F32),
            pltpu.VMEM((1, 2 * tq), F32),
            pltpu.VMEM((1, 2 * tq), F32),
            pltpu.VMEM((dv, 2 * tq), F32),
        ],
    )
    return pl.pallas_call(
        kern,
        grid_spec=grid_spec,
        out_shape=jax.ShapeDtypeStruct((b, s, h * dv), BF16),
        compiler_params=_cparams(3),
        name="diff_attention",
    )(lo.reshape(-1), hi.reshape(-1), fast.reshape(-1),
      qkv, qkv, qkv, ub.reshape(-1, 1, 1), slopes, lq1, lk1, lq2, lk2, subln_col)


def _proj_res_kernel(x_ref, a_ref, w_ref, gate_ref, o_ref):
    y = jnp.dot(a_ref[0], w_ref[...], preferred_element_type=F32)
    o_ref[0] = x_ref[0] + gate_ref[0] * y


def _proj_residual(x, a, w_bf16, gate, tm):
    b, s, d = x.shape
    kdim = a.shape[-1]
    return pl.pallas_call(
        _proj_res_kernel,
        grid=(b, s // tm),
        in_specs=[
            pl.BlockSpec((1, tm, d), lambda i, j: (i, j, 0)),
            pl.BlockSpec((1, tm, kdim), lambda i, j: (i, j, 0)),
            pl.BlockSpec((kdim, d), lambda i, j: (0, 0)),
            pl.BlockSpec((1, 1, d), lambda i, j: (i, 0, 0)),
        ],
        out_specs=pl.BlockSpec((1, tm, d), lambda i, j: (i, j, 0)),
        out_shape=jax.ShapeDtypeStruct((b, s, d), F32),
        compiler_params=_cparams(2),
        name="proj_residual",
    )(x, a, w_bf16, gate)


def _routing_gates_t(lt):
    ng, ne = N_EXPERT_GROUPS, EXPERTS_PER_GROUP
    gl = [lt[g:g + 1, :] for g in range(ng)]
    gmax = functools.reduce(jnp.maximum, gl)
    den = functools.reduce(lambda a, b: a + b, [jnp.exp(v - gmax) for v in gl])
    g_w = 1.0 / den
    sel = []
    taken = jnp.zeros(gmax.shape, jnp.bool_)
    for g in range(ng):
        s = jnp.logical_and(gl[g] == gmax, jnp.logical_not(taken))
        taken = jnp.logical_or(taken, s)
        sel.append(s)
    ev = []
    for e in range(ne):
        v = jnp.zeros_like(gmax)
        for g in range(ng):
            row = ng + ne * g + e
            v = jnp.where(sel[g], lt[row:row + 1, :], v)
        ev.append(v)
    top1 = functools.reduce(jnp.maximum, ev)
    is1 = []
    taken = jnp.zeros(gmax.shape, jnp.bool_)
    for e in range(ne):
        s = jnp.logical_and(ev[e] == top1, jnp.logical_not(taken))
        taken = jnp.logical_or(taken, s)
        is1.append(s)
    rest = [jnp.where(is1[e], -jnp.inf, ev[e]) for e in range(ne)]
    top2 = functools.reduce(jnp.maximum, rest)
    is2 = []
    taken = jnp.zeros(gmax.shape, jnp.bool_)
    for e in range(ne):
        s = jnp.logical_and(jnp.logical_and(rest[e] == top2, jnp.logical_not(is1[e])),
                            jnp.logical_not(taken))
        taken = jnp.logical_or(taken, s)
        is2.append(s)
    ex = jnp.exp(top2 - top1)
    w1 = 1.0 / (1.0 + ex)
    w2 = ex / (1.0 + ex)
    ew = [jnp.where(is1[e], w1, jnp.where(is2[e], w2, 0.0)) for e in range(ne)]
    rows = [jnp.where(sel[g], g_w * ew[e], 0.0) for g in range(ng) for e in range(ne)]
    return jnp.concatenate(rows, axis=0), sel


MOE_TM = 1024
MOE_CH = 256
MOE_NB = 7
MOE_POS_LANE = 3 * N_EXPERTS


def _route_sort_kernel(x_ref, g_ref, sh_ref, sc_ref, wr_ref, br_ref, tri_ref,
                       hs_ref, gs_ref, meta_ref, gid_ref):
    ng, ch, nb = N_EXPERT_GROUPS, MOE_CH, MOE_NB
    tm = x_ref.shape[0]
    h32 = _rms_mod(x_ref[...], g_ref[...], sh_ref[0], sc_ref[0])
    lt = lax.dot_general(wr_ref[...], h32, (((1,), (1,)), ((), ())),
                         precision=lax.Precision.HIGHEST,
                         preferred_element_type=F32) + br_ref[...]
    gates_t, sel = _routing_gates_t(lt)
    oh = jnp.concatenate([m.astype(F32) for m in sel] + [jnp.zeros((16 - ng, tm), F32)], axis=0)
    cnt = jnp.dot(oh.astype(BF16), tri_ref[...], preferred_element_type=F32)
    rank = jnp.sum(oh * cnt, axis=0, keepdims=True)
    n = jnp.sum(oh, axis=1, keepdims=True)
    nch = jnp.floor((n + float(ch - 1)) * (1.0 / ch))
    first = [jnp.zeros((1, 1), F32)]
    for g in range(ng):
        first.append(first[-1] + nch[g:g + 1, :])
    seg = functools.reduce(lambda a, b: a + b,
                           [oh[g:g + 1, :] * (first[g] * float(ch)) for g in range(ng)])
    pos = seg + rank
    rows = lax.broadcasted_iota(jnp.int32, (nb * ch, tm), 0)
    perm = jnp.where(rows == pos.astype(jnp.int32), 1.0, 0.0).astype(BF16)
    hs_ref[0] = jnp.dot(perm, h32.astype(BF16), preferred_element_type=F32).astype(BF16)
    hi = gates_t.astype(BF16).astype(F32)
    r1 = gates_t - hi
    mid = r1.astype(BF16).astype(F32)
    lo = r1 - mid
    meta_t = jnp.concatenate([hi, mid, lo, pos, jnp.zeros((LANES - MOE_POS_LANE - 1, tm), F32)], axis=0)
    meta = meta_t.T
    meta_ref[...] = meta
    gs_ref[0] = jnp.dot(perm, meta.astype(BF16), preferred_element_type=F32)
    lane = lax.broadcasted_iota(jnp.int32, (1, LANES), 1).astype(F32)
    gid = functools.reduce(lambda a, b: a + b,
                           [jnp.where(lane >= first[g + 1], 1.0, 0.0) for g in range(ng)])
    gid_ref[0] = gid.astype(jnp.int32)


def _route_sort(x2, g, shift, scale, wr, br, tri, tiles_per_batch):
    t, d = x2.shape
    tm, rows = MOE_TM, MOE_NB * MOE_CH
    nt = t // tm
    bvec = lambda i: (i // tiles_per_batch, 0, 0)
    return pl.pallas_call(
        _route_sort_kernel,
        grid=(nt,),
        in_specs=[
            pl.BlockSpec((tm, d), lambda i: (i, 0)),
            pl.BlockSpec((1, d), lambda i: (0, 0)),
            pl.BlockSpec((1, 1, d), bvec),
            pl.BlockSpec((1, 1, d), bvec),
            pl.BlockSpec((LANES, d), lambda i: (0, 0)),
            pl.BlockSpec((LANES, 1), lambda i: (0, 0)),
            pl.BlockSpec((tm, tm), lambda i: (0, 0)),
        ],
        out_specs=[
            pl.BlockSpec((1, rows, d), lambda i: (i, 0, 0)),
            pl.BlockSpec((1, rows, LANES), lambda i: (i, 0, 0)),
            pl.BlockSpec((tm, LANES), lambda i: (i, 0)),
            pl.BlockSpec((1, 1, LANES), lambda i: (i, 0, 0)),
        ],
        out_shape=[
            jax.ShapeDtypeStruct((nt, rows, d), BF16),
            jax.ShapeDtypeStruct((nt, rows, LANES), F32),
            jax.ShapeDtypeStruct((t, LANES), F32),
            jax.ShapeDtypeStruct((nt, 1, LANES), jnp.int32),
        ],
        compiler_params=_cparams(1),
        name="moe_route_sort",
    )(x2, g, shift, scale, wr, br, tri)


def _experts_kernel(order_ref, grp_ref, hs_ref, gs_ref, wg_ref, wu_ref, wd_ref, ys_ref):
    ne = EXPERTS_PER_GROUP
    grp = grp_ref[pl.program_id(0)]

    @pl.when(grp < N_EXPERT_GROUPS)
    def _():
        h = hs_ref[0]
        gs = gs_ref[0]
        lane = lax.broadcasted_iota(jnp.int32, gs.shape, 1)
        hid = []
        for e in range(ne):
            idx = grp * ne + e
            pick = jnp.logical_or(jnp.logical_or(lane == idx, lane == idx + N_EXPERTS),
                                  lane == idx + 2 * N_EXPERTS)
            gate = jnp.sum(jnp.where(pick, gs, 0.0), axis=1, keepdims=True)
            a = jnp.dot(h, wg_ref[0, e], preferred_element_type=F32)
            u = jnp.dot(h, wu_ref[0, e], preferred_element_type=F32)
            hid.append(((a / (1.0 + jnp.exp(-a))) * u * gate).astype(BF16))
        hid = jnp.concatenate(hid, axis=1)
        ys_ref[0] = jnp.dot(hid, wd_ref[0], preferred_element_type=F32).astype(ys_ref.dtype)

    @pl.when(grp >= N_EXPERT_GROUPS)
    def _():
        ys_ref[0] = jnp.zeros(ys_ref.shape[1:], ys_ref.dtype)


def _experts(order, grp, hs, gs, wg4, wu4, wd4):
    nchunks, ch, d = hs.shape
    ne, f = wg4.shape[1], wg4.shape[-1]
    wsel = lambda s, order, grp: (jnp.minimum(grp[s], N_EXPERT_GROUPS - 1), 0, 0, 0)
    grid_spec = pltpu.PrefetchScalarGridSpec(
        num_scalar_prefetch=2,
        grid=(nchunks,),
        in_specs=[
            pl.BlockSpec((1, ch, d), lambda s, order, grp: (order[s], 0, 0)),
            pl.BlockSpec((1, ch, LANES), lambda s, order, grp: (order[s], 0, 0)),
            pl.BlockSpec((1, ne, d, f), wsel),
            pl.BlockSpec((1, ne, d, f), wsel),
            pl.BlockSpec((1, ne * f, d),
                         lambda s, order, grp: (jnp.minimum(grp[s], N_EXPERT_GROUPS - 1), 0, 0)),
        ],
        out_specs=pl.BlockSpec((1, ch, d), lambda s, order, grp: (order[s], 0, 0)),
    )
    return pl.pallas_call(
        _experts_kernel,
        grid_spec=grid_spec,
        out_shape=jax.ShapeDtypeStruct((nchunks, ch, d), BF16),
        compiler_params=_cparams(1),
        name="moe_experts",
    )(order, grp, hs, gs, wg4, wu4, wd4)


def _unsort_kernel(x_ref, ys_ref, meta_ref, gate_ref, gf_ref, o_ref, *, final_norm):
    rows = ys_ref.shape[1]
    tm = x_ref.shape[0]
    pos = meta_ref[:, MOE_POS_LANE:MOE_POS_LANE + 1].astype(jnp.int32)
    cols = lax.broadcasted_iota(jnp.int32, (tm, rows), 1)
    perm_t = jnp.where(cols == pos, 1.0, 0.0).astype(BF16)
    y = jnp.dot(perm_t, ys_ref[0], preferred_element_type=F32)
    y = x_ref[...] + gate_ref[0] * y
    if final_norm:
        y = y * lax.rsqrt(jnp.mean(y * y, axis=-1, keepdims=True) + EPS) * gf_ref[...]
    o_ref[...] = y


def _unsort_residual(x2, ys, meta, gate, gfinal, final_norm, tiles_per_batch):
    t, d = x2.shape
    tm = MOE_TM
    nt, rows, _ = ys.shape
    kern = functools.partial(_unsort_kernel, final_norm=final_norm)
    return pl.pallas_call(
        kern,
        grid=(nt,),
        in_specs=[
            pl.BlockSpec((tm, d), lambda i: (i, 0)),
            pl.BlockSpec((1, rows, d), lambda i: (i, 0, 0)),
            pl.BlockSpec((tm, LANES), lambda i: (i, 0)),
            pl.BlockSpec((1, 1, d), lambda i: (i // tiles_per_batch, 0, 0)),
            pl.BlockSpec((1, d), lambda i: (0, 0)),
        ],
        out_specs=pl.BlockSpec((tm, d), lambda i: (i, 0)),
        out_shape=jax.ShapeDtypeStruct((t, d), F32),
        compiler_params=_cparams(1),
        name="moe_unsort_residual",
    )(x2, ys, meta, gate, gfinal)


def moe_layer(x, g, sh2, sc2, g2, w_rg, b_rg, w_re, b_re, w_gate, w_up, w_down, gfinal, final_norm):
    b, s, d = x.shape
    ng, ne = N_EXPERT_GROUPS, EXPERTS_PER_GROUP
    f = w_gate.shape[-1]
    assert s % MOE_TM == 0
    nr = ng + N_EXPERTS
    wr = jnp.concatenate([w_rg.T, w_re.reshape(d, N_EXPERTS).T,
                          jnp.zeros((LANES - nr, d), F32)], axis=0)
    br = jnp.concatenate([b_rg, b_re.reshape(N_EXPERTS),
                          jnp.zeros((LANES - nr,), F32)]).reshape(LANES, 1)
    ti = np.arange(MOE_TM)
    tri = jnp.asarray(ti[:, None] < ti[None, :], F32).astype(BF16)
    x2 = x.reshape(b * s, d)
    tpb = s // MOE_TM
    hs, gs, meta, gid = _route_sort(x2, g.reshape(1, d), sh2, sc2, wr, br, tri, tpb)
    nt = hs.shape[0]
    gid_flat = gid[:, 0, :MOE_NB].reshape(-1)
    order = jnp.argsort(gid_flat, stable=True).astype(jnp.int32)
    grp = gid_flat[order]
    ys = _experts(order, grp,
                  hs.reshape(nt * MOE_NB, MOE_CH, d), gs.reshape(nt * MOE_NB, MOE_CH, LANES),
                  w_gate.astype(BF16).reshape(ng, ne, d, f), w_up.astype(BF16).reshape(ng, ne, d, f),
                  w_down.astype(BF16).reshape(ng, ne * f, d))
    out = _unsort_residual(x2, ys.reshape(nt, MOE_NB * MOE_CH, d), meta, g2,
                           gfinal.reshape(1, d), final_norm, tpb)
    return out.reshape(b, s, d)


def _fft1_kernel(u_ref, w_ref, twr_ref, twi_ref, o_ref, *, n_sub, width):
    n1 = FFT_N1
    y = jnp.dot(w_ref[...], u_ref[0], preferred_element_type=F32)
    for j in range(n_sub):
        cols = slice(j * width, (j + 1) * width)
        yr, yi = y[:n1, cols], y[n1:, cols]
        twr, twi = twr_ref[j], twi_ref[j]
        o_ref[0, :n1, cols] = (yr * twr - yi * twi).astype(o_ref.dtype)
        o_ref[0, n1:, cols] = (yr * twi + yi * twr).astype(o_ref.dtype)


def _fft_stage1(u3, w1, twr, twi, width):
    b, n1, cols = u3.shape
    n_sub = 8
    tn = n_sub * width
    kern = functools.partial(_fft1_kernel, n_sub=n_sub, width=width)
    return pl.pallas_call(
        kern,
        grid=(b, cols // tn),
        in_specs=[
            pl.BlockSpec((1, n1, tn), lambda i, j: (i, 0, j)),
            pl.BlockSpec((2 * n1, n1), lambda i, j: (0, 0)),
            pl.BlockSpec((n_sub, n1, 1), lambda i, j: (j, 0, 0)),
            pl.BlockSpec((n_sub, n1, 1), lambda i, j: (j, 0, 0)),
        ],
        out_specs=pl.BlockSpec((1, 2 * n1, tn), lambda i, j: (i, 0, j)),
        out_shape=jax.ShapeDtypeStruct((b, 2 * n1, cols), BF16),
        compiler_params=_cparams(2),
        name="fft_stage1",
    )(u3, w1, twr, twi)


def _fft2_kernel(y_ref, g_ref, o_ref):
    kp, n2 = FFT_K1_PER_STEP, FFT_N2
    width = y_ref.shape[-1]
    y = y_ref[0].reshape(2 * kp * n2, width)
    r = jnp.dot(g_ref[...], y, preferred_element_type=F32)
    o_ref[0] = r.reshape(2, n2, kp, width).astype(o_ref.dtype)


def _fft_stage2(y5, gmat):
    b, _, n1, n2, width = y5.shape
    kp = FFT_K1_PER_STEP
    return pl.pallas_call(
        _fft2_kernel,
        grid=(b, n1 // kp),
        in_specs=[
            pl.BlockSpec((1, 2, kp, n2, width), lambda i, j: (i, 0, j, 0, 0)),
            pl.BlockSpec((2 * kp * n2, 2 * kp * n2), lambda i, j: (0, 0)),
        ],
        out_specs=pl.BlockSpec((1, 2, n2, kp, width), lambda i, j: (i, 0, 0, j, 0)),
        out_shape=jax.ShapeDtypeStruct((b, 2, n2, n1, width), F32),
        compiler_params=_cparams(2),
        name="fft_stage2",
    )(y5, gmat)


def _fmix_out_kernel(x_ref, a_ref, cc_ref, sc_ref, w_ref, gate_ref, o_ref):
    gd = cc_ref.shape[0]
    ar, ai = a_ref[0, 0].astype(BF16), a_ref[0, 1].astype(BF16)
    parts = []
    for g in range(N_FOURIER_GROUPS):
        cols = slice(g * gd, (g + 1) * gd)
        parts.append(jnp.dot(ar[:, cols], cc_ref[...], preferred_element_type=F32)
                     + jnp.dot(ai[:, cols], sc_ref[...], preferred_element_type=F32))
    f = jnp.concatenate(parts, axis=1).astype(BF16)
    y = jnp.dot(f, w_ref[...], preferred_element_type=F32)
    o_ref[0] = x_ref[0] + gate_ref[0] * y


def _fmix_out(x, a4, cc, sc, w_bf16, gate, tm):
    b, s, d = x.shape
    gd = cc.shape[0]
    return pl.pallas_call(
        _fmix_out_kernel,
        grid=(b, s // tm),
        in_specs=[
            pl.BlockSpec((1, tm, d), lambda i, j: (i, j, 0)),
            pl.BlockSpec((1, 2, tm, d), lambda i, j: (i, 0, j, 0)),
            pl.BlockSpec((gd, gd), lambda i, j: (0, 0)),
            pl.BlockSpec((gd, gd), lambda i, j: (0, 0)),
            pl.BlockSpec((d, d), lambda i, j: (0, 0)),
            pl.BlockSpec((1, 1, d), lambda i, j: (i, 0, 0)),
        ],
        out_specs=pl.BlockSpec((1, tm, d), lambda i, j: (i, j, 0)),
        out_shape=jax.ShapeDtypeStruct((b, s, d), F32),
        compiler_params=_cparams(2),
        name="fourier_out",
    )(x, a4, cc, sc, w_bf16, gate)


def _dft_tables(seq, gd):
    n1, n2, kp = FFT_N1, FFT_N2, FFT_K1_PER_STEP

    def cs(num, den):
        ang = (2.0 * math.pi / den) * (num % den).astype(np.float64)
        return np.cos(ang), np.sin(ang)

    i1 = np.arange(n1)
    c1, s1 = cs(np.outer(i1, i1), n1)
    w1 = np.concatenate([c1, -s1], axis=0)
    ct, st = cs(np.outer(np.arange(n2), i1), seq)
    twr, twi = ct[:, :, None], -st[:, :, None]
    i2 = np.arange(n2)
    c2, s2 = cs(np.outer(i2, i2), n2)
    m3 = np.block([[c2, s2], [-s2, c2]])
    m3 = m3.reshape(2, n2, 2, n2)
    gmat = np.zeros((2, n2, kp, 2, kp, n2))
    for j in range(kp):
        gmat[:, :, j, :, j, :] = m3
    gmat = gmat.reshape(2 * n2 * kp, 2 * kp * n2)
    ic = np.arange(gd)
    cc, sc = cs(np.outer(ic, ic), gd)
    norm = 1.0 / math.sqrt(seq * gd)
    return tuple(jnp.asarray(t, F32) for t in (w1, twr, twi, gmat, cc * norm, sc * norm))


def kernel(x, c, ada_w, ada_b, norm_mix, norm_ffn, attn_w_in, attn_lam_q1, attn_lam_k1, attn_lam_q2, attn_lam_k2, attn_subln, attn_w_out, fourier_w_in, fourier_w_out, router_group_w, router_group_b, router_expert_w, router_expert_b, expert_w_gate, expert_w_up, expert_w_down, norm_final):
    b, s, d = x.shape
    depth = ada_w.shape[0]
    assert s == FFT_N1 * FFT_N2 and depth == 2

    c8 = jnp.zeros((8, d), F32).at[:b].set(c)
    mod = _ada(c8, ada_w, ada_b.reshape(depth, 1, N_ADA * d))[:, :b]
    mod = mod.reshape(depth, b, N_ADA, 1, d)

    slopes = (2.0 ** (-8.0 * jnp.arange(1, N_DIFF_HEADS + 1, dtype=F32) / N_DIFF_HEADS)
              ).reshape(N_DIFF_HEADS, 1, 1)
    gd = d // N_FOURIER_GROUPS
    w1, twr, twi, gmat, cc, sc = _dft_tables(s, gd)
    w1, gmat, cc, sc = (t.astype(BF16) for t in (w1, gmat, cc, sc))

    for i in range(depth):
        sh1, sc1, g1, sh2, sc2, g2 = [mod[i, :, k] for k in range(N_ADA)]
        j = i // 2
        if i % 2 == 0:
            qscale = jnp.concatenate([jnp.full((d,), LOG2E * DIFF_HEAD_DIM ** -0.5, F32),
                                      jnp.ones((2 * d,), F32)])
            w_in = (attn_w_in[j] * qscale).astype(BF16)
            qkv = _norm_mod_matmul(x, norm_mix[i].reshape(1, d), sh1, sc1, w_in, tm=512)
            lam_init = 0.8 - 0.6 * math.exp(-0.3 * i)
            o = _diff_attention(qkv, slopes,
                                attn_lam_q1[j].reshape(1, -1), attn_lam_k1[j].reshape(1, -1),
                                attn_lam_q2[j].reshape(1, -1), attn_lam_k2[j].reshape(1, -1),
                                attn_subln[j].reshape(-1, 1), lam_init)
            x = _proj_residual(x, o, attn_w_out[j].astype(BF16), g1, tm=512)
        else:
            u = _norm_mod_matmul(x, norm_mix[i].reshape(1, d), sh1, sc1,
                                 fourier_w_in[j].astype(BF16), tm=512)
            y = _fft_stage1(u.reshape(b, FFT_N1, FFT_N2 * d), w1, twr, twi, d)
            a = _fft_stage2(y.reshape(b, 2, FFT_N1, FFT_N2, d), gmat)
            x = _fmix_out(x, a.reshape(b, 2, s, d), cc, sc, fourier_w_out[j].astype(BF16), g1, tm=512)

        x = moe_layer(x, norm_ffn[i], sh2, sc2, g2, router_group_w[i], router_group_b[i],
                      router_expert_w[i], router_expert_b[i], expert_w_gate[i], expert_w_up[i],
                      expert_w_down[i], norm_final, i == depth - 1)
    return x
```

```python
import functools
import math

import numpy as np
import jax
import jax.numpy as jnp
from jax import lax
from jax.experimental import pallas as pl
from jax.experimental.pallas import tpu as pltpu

F32 = jnp.float32
BF16 = jnp.bfloat16

EPS = 1e-6
N_DIFF_HEADS = 8
DIFF_HEAD_DIM = 64
DIFF_V_DIM = 128
N_FOURIER_GROUPS = 4
N_EXPERT_GROUPS = 4
EXPERTS_PER_GROUP = 4
N_EXPERTS = 16
N_ADA = 6

LANES = 128
VMEM_LIMIT_BYTES = 56 * 1024 * 1024

FFT_N1 = 128
FFT_N2 = 64
FFT_K1_PER_STEP = 8


def _cparams(n_axes):
    return pltpu.CompilerParams(
        dimension_semantics=("arbitrary",) * n_axes,
        vmem_limit_bytes=VMEM_LIMIT_BYTES,
    )


def _rms_mod(x, g, shift, scale):
    y = x * lax.rsqrt(jnp.mean(x * x, axis=-1, keepdims=True) + EPS)
    return (y * g) * (1.0 + scale) + shift


CAST_BLOCK_ROWS = 2048


def _cast_kernel(w_ref, o_ref):
    o_ref[...] = w_ref[...].astype(o_ref.dtype)


def _to_bf16(w_layers, layer):
    cols = w_layers.shape[-1]
    w3 = w_layers.reshape(w_layers.shape[0], -1, cols)
    rows = w3.shape[1]
    tr = min(CAST_BLOCK_ROWS, rows)
    assert rows % tr == 0
    out = pl.pallas_call(
        _cast_kernel,
        grid=(rows // tr,),
        in_specs=[pl.BlockSpec((1, tr, cols), lambda i: (layer, i, 0))],
        out_specs=pl.BlockSpec((1, tr, cols), lambda i: (0, i, 0)),
        out_shape=jax.ShapeDtypeStruct((1, rows, cols), BF16),
        compiler_params=_cparams(1),
        name="cast_bf16",
    )(w3)
    return out.reshape(w_layers.shape[1:])


def _ada_kernel(c_ref, w_ref, b_ref, o_ref):
    c = c_ref[...]
    cond = c / (1.0 + jnp.exp(-c))
    o_ref[0] = jnp.dot(cond, w_ref[0], precision=lax.Precision.HIGHEST,
                       preferred_element_type=F32) + b_ref[0]


def _ada(c8, ada_w, ada_b3):
    depth, d, n = ada_w.shape
    tn = 1536
    return pl.pallas_call(
        _ada_kernel,
        grid=(depth, n // tn),
        in_specs=[
            pl.BlockSpec((8, d), lambda i, j: (0, 0)),
            pl.BlockSpec((1, d, tn), lambda i, j: (i, 0, j)),
            pl.BlockSpec((1, 1, tn), lambda i, j: (i, 0, j)),
        ],
        out_specs=pl.BlockSpec((1, 8, tn), lambda i, j: (i, 0, j)),
        out_shape=jax.ShapeDtypeStruct((depth, 8, n), F32),
        compiler_params=_cparams(2),
        name="ada_mod",
    )(c8, ada_w, ada_b3)


def _nmm_kernel(x_ref, g_ref, sh_ref, sc_ref, w_ref, o_ref):
    h = _rms_mod(x_ref[0], g_ref[...], sh_ref[0], sc_ref[0])
    o_ref[0] = jnp.dot(h.astype(BF16), w_ref[...],
                       preferred_element_type=F32).astype(o_ref.dtype)


def _norm_mod_matmul(x, g, shift, scale, w_bf16, tm):
    b, s, d = x.shape
    n = w_bf16.shape[1]
    return pl.pallas_call(
        _nmm_kernel,
        grid=(b, s // tm),
        in_specs=[
            pl.BlockSpec((1, tm, d), lambda i, j: (i, j, 0)),
            pl.BlockSpec((1, d), lambda i, j: (0, 0)),
            pl.BlockSpec((1, 1, d), lambda i, j: (i, 0, 0)),
            pl.BlockSpec((1, 1, d), lambda i, j: (i, 0, 0)),
            pl.BlockSpec((d, n), lambda i, j: (0, 0)),
        ],
        out_specs=pl.BlockSpec((1, tm, n), lambda i, j: (i, j, 0)),
        out_shape=jax.ShapeDtypeStruct((b, s, n), BF16),
        compiler_params=_cparams(2),
        name="norm_mod_matmul",
    )(x, g, shift, scale, w_bf16)


ATT_TQ = 256
ATT_TK = 512
ATT_PAIR = 2 * ATT_TK
ATT_NQP = ATT_PAIR // ATT_TQ
LOG2E = 1.4426950408889634
ATT_SKIP_BITS = 40.0
ATT_FAST_MAX_U = 36.0


def _attn_bounds_kernel(q_ref, k_ref, slope_ref, lo_ref, hi_ref, fast_ref, ub_ref, *, seq):
    tq, pair = ATT_TQ, ATT_PAIR
    nq = seq // tq
    lane = lax.broadcasted_iota(jnp.int32, (seq, 2 * DIFF_HEAD_DIM), 1)
    first = lane < DIFF_HEAD_DIM

    def max_half_norm2(ref):
        x = ref[0].astype(F32)
        xx = x * x
        n1 = jnp.sum(jnp.where(first, xx, 0.0), axis=1, keepdims=True)
        n2 = jnp.sum(jnp.where(first, 0.0, xx), axis=1, keepdims=True)
        return jnp.maximum(n1, n2)

    qn2 = jnp.max(max_half_norm2(q_ref).reshape(nq, tq, 1), axis=1)
    kn2 = jnp.max(max_half_norm2(k_ref), axis=0, keepdims=True)
    ub = jnp.sqrt(qn2 * kn2) * (1.0 + 2.0 ** -6)
    slope = slope_ref[0] * LOG2E
    dist = (2.0 * ub + (ATT_SKIP_BITS + math.log2(seq))) / slope
    dist = jnp.ceil(jnp.minimum(dist, float(seq))).astype(jnp.int32)
    q0 = lax.broadcasted_iota(jnp.int32, (nq, 1), 0) * tq
    jmin = jnp.maximum(q0 - dist + 1, 0)
    jmax = jnp.minimum(q0 + (tq - 1) + dist - 1, seq - 1)
    shift = jnp.full((nq, 1), int(math.log2(pair)), jnp.int32)
    lo_ref[0, 0] = lax.shift_right_logical(jmin, shift)
    hi_ref[0, 0] = lax.shift_right_logical(jmax, shift) + 1
    fast_ref[0, 0] = (ub <= ATT_FAST_MAX_U).astype(jnp.int32)
    ub_ref[0, 0] = ub


def _attn_bounds(qkv, slopes):
    b, s, _ = qkv.shape
    h, dv = N_DIFF_HEADS, DIFF_V_DIM
    nq = s // ATT_TQ
    kern = functools.partial(_attn_bounds_kernel, seq=s)
    oblk = pl.BlockSpec((1, 1, nq, 1), lambda bi, hi: (bi, hi, 0, 0))
    ishape = jax.ShapeDtypeStruct((b, h, nq, 1), jnp.int32)
    return pl.pallas_call(
        kern,
        grid=(b, h),
        in_specs=[
            pl.BlockSpec((1, s, dv), lambda bi, hi: (bi, 0, hi)),
            pl.BlockSpec((1, s, dv), lambda bi, hi: (bi, 0, h + hi)),
            pl.BlockSpec((1, 1, 1), lambda bi, hi: (hi, 0, 0)),
        ],
        out_specs=[oblk, oblk, oblk, oblk],
        out_shape=[ishape, ishape, ishape, jax.ShapeDtypeStruct((b, h, nq, 1), F32)],
        compiler_params=_cparams(2),
        name="attn_bounds",
    )(qkv, qkv, slopes)


def _attn_kernel(lo_ref, hi_ref, fast_ref, q_ref, k_ref, v_ref, ub_ref, slope_ref,
                 lq1_ref, lk1_ref, lq2_ref, lk2_ref, sub_ref,
                 o_ref, vt_ref, bias_ref, u_ref, m_ref, l_ref, acc_ref, *, lam_init, seq):
    tq, tk, pair, nqp = ATT_TQ, ATT_TK, ATT_PAIR, ATT_NQP
    dh = DIFF_HEAD_DIM
    qi = pl.program_id(2)
    tile = (pl.program_id(0) * N_DIFF_HEADS + pl.program_id(1)) * (seq // tq) + qi
    lo, hi = lo_ref[tile], hi_ref[tile]
    slope = slope_ref[0] * LOG2E

    @pl.when(qi == 0)
    def _():
        for c in range(seq // 512):
            vt_ref[:, c * 512:(c + 1) * 512] = v_ref[0, c * 512:(c + 1) * 512, :].T
        for s in range(2):
            rows = slice(s * tk, (s + 1) * tk)
            r = lax.broadcasted_iota(jnp.int32, (tk, 2 * tq), 0) + s * tk
            c = lax.broadcasted_iota(jnp.int32, (tk, 2 * tq), 1)
            c = jnp.where(c >= tq, c - tq, c)
            rc = (r - c).astype(F32)
            bias_ref[0, rows, :] = slope * rc
            bias_ref[1, rows, :] = -(slope * rc)
            for w in range(nqp):
                bias_ref[2 + w, rows, :] = -(slope * jnp.abs(rc - float(w * tq)))

    qt = q_ref[0].T
    z = jnp.zeros((dh, tq), BF16)
    qbd = jnp.concatenate([jnp.concatenate([qt[:dh], z], axis=1),
                           jnp.concatenate([z, qt[dh:]], axis=1)], axis=0)

    l_ref[...] = jnp.zeros(l_ref.shape, F32)
    acc_ref[...] = jnp.zeros(acc_ref.shape, F32)

    kd = qi // nqp
    w = qi % nqp

    def mode(kp):
        delta = (kp * pair - qi * tq).astype(F32)
        idx = jnp.where(kp < kd, 0, jnp.where(kp > kd, 1, 2 + w))
        sgn = jnp.where(kp < kd, delta, jnp.where(kp > kd, -delta, 0.0))
        return idx, slope * sgn

    def scores(kp, idx, s):
        k0 = pl.multiple_of(kp * pair + s * tk, tk)
        kblk = k_ref[0, pl.ds(k0, tk), :]
        u = jnp.dot(kblk, qbd, preferred_element_type=F32)
        return u + bias_ref[idx, s * tk:(s + 1) * tk, :]

    def values(kp, s):
        k0 = pl.multiple_of(kp * pair + s * tk, tk)
        return vt_ref[:, pl.ds(k0, tk)]

    def pipelined(qk_phase, sm_phase):
        idx0, _ = mode(lo)
        carry0 = qk_phase(lo, idx0, 0)

        def body(kp, carry_a):
            idx, cst = mode(kp)
            carry_b = qk_phase(kp, idx, 1)
            sm_phase(kp, cst, 0, carry_a)
            idx_n, _ = mode(kp + 1)
            carry_a_next = qk_phase(kp + 1, idx_n, 0)
            sm_phase(kp, cst, 1, carry_b)
            return carry_a_next

        n_double = (hi - 1 - lo) // 2
        carry_a = lax.fori_loop(0, n_double,
                                lambda i, c: body(lo + 2 * i + 1, body(lo + 2 * i, c)), carry0)
        carry_a = lax.fori_loop(lo + 2 * n_double, hi - 1, body, carry_a)
        idx_l, cst_l = mode(hi - 1)
        carry_b = qk_phase(hi - 1, idx_l, 1)
        sm_phase(hi - 1, cst_l, 0, carry_a)
        sm_phase(hi - 1, cst_l, 1, carry_b)

    def fixed_reference():
        ub = ub_ref[0]

        def qk_phase(kp, idx, s):
            u_ref[s] = scores(kp, idx, s)
            return 0

        def sm_phase(kp, cst, s, _):
            p = jnp.exp2(u_ref[s] + (cst - ub))
            l_ref[...] += jnp.sum(p, axis=0, keepdims=True)
            acc_ref[...] += jnp.dot(values(kp, s), p.astype(BF16), preferred_element_type=F32)

        pipelined(qk_phase, sm_phase)

    def running_max():
        m_ref[...] = jnp.full(m_ref.shape, -jnp.inf, F32)

        def qk_phase(kp, idx, s):
            v = scores(kp, idx, s)
            u_ref[s] = v
            return jnp.max(v, axis=0, keepdims=True)

        def sm_phase(kp, cst, s, mx):
            m_old = m_ref[...]
            m_new = jnp.maximum(m_old, mx + cst)
            alpha = jnp.exp2(m_old - m_new)
            p = jnp.exp2(u_ref[s] - (m_new - cst))
            l_ref[...] = alpha * l_ref[...] + jnp.sum(p, axis=0, keepdims=True)
            m_ref[...] = m_new
            acc_ref[...] = alpha * acc_ref[...] + jnp.dot(values(kp, s), p.astype(BF16),
                                                          preferred_element_type=F32)

        pipelined(qk_phase, sm_phase)

    lax.cond(fast_ref[tile] == 1, fixed_reference, running_max)

    lam = (jnp.exp(jnp.sum(lq1_ref[...] * lk1_ref[...], axis=-1, keepdims=True))
           - jnp.exp(jnp.sum(lq2_ref[...] * lk2_ref[...], axis=-1, keepdims=True))
           + lam_init)
    acc = acc_ref[...]
    l = l_ref[...]
    ot = acc[:, :tq] / l[:, :tq] - lam * (acc[:, tq:] / l[:, tq:])
    ms = jnp.mean(ot * ot, axis=0, keepdims=True)
    ot = ot * lax.rsqrt(ms + EPS) * sub_ref[...] * (1.0 - lam_init)
    o_ref[0] = ot.T.astype(o_ref.dtype)


def _diff_attention(qkv, slopes, lq1, lk1, lq2, lk2, subln_col, lam_init):
    b, s, _ = qkv.shape
    h, dv = N_DIFF_HEADS, DIFF_V_DIM
    tq, tk = ATT_TQ, ATT_TK
    nq = s // tq
    assert s % ATT_PAIR == 0
    lo, hi, fast, ub = _attn_bounds(qkv, slopes)
    kern = functools.partial(_attn_kernel, lam_init=lam_init, seq=s)
    vec = pl.BlockSpec((1, DIFF_HEAD_DIM), lambda bi, hi, qi, *_: (0, 0))
    grid_spec = pltpu.PrefetchScalarGridSpec(
        num_scalar_prefetch=3,
        grid=(b, h, nq),
        in_specs=[
            pl.BlockSpec((1, tq, dv), lambda bi, hi, qi, *_: (bi, qi, hi)),
            pl.BlockSpec((1, s, dv), lambda bi, hi, qi, *_: (bi, 0, h + hi)),
            pl.BlockSpec((1, s, dv), lambda bi, hi, qi, *_: (bi, 0, 2 * h + hi)),
            pl.BlockSpec((1, 1, 1), lambda bi, hi, qi, *_: ((bi * h + hi) * nq + qi, 0, 0)),
            pl.BlockSpec((1, 1, 1), lambda bi, hi, qi, *_: (hi, 0, 0)),
            vec, vec, vec, vec,
            pl.BlockSpec((dv, 1), lambda bi, hi, qi, *_: (0, 0)),
        ],
        out_specs=pl.BlockSpec((1, tq, dv), lambda bi, hi, qi, *_: (bi, qi, hi)),
        scratch_shapes=[
            pltpu.VMEM((dv, s), BF16),
            pltpu.VMEM((2 + ATT_NQP, ATT_PAIR, 2 * tq), F32),
            pltpu.VMEM((2, tk, 2 * tq), F32),
            pltpu.VMEM((1, 2 * tq), F32),
            pltpu.VMEM((1, 2 * tq), F32),
            pltpu.VMEM((dv, 2 * tq), F32),
        ],
    )
    return pl.pallas_call(
        kern,
        grid_spec=grid_spec,
        out_shape=jax.ShapeDtypeStruct((b, s, h * dv), BF16),
        compiler_params=_cparams(3),
        name="diff_attention",
    )(lo.reshape(-1), hi.reshape(-1), fast.reshape(-1),
      qkv, qkv, qkv, ub.reshape(-1, 1, 1), slopes, lq1, lk1, lq2, lk2, subln_col)


def _proj_res_kernel(x_ref, a_ref, w_ref, gate_ref, o_ref):
    y = jnp.dot(a_ref[0], w_ref[...], preferred_element_type=F32)
    o_ref[0] = x_ref[0] + gate_ref[0] * y


def _proj_residual(x, a, w_bf16, gate, tm):
    b, s, d = x.shape
    kdim = a.shape[-1]
    return pl.pallas_call(
        _proj_res_kernel,
        grid=(b, s // tm),
        in_specs=[
            pl.BlockSpec((1, tm, d), lambda i, j: (i, j, 0)),
            pl.BlockSpec((1, tm, kdim), lambda i, j: (i, j, 0)),
            pl.BlockSpec((kdim, d), lambda i, j: (0, 0)),
            pl.BlockSpec((1, 1, d), lambda i, j: (i, 0, 0)),
        ],
        out_specs=pl.BlockSpec((1, tm, d), lambda i, j: (i, j, 0)),
        out_shape=jax.ShapeDtypeStruct((b, s, d), F32),
        compiler_params=_cparams(2),
        name="proj_residual",
    )(x, a, w_bf16, gate)


def _routing_gates_t(lt):
    ng, ne = N_EXPERT_GROUPS, EXPERTS_PER_GROUP
    gl = [lt[g:g + 1, :] for g in range(ng)]
    gmax = functools.reduce(jnp.maximum, gl)
    den = functools.reduce(lambda a, b: a + b, [jnp.exp(v - gmax) for v in gl])
    g_w = 1.0 / den
    sel = []
    taken = jnp.zeros(gmax.shape, jnp.bool_)
    for g in range(ng):
        s = jnp.logical_and(gl[g] == gmax, jnp.logical_not(taken))
        taken = jnp.logical_or(taken, s)
        sel.append(s)
    ev = []
    for e in range(ne):
        v = jnp.zeros_like(gmax)
        for g in range(ng):
            row = ng + ne * g + e
            v = jnp.where(sel[g], lt[row:row + 1, :], v)
        ev.append(v)
    top1 = functools.reduce(jnp.maximum, ev)
    is1 = []
    taken = jnp.zeros(gmax.shape, jnp.bool_)
    for e in range(ne):
        s = jnp.logical_and(ev[e] == top1, jnp.logical_not(taken))
        taken = jnp.logical_or(taken, s)
        is1.append(s)
    rest = [jnp.where(is1[e], -jnp.inf, ev[e]) for e in range(ne)]
    top2 = functools.reduce(jnp.maximum, rest)
    is2 = []
    taken = jnp.zeros(gmax.shape, jnp.bool_)
    for e in range(ne):
        s = jnp.logical_and(jnp.logical_and(rest[e] == top2, jnp.logical_not(is1[e])),
                            jnp.logical_not(taken))
        taken = jnp.logical_or(taken, s)
        is2.append(s)
    ex = jnp.exp(top2 - top1)
    w1 = 1.0 / (1.0 + ex)
    w2 = ex / (1.0 + ex)
    ew = [jnp.where(is1[e], w1, jnp.where(is2[e], w2, 0.0)) for e in range(ne)]
    rows = [jnp.where(sel[g], g_w * ew[e], 0.0) for g in range(ng) for e in range(ne)]
    return jnp.concatenate(rows, axis=0), sel


MOE_TM = 1024
MOE_CH = 288
MOE_NB = 7
MOE_POS_LANE = 3 * N_EXPERTS


def _route_sort_kernel(x_ref, g_ref, sh_ref, sc_ref, wr_ref, br_ref, tri_ref,
                       hs_ref, gs_ref, meta_ref, gid_ref):
    ng, ch, nb = N_EXPERT_GROUPS, MOE_CH, MOE_NB
    tm = x_ref.shape[0]
    h32 = _rms_mod(x_ref[...], g_ref[...], sh_ref[0], sc_ref[0])
    lt = lax.dot_general(wr_ref[...], h32, (((1,), (1,)), ((), ())),
                         precision=lax.Precision.HIGHEST,
                         preferred_element_type=F32) + br_ref[...]
    gates_t, sel = _routing_gates_t(lt)
    oh = jnp.concatenate([m.astype(F32) for m in sel] + [jnp.zeros((16 - ng, tm), F32)], axis=0)
    cnt = jnp.dot(oh.astype(BF16), tri_ref[...], preferred_element_type=F32)
    rank = jnp.sum(oh * cnt, axis=0, keepdims=True)
    n = jnp.sum(oh, axis=1, keepdims=True)
    nch = jnp.floor((n + float(ch - 1)) * (1.0 / ch))
    first = [jnp.zeros((1, 1), F32)]
    for g in range(ng):
        first.append(first[-1] + nch[g:g + 1, :])
    seg = functools.reduce(lambda a, b: a + b,
                           [oh[g:g + 1, :] * (first[g] * float(ch)) for g in range(ng)])
    pos = seg + rank
    rows = lax.broadcasted_iota(jnp.int32, (nb * ch, tm), 0)
    perm = jnp.where(rows == pos.astype(jnp.int32), 1.0, 0.0).astype(BF16)
    hs_ref[0] = jnp.dot(perm, h32.astype(BF16), preferred_element_type=F32).astype(BF16)
    hi = gates_t.astype(BF16).astype(F32)
    r1 = gates_t - hi
    mid = r1.astype(BF16).astype(F32)
    lo = r1 - mid
    meta_t = jnp.concatenate([hi, mid, lo, pos, jnp.zeros((LANES - MOE_POS_LANE - 1, tm), F32)], axis=0)
    meta = meta_t.T
    meta_ref[...] = meta
    gs_ref[0] = jnp.dot(perm, meta.astype(BF16), preferred_element_type=F32)
    lane = lax.broadcasted_iota(jnp.int32, (1, LANES), 1).astype(F32)
    gid = functools.reduce(lambda a, b: a + b,
                           [jnp.where(lane >= first[g + 1], 1.0, 0.0) for g in range(ng)])
    gid_ref[0] = gid.astype(jnp.int32)


def _route_sort(x2, g, shift, scale, wr, br, tri, tiles_per_batch):
    t, d = x2.shape
    tm, rows = MOE_TM, MOE_NB * MOE_CH
    nt = t // tm
    bvec = lambda i: (i // tiles_per_batch, 0, 0)
    return pl.pallas_call(
        _route_sort_kernel,
        grid=(nt,),
        in_specs=[
            pl.BlockSpec((tm, d), lambda i: (i, 0)),
            pl.BlockSpec((1, d), lambda i: (0, 0)),
            pl.BlockSpec((1, 1, d), bvec),
            pl.BlockSpec((1, 1, d), bvec),
            pl.BlockSpec((LANES, d), lambda i: (0, 0)),
            pl.BlockSpec((LANES, 1), lambda i: (0, 0)),
            pl.BlockSpec((tm, tm), lambda i: (0, 0)),
        ],
        out_specs=[
            pl.BlockSpec((1, rows, d), lambda i: (i, 0, 0)),
            pl.BlockSpec((1, rows, LANES), lambda i: (i, 0, 0)),
            pl.BlockSpec((tm, LANES), lambda i: (i, 0)),
            pl.BlockSpec((1, 1, LANES), lambda i: (i, 0, 0)),
        ],
        out_shape=[
            jax.ShapeDtypeStruct((nt, rows, d), BF16),
            jax.ShapeDtypeStruct((nt, rows, LANES), F32),
            jax.ShapeDtypeStruct((t, LANES), F32),
            jax.ShapeDtypeStruct((nt, 1, LANES), jnp.int32),
        ],
        compiler_params=_cparams(1),
        name="moe_route_sort",
    )(x2, g, shift, scale, wr, br, tri)


def _experts_kernel(order_ref, grp_ref, hs_ref, gs_ref, wg_ref, wu_ref, wd_ref, ys_ref):
    ne = EXPERTS_PER_GROUP
    grp = grp_ref[pl.program_id(0)]

    @pl.when(grp < N_EXPERT_GROUPS)
    def _():
        h = hs_ref[0]
        gs = gs_ref[0]
        lane = lax.broadcasted_iota(jnp.int32, gs.shape, 1)
        hid = []
        for e in range(ne):
            idx = grp * ne + e
            pick = jnp.logical_or(jnp.logical_or(lane == idx, lane == idx + N_EXPERTS),
                                  lane == idx + 2 * N_EXPERTS)
            gate = jnp.sum(jnp.where(pick, gs, 0.0), axis=1, keepdims=True)
            a = jnp.dot(h, wg_ref[0, e], preferred_element_type=F32)
            u = jnp.dot(h, wu_ref[0, e], preferred_element_type=F32)
            hid.append(((a / (1.0 + jnp.exp(-a))) * u * gate).astype(BF16))
        hid = jnp.concatenate(hid, axis=1)
        ys_ref[0] = jnp.dot(hid, wd_ref[0], preferred_element_type=F32).astype(ys_ref.dtype)

    @pl.when(grp >= N_EXPERT_GROUPS)
    def _():
        ys_ref[0] = jnp.zeros(ys_ref.shape[1:], ys_ref.dtype)


def _experts(order, grp, hs, gs, wg4, wu4, wd4):
    nchunks, ch, d = hs.shape
    ne, f = wg4.shape[1], wg4.shape[-1]
    wsel = lambda s, order, grp: (jnp.minimum(grp[s], N_EXPERT_GROUPS - 1), 0, 0, 0)
    grid_spec = pltpu.PrefetchScalarGridSpec(
        num_scalar_prefetch=2,
        grid=(nchunks,),
        in_specs=[
            pl.BlockSpec((1, ch, d), lambda s, order, grp: (order[s], 0, 0)),
            pl.BlockSpec((1, ch, LANES), lambda s, order, grp: (order[s], 0, 0)),
            pl.BlockSpec((1, ne, d, f), wsel),
            pl.BlockSpec((1, ne, d, f), wsel),
            pl.BlockSpec((1, ne * f, d),
                         lambda s, order, grp: (jnp.minimum(grp[s], N_EXPERT_GROUPS - 1), 0, 0)),
        ],
        out_specs=pl.BlockSpec((1, ch, d), lambda s, order, grp: (order[s], 0, 0)),
    )
    return pl.pallas_call(
        _experts_kernel,
        grid_spec=grid_spec,
        out_shape=jax.ShapeDtypeStruct((nchunks, ch, d), BF16),
        compiler_params=_cparams(1),
        name="moe_experts",
    )(order, grp, hs, gs, wg4, wu4, wd4)


def _unsort_kernel(x_ref, ys_ref, meta_ref, gate_ref, gf_ref, o_ref, *, final_norm):
    rows = ys_ref.shape[1]
    tm = x_ref.shape[0]
    pos = meta_ref[:, MOE_POS_LANE:MOE_POS_LANE + 1].astype(jnp.int32)
    cols = lax.broadcasted_iota(jnp.int32, (tm, rows), 1)
    perm_t = jnp.where(cols == pos, 1.0, 0.0).astype(BF16)
    y = jnp.dot(perm_t, ys_ref[0], preferred_element_type=F32)
    y = x_ref[...] + gate_ref[0] * y
    if final_norm:
        y = y * lax.rsqrt(jnp.mean(y * y, axis=-1, keepdims=True) + EPS) * gf_ref[...]
    o_ref[...] = y


def _unsort_residual(x2, ys, meta, gate, gfinal, final_norm, tiles_per_batch):
    t, d = x2.shape
    tm = MOE_TM
    nt, rows, _ = ys.shape
    kern = functools.partial(_unsort_kernel, final_norm=final_norm)
    return pl.pallas_call(
        kern,
        grid=(nt,),
        in_specs=[
            pl.BlockSpec((tm, d), lambda i: (i, 0)),
            pl.BlockSpec((1, rows, d), lambda i: (i, 0, 0)),
            pl.BlockSpec((tm, LANES), lambda i: (i, 0)),
            pl.BlockSpec((1, 1, d), lambda i: (i // tiles_per_batch, 0, 0)),
            pl.BlockSpec((1, d), lambda i: (0, 0)),
        ],
        out_specs=pl.BlockSpec((tm, d), lambda i: (i, 0)),
        out_shape=jax.ShapeDtypeStruct((t, d), F32),
        compiler_params=_cparams(1),
        name="moe_unsort_residual",
    )(x2, ys, meta, gate, gfinal)


def moe_layer(x, g, sh2, sc2, g2, w_rg, b_rg, w_re, b_re, w_gate, w_up, w_down, layer, gfinal, final_norm):
    b, s, d = x.shape
    ng, ne = N_EXPERT_GROUPS, EXPERTS_PER_GROUP
    f = w_gate.shape[-1]
    assert s % MOE_TM == 0
    nr = ng + N_EXPERTS
    wr = jnp.concatenate([w_rg.T, w_re.reshape(d, N_EXPERTS).T,
                          jnp.zeros((LANES - nr, d), F32)], axis=0)
    br = jnp.concatenate([b_rg, b_re.reshape(N_EXPERTS),
                          jnp.zeros((LANES - nr,), F32)]).reshape(LANES, 1)
    ti = np.arange(MOE_TM)
    tri = jnp.asarray(ti[:, None] < ti[None, :], F32).astype(BF16)
    x2 = x.reshape(b * s, d)
    tpb = s // MOE_TM
    hs, gs, meta, gid = _route_sort(x2, g.reshape(1, d), sh2, sc2, wr, br, tri, tpb)
    nt = hs.shape[0]
    gid_flat = gid[:, 0, :MOE_NB].reshape(-1)
    order = jnp.argsort(gid_flat, stable=True).astype(jnp.int32)
    grp = gid_flat[order]
    ys = _experts(order, grp,
                  hs.reshape(nt * MOE_NB, MOE_CH, d), gs.reshape(nt * MOE_NB, MOE_CH, LANES),
                  _to_bf16(w_gate, layer).reshape(ng, ne, d, f),
                  _to_bf16(w_up, layer).reshape(ng, ne, d, f),
                  _to_bf16(w_down, layer).reshape(ng, ne * f, d))
    out = _unsort_residual(x2, ys.reshape(nt, MOE_NB * MOE_CH, d), meta, g2,
                           gfinal.reshape(1, d), final_norm, tpb)
    return out.reshape(b, s, d)


def _fft1_kernel(u_ref, w_ref, twr_ref, twi_ref, o_ref, *, n_sub, width):
    n1 = FFT_N1
    y = jnp.dot(w_ref[...], u_ref[0], preferred_element_type=F32)
    for j in range(n_sub):
        cols = slice(j * width, (j + 1) * width)
        yr, yi = y[:n1, cols], y[n1:, cols]
        twr, twi = twr_ref[j], twi_ref[j]
        o_ref[0, :n1, cols] = (yr * twr - yi * twi).astype(o_ref.dtype)
        o_ref[0, n1:, cols] = (yr * twi + yi * twr).astype(o_ref.dtype)


def _fft_stage1(u3, w1, twr, twi, width):
    b, n1, cols = u3.shape
    n_sub = 8
    tn = n_sub * width
    kern = functools.partial(_fft1_kernel, n_sub=n_sub, width=width)
    return pl.pallas_call(
        kern,
        grid=(b, cols // tn),
        in_specs=[
            pl.BlockSpec((1, n1, tn), lambda i, j: (i, 0, j)),
            pl.BlockSpec((2 * n1, n1), lambda i, j: (0, 0)),
            pl.BlockSpec((n_sub, n1, 1), lambda i, j: (j, 0, 0)),
            pl.BlockSpec((n_sub, n1, 1), lambda i, j: (j, 0, 0)),
        ],
        out_specs=pl.BlockSpec((1, 2 * n1, tn), lambda i, j: (i, 0, j)),
        out_shape=jax.ShapeDtypeStruct((b, 2 * n1, cols), BF16),
        compiler_params=_cparams(2),
        name="fft_stage1",
    )(u3, w1, twr, twi)


def _fft2_kernel(y_ref, g_ref, o_ref):
    kp, n2 = FFT_K1_PER_STEP, FFT_N2
    width = y_ref.shape[-1]
    y = y_ref[0].reshape(2 * kp * n2, width)
    r = jnp.dot(g_ref[...], y, preferred_element_type=F32)
    o_ref[0] = r.reshape(2, n2, kp, width).astype(o_ref.dtype)


def _fft_stage2(y5, gmat):
    b, _, n1, n2, width = y5.shape
    kp = FFT_K1_PER_STEP
    return pl.pallas_call(
        _fft2_kernel,
        grid=(b, n1 // kp),
        in_specs=[
            pl.BlockSpec((1, 2, kp, n2, width), lambda i, j: (i, 0, j, 0, 0)),
            pl.BlockSpec((2 * kp * n2, 2 * kp * n2), lambda i, j: (0, 0)),
        ],
        out_specs=pl.BlockSpec((1, 2, n2, kp, width), lambda i, j: (i, 0, 0, j, 0)),
        out_shape=jax.ShapeDtypeStruct((b, 2, n2, n1, width), F32),
        compiler_params=_cparams(2),
        name="fft_stage2",
    )(y5, gmat)


def _fmix_out_kernel(x_ref, a_ref, cc_ref, sc_ref, w_ref, gate_ref, o_ref):
    gd = cc_ref.shape[0]
    ar, ai = a_ref[0, 0].astype(BF16), a_ref[0, 1].astype(BF16)
    parts = []
    for g in range(N_FOURIER_GROUPS):
        cols = slice(g * gd, (g + 1) * gd)
        parts.append(jnp.dot(ar[:, cols], cc_ref[...], preferred_element_type=F32)
                     + jnp.dot(ai[:, cols], sc_ref[...], preferred_element_type=F32))
    f = jnp.concatenate(parts, axis=1).astype(BF16)
    y = jnp.dot(f, w_ref[...], preferred_element_type=F32)
    o_ref[0] = x_ref[0] + gate_ref[0] * y


def _fmix_out(x, a4, cc, sc, w_bf16, gate, tm):
    b, s, d = x.shape
    gd = cc.shape[0]
    return pl.pallas_call(
        _fmix_out_kernel,
        grid=(b, s // tm),
        in_specs=[
            pl.BlockSpec((1, tm, d), lambda i, j: (i, j, 0)),
            pl.BlockSpec((1, 2, tm, d), lambda i, j: (i, 0, j, 0)),
            pl.BlockSpec((gd, gd), lambda i, j: (0, 0)),
            pl.BlockSpec((gd, gd), lambda i, j: (0, 0)),
            pl.BlockSpec((d, d), lambda i, j: (0, 0)),
            pl.BlockSpec((1, 1, d), lambda i, j: (i, 0, 0)),
        ],
        out_specs=pl.BlockSpec((1, tm, d), lambda i, j: (i, j, 0)),
        out_shape=jax.ShapeDtypeStruct((b, s, d), F32),
        compiler_params=_cparams(2),
        name="fourier_out",
    )(x, a4, cc, sc, w_bf16, gate)


def _dft_tables(seq, gd):
    n1, n2, kp = FFT_N1, FFT_N2, FFT_K1_PER_STEP

    def cs(num, den):
        ang = (2.0 * math.pi / den) * (num % den).astype(np.float64)
        return np.cos(ang), np.sin(ang)

    i1 = np.arange(n1)
    c1, s1 = cs(np.outer(i1, i1), n1)
    w1 = np.concatenate([c1, -s1], axis=0)
    ct, st = cs(np.outer(np.arange(n2), i1), seq)
    twr, twi = ct[:, :, None], -st[:, :, None]
    i2 = np.arange(n2)
    c2, s2 = cs(np.outer(i2, i2), n2)
    m3 = np.block([[c2, s2], [-s2, c2]])
    m3 = m3.reshape(2, n2, 2, n2)
    gmat = np.zeros((2, n2, kp, 2, kp, n2))
    for j in range(kp):
        gmat[:, :, j, :, j, :] = m3
    gmat = gmat.reshape(2 * n2 * kp, 2 * kp * n2)
    ic = np.arange(gd)
    cc, sc = cs(np.outer(ic, ic), gd)
    norm = 1.0 / math.sqrt(seq * gd)
    return tuple(jnp.asarray(t, F32) for t in (w1, twr, twi, gmat, cc * norm, sc * norm))


def kernel(x, c, ada_w, ada_b, norm_mix, norm_ffn, attn_w_in, attn_lam_q1, attn_lam_k1, attn_lam_q2, attn_lam_k2, attn_subln, attn_w_out, fourier_w_in, fourier_w_out, router_group_w, router_group_b, router_expert_w, router_expert_b, expert_w_gate, expert_w_up, expert_w_down, norm_final):
    b, s, d = x.shape
    depth = ada_w.shape[0]
    assert s == FFT_N1 * FFT_N2 and depth == 2

    c8 = jnp.zeros((8, d), F32).at[:b].set(c)
    mod = _ada(c8, ada_w, ada_b.reshape(depth, 1, N_ADA * d))[:, :b]
    mod = mod.reshape(depth, b, N_ADA, 1, d)

    slopes = (2.0 ** (-8.0 * jnp.arange(1, N_DIFF_HEADS + 1, dtype=F32) / N_DIFF_HEADS)
              ).reshape(N_DIFF_HEADS, 1, 1)
    gd = d // N_FOURIER_GROUPS
    w1, twr, twi, gmat, cc, sc = _dft_tables(s, gd)
    w1, gmat, cc, sc = (t.astype(BF16) for t in (w1, gmat, cc, sc))

    for i in range(depth):
        sh1, sc1, g1, sh2, sc2, g2 = [mod[i, :, k] for k in range(N_ADA)]
        j = i // 2
        if i % 2 == 0:
            qscale = jnp.concatenate([jnp.full((d,), LOG2E * DIFF_HEAD_DIM ** -0.5, F32),
                                      jnp.ones((2 * d,), F32)])
            w_in = (attn_w_in[j] * qscale).astype(BF16)
            qkv = _norm_mod_matmul(x, norm_mix[i].reshape(1, d), sh1, sc1, w_in, tm=512)
            lam_init = 0.8 - 0.6 * math.exp(-0.3 * i)
            o = _diff_attention(qkv, slopes,
                                attn_lam_q1[j].reshape(1, -1), attn_lam_k1[j].reshape(1, -1),
                                attn_lam_q2[j].reshape(1, -1), attn_lam_k2[j].reshape(1, -1),
                                attn_subln[j].reshape(-1, 1), lam_init)
            x = _proj_residual(x, o, attn_w_out[j].astype(BF16), g1, tm=512)
        else:
            u = _norm_mod_matmul(x, norm_mix[i].reshape(1, d), sh1, sc1,
                                 fourier_w_in[j].astype(BF16), tm=512)
            y = _fft_stage1(u.reshape(b, FFT_N1, FFT_N2 * d), w1, twr, twi, d)
            a = _fft_stage2(y.reshape(b, 2, FFT_N1, FFT_N2, d), gmat)
            x = _fmix_out(x, a.reshape(b, 2, s, d), cc, sc, fourier_w_out[j].astype(BF16), g1, tm=512)

        x = moe_layer(x, norm_ffn[i], sh2, sc2, g2, router_group_w[i], router_group_b[i],
                      router_expert_w[i], router_expert_b[i], expert_w_gate, expert_w_up,
                      expert_w_down, i, norm_final, i == depth - 1)
    return x
```

```python
import functools
import math

import numpy as np
import jax
import jax.numpy as jnp
from jax import lax
from jax.experimental import pallas as pl
from jax.experimental.pallas import tpu as pltpu

F32 = jnp.float32
BF16 = jnp.bfloat16

EPS = 1e-6
N_DIFF_HEADS = 8
DIFF_HEAD_DIM = 64
DIFF_V_DIM = 128
N_FOURIER_GROUPS = 4
N_EXPERT_GROUPS = 4
EXPERTS_PER_GROUP = 4
N_EXPERTS = 16
N_ADA = 6

LANES = 128
VMEM_LIMIT_BYTES = 56 * 1024 * 1024

FFT_N1 = 128
FFT_N2 = 64
FFT_K1_PER_STEP = 8


def _cparams(n_axes):
    return pltpu.CompilerParams(
        dimension_semantics=("arbitrary",) * n_axes,
        vmem_limit_bytes=VMEM_LIMIT_BYTES,
    )


def _rms_mod(x, g, shift, scale):
    y = x * lax.rsqrt(jnp.mean(x * x, axis=-1, keepdims=True) + EPS)
    return (y * g) * (1.0 + scale) + shift


CAST_BLOCK_BYTES = 4 * 1024 * 1024


def _cast_kernel(w_ref, o_ref):
    o_ref[...] = w_ref[...].astype(o_ref.dtype)


def _to_bf16(w_layers, layer):
    cols = w_layers.shape[-1]
    w3 = w_layers.reshape(w_layers.shape[0], -1, cols)
    rows = w3.shape[1]
    tr = min(CAST_BLOCK_BYTES // (4 * cols), rows)
    assert rows % tr == 0
    out = pl.pallas_call(
        _cast_kernel,
        grid=(rows // tr,),
        in_specs=[pl.BlockSpec((1, tr, cols), lambda i: (layer, i, 0))],
        out_specs=pl.BlockSpec((1, tr, cols), lambda i: (0, i, 0)),
        out_shape=jax.ShapeDtypeStruct((1, rows, cols), BF16),
        compiler_params=_cparams(1),
        name="cast_bf16",
    )(w3)
    return out.reshape(w_layers.shape[1:])


def _ada_kernel(c_ref, w_ref, b_ref, o_ref):
    c = c_ref[...]
    cond = c / (1.0 + jnp.exp(-c))
    o_ref[0] = jnp.dot(cond, w_ref[0], precision=lax.Precision.HIGHEST,
                       preferred_element_type=F32) + b_ref[0]


def _ada(c8, ada_w, ada_b3):
    depth, d, n = ada_w.shape
    tn = 1536
    return pl.pallas_call(
        _ada_kernel,
        grid=(depth, n // tn),
        in_specs=[
            pl.BlockSpec((8, d), lambda i, j: (0, 0)),
            pl.BlockSpec((1, d, tn), lambda i, j: (i, 0, j)),
            pl.BlockSpec((1, 1, tn), lambda i, j: (i, 0, j)),
        ],
        out_specs=pl.BlockSpec((1, 8, tn), lambda i, j: (i, 0, j)),
        out_shape=jax.ShapeDtypeStruct((depth, 8, n), F32),
        compiler_params=_cparams(2),
        name="ada_mod",
    )(c8, ada_w, ada_b3)


def _nmm_kernel(x_ref, g_ref, sh_ref, sc_ref, w_ref, o_ref):
    h = _rms_mod(x_ref[0], g_ref[...], sh_ref[0], sc_ref[0])
    o_ref[0] = jnp.dot(h.astype(BF16), w_ref[...],
                       preferred_element_type=F32).astype(o_ref.dtype)


def _norm_mod_matmul(x, g, shift, scale, w_bf16, tm):
    b, s, d = x.shape
    n = w_bf16.shape[1]
    return pl.pallas_call(
        _nmm_kernel,
        grid=(b, s // tm),
        in_specs=[
            pl.BlockSpec((1, tm, d), lambda i, j: (i, j, 0)),
            pl.BlockSpec((1, d), lambda i, j: (0, 0)),
            pl.BlockSpec((1, 1, d), lambda i, j: (i, 0, 0)),
            pl.BlockSpec((1, 1, d), lambda i, j: (i, 0, 0)),
            pl.BlockSpec((d, n), lambda i, j: (0, 0)),
        ],
        out_specs=pl.BlockSpec((1, tm, n), lambda i, j: (i, j, 0)),
        out_shape=jax.ShapeDtypeStruct((b, s, n), BF16),
        compiler_params=_cparams(2),
        name="norm_mod_matmul",
    )(x, g, shift, scale, w_bf16)


ATT_TQ = 256
ATT_TK = 512
ATT_PAIR = 2 * ATT_TK
ATT_NQP = ATT_PAIR // ATT_TQ
LOG2E = 1.4426950408889634
ATT_SKIP_BITS = 40.0
ATT_FAST_MAX_U = 36.0


def _attn_bounds_kernel(q_ref, k_ref, slope_ref, lo_ref, hi_ref, fast_ref, ub_ref, *, seq):
    tq, pair = ATT_TQ, ATT_PAIR
    nq = seq // tq
    lane = lax.broadcasted_iota(jnp.int32, (seq, 2 * DIFF_HEAD_DIM), 1)
    first = lane < DIFF_HEAD_DIM

    def max_half_norm2(ref):
        x = ref[0].astype(F32)
        xx = x * x
        n1 = jnp.sum(jnp.where(first, xx, 0.0), axis=1, keepdims=True)
        n2 = jnp.sum(jnp.where(first, 0.0, xx), axis=1, keepdims=True)
        return jnp.maximum(n1, n2)

    qn2 = jnp.max(max_half_norm2(q_ref).reshape(nq, tq, 1), axis=1)
    kn2 = jnp.max(max_half_norm2(k_ref), axis=0, keepdims=True)
    ub = jnp.sqrt(qn2 * kn2) * (1.0 + 2.0 ** -6)
    slope = slope_ref[0] * LOG2E
    dist = (2.0 * ub + (ATT_SKIP_BITS + math.log2(seq))) / slope
    dist = jnp.ceil(jnp.minimum(dist, float(seq))).astype(jnp.int32)
    q0 = lax.broadcasted_iota(jnp.int32, (nq, 1), 0) * tq
    jmin = jnp.maximum(q0 - dist + 1, 0)
    jmax = jnp.minimum(q0 + (tq - 1) + dist - 1, seq - 1)
    shift = jnp.full((nq, 1), int(math.log2(pair)), jnp.int32)
    lo_ref[0, 0] = lax.shift_right_logical(jmin, shift)
    hi_ref[0, 0] = lax.shift_right_logical(jmax, shift) + 1
    fast_ref[0, 0] = (ub <= ATT_FAST_MAX_U).astype(jnp.int32)
    ub_ref[0, 0] = ub


def _attn_bounds(qkv, slopes):
    b, s, _ = qkv.shape
    h, dv = N_DIFF_HEADS, DIFF_V_DIM
    nq = s // ATT_TQ
    kern = functools.partial(_attn_bounds_kernel, seq=s)
    oblk = pl.BlockSpec((1, 1, nq, 1), lambda bi, hi: (bi, hi, 0, 0))
    ishape = jax.ShapeDtypeStruct((b, h, nq, 1), jnp.int32)
    return pl.pallas_call(
        kern,
        grid=(b, h),
        in_specs=[
            pl.BlockSpec((1, s, dv), lambda bi, hi: (bi, 0, hi)),
            pl.BlockSpec((1, s, dv), lambda bi, hi: (bi, 0, h + hi)),
            pl.BlockSpec((1, 1, 1), lambda bi, hi: (hi, 0, 0)),
        ],
        out_specs=[oblk, oblk, oblk, oblk],
        out_shape=[ishape, ishape, ishape, jax.ShapeDtypeStruct((b, h, nq, 1), F32)],
        compiler_params=_cparams(2),
        name="attn_bounds",
    )(qkv, qkv, slopes)


def _attn_kernel(lo_ref, hi_ref, fast_ref, q_ref, k_ref, v_ref, ub_ref, slope_ref,
                 lq1_ref, lk1_ref, lq2_ref, lk2_ref, sub_ref,
                 o_ref, vt_ref, bias_ref, u_ref, m_ref, l_ref, acc_ref, *, lam_init, seq):
    tq, tk, pair, nqp = ATT_TQ, ATT_TK, ATT_PAIR, ATT_NQP
    dh = DIFF_HEAD_DIM
    qi = pl.program_id(2)
    tile = (pl.program_id(0) * N_DIFF_HEADS + pl.program_id(1)) * (seq // tq) + qi
    lo, hi = lo_ref[tile], hi_ref[tile]
    slope = slope_ref[0] * LOG2E

    @pl.when(qi == 0)
    def _():
        for c in range(seq // 512):
            vt_ref[:, c * 512:(c + 1) * 512] = v_ref[0, c * 512:(c + 1) * 512, :].T
        for s in range(2):
            rows = slice(s * tk, (s + 1) * tk)
            r = lax.broadcasted_iota(jnp.int32, (tk, 2 * tq), 0) + s * tk
            c = lax.broadcasted_iota(jnp.int32, (tk, 2 * tq), 1)
            c = jnp.where(c >= tq, c - tq, c)
            rc = (r - c).astype(F32)
            bias_ref[0, rows, :] = slope * rc
            bias_ref[1, rows, :] = -(slope * rc)
            for w in range(nqp):
                bias_ref[2 + w, rows, :] = -(slope * jnp.abs(rc - float(w * tq)))

    qt = q_ref[0].T
    z = jnp.zeros((dh, tq), BF16)
    qbd = jnp.concatenate([jnp.concatenate([qt[:dh], z], axis=1),
                           jnp.concatenate([z, qt[dh:]], axis=1)], axis=0)

    l_ref[...] = jnp.zeros(l_ref.shape, F32)
    acc_ref[...] = jnp.zeros(acc_ref.shape, F32)

    kd = qi // nqp
    w = qi % nqp

    def mode(kp):
        delta = (kp * pair - qi * tq).astype(F32)
        idx = jnp.where(kp < kd, 0, jnp.where(kp > kd, 1, 2 + w))
        sgn = jnp.where(kp < kd, delta, jnp.where(kp > kd, -delta, 0.0))
        return idx, slope * sgn

    def scores(kp, idx, s):
        k0 = pl.multiple_of(kp * pair + s * tk, tk)
        kblk = k_ref[0, pl.ds(k0, tk), :]
        u = jnp.dot(kblk, qbd, preferred_element_type=F32)
        return u + bias_ref[idx, s * tk:(s + 1) * tk, :]

    def values(kp, s):
        k0 = pl.multiple_of(kp * pair + s * tk, tk)
        return vt_ref[:, pl.ds(k0, tk)]

    def pipelined(qk_phase, sm_phase):
        idx0, _ = mode(lo)
        carry0 = qk_phase(lo, idx0, 0)

        def body(kp, carry_a):
            idx, cst = mode(kp)
            carry_b = qk_phase(kp, idx, 1)
            sm_phase(kp, cst, 0, carry_a)
            kn = jnp.minimum(kp + 1, hi - 1)
            idx_n, _ = mode(kn)
            carry_a_next = qk_phase(kn, idx_n, 0)
            sm_phase(kp, cst, 1, carry_b)
            return carry_a_next

        n_double = (hi - lo) // 2
        carry_a = lax.fori_loop(0, n_double,
                                lambda i, c: body(lo + 2 * i + 1, body(lo + 2 * i, c)), carry0)
        lax.fori_loop(lo + 2 * n_double, hi, body, carry_a)

    def fixed_reference():
        ub = ub_ref[0]

        def qk_phase(kp, idx, s):
            u_ref[s] = scores(kp, idx, s)
            return 0

        def sm_phase(kp, cst, s, _):
            p = jnp.exp2(u_ref[s] + (cst - ub))
            l_ref[...] += jnp.sum(p, axis=0, keepdims=True)
            acc_ref[...] += jnp.dot(values(kp, s), p.astype(BF16), preferred_element_type=F32)

        pipelined(qk_phase, sm_phase)

    def running_max():
        m_ref[...] = jnp.full(m_ref.shape, -jnp.inf, F32)

        def qk_phase(kp, idx, s):
            v = scores(kp, idx, s)
            u_ref[s] = v
            return jnp.max(v, axis=0, keepdims=True)

        def sm_phase(kp, cst, s, mx):
            m_old = m_ref[...]
            m_new = jnp.maximum(m_old, mx + cst)
            alpha = jnp.exp2(m_old - m_new)
            p = jnp.exp2(u_ref[s] - (m_new - cst))
            l_ref[...] = alpha * l_ref[...] + jnp.sum(p, axis=0, keepdims=True)
            m_ref[...] = m_new
            acc_ref[...] = alpha * acc_ref[...] + jnp.dot(values(kp, s), p.astype(BF16),
                                                          preferred_element_type=F32)

        pipelined(qk_phase, sm_phase)

    lax.cond(fast_ref[tile] == 1, fixed_reference, running_max)

    lam = (jnp.exp(jnp.sum(lq1_ref[...] * lk1_ref[...], axis=-1, keepdims=True))
           - jnp.exp(jnp.sum(lq2_ref[...] * lk2_ref[...], axis=-1, keepdims=True))
           + lam_init)
    acc = acc_ref[...]
    l = l_ref[...]
    ot = acc[:, :tq] / l[:, :tq] - lam * (acc[:, tq:] / l[:, tq:])
    ms = jnp.mean(ot * ot, axis=0, keepdims=True)
    ot = ot * lax.rsqrt(ms + EPS) * sub_ref[...] * (1.0 - lam_init)
    o_ref[0] = ot.T.astype(o_ref.dtype)


def _diff_attention(qkv, slopes, lq1, lk1, lq2, lk2, subln_col, lam_init):
    b, s, _ = qkv.shape
    h, dv = N_DIFF_HEADS, DIFF_V_DIM
    tq, tk = ATT_TQ, ATT_TK
    nq = s // tq
    assert s % ATT_PAIR == 0
    lo, hi, fast, ub = _attn_bounds(qkv, slopes)
    kern = functools.partial(_attn_kernel, lam_init=lam_init, seq=s)
    vec = pl.BlockSpec((1, DIFF_HEAD_DIM), lambda bi, hi, qi, *_: (0, 0))
    grid_spec = pltpu.PrefetchScalarGridSpec(
        num_scalar_prefetch=3,
        grid=(b, h, nq),
        in_specs=[
            pl.BlockSpec((1, tq, dv), lambda bi, hi, qi, *_: (bi, qi, hi)),
            pl.BlockSpec((1, s, dv), lambda bi, hi, qi, *_: (bi, 0, h + hi)),
            pl.BlockSpec((1, s, dv), lambda bi, hi, qi, *_: (bi, 0, 2 * h + hi)),
            pl.BlockSpec((1, 1, 1), lambda bi, hi, qi, *_: ((bi * h + hi) * nq + qi, 0, 0)),
            pl.BlockSpec((1, 1, 1), lambda bi, hi, qi, *_: (hi, 0, 0)),
            vec, vec, vec, vec,
            pl.BlockSpec((dv, 1), lambda bi, hi, qi, *_: (0, 0)),
        ],
        out_specs=pl.BlockSpec((1, tq, dv), lambda bi, hi, qi, *_: (bi, qi, hi)),
        scratch_shapes=[
            pltpu.VMEM((dv, s), BF16),
            pltpu.VMEM((2 + ATT_NQP, ATT_PAIR, 2 * tq), F32),
            pltpu.VMEM((2, tk, 2 * tq), F32),
            pltpu.VMEM((1, 2 * tq), F32),
            pltpu.VMEM((1, 2 * tq), F32),
            pltpu.VMEM((dv, 2 * tq), F32),
        ],
    )
    return pl.pallas_call(
        kern,
        grid_spec=grid_spec,
        out_shape=jax.ShapeDtypeStruct((b, s, h * dv), BF16),
        compiler_params=_cparams(3),
        name="diff_attention",
    )(lo.reshape(-1), hi.reshape(-1), fast.reshape(-1),
      qkv, qkv, qkv, ub.reshape(-1, 1, 1), slopes, lq1, lk1, lq2, lk2, subln_col)


def _proj_res_kernel(x_ref, a_ref, w_ref, gate_ref, o_ref):
    y = jnp.dot(a_ref[0], w_ref[...], preferred_element_type=F32)
    o_ref[0] = x_ref[0] + gate_ref[0] * y


def _proj_residual(x, a, w_bf16, gate, tm):
    b, s, d = x.shape
    kdim = a.shape[-1]
    return pl.pallas_call(
        _proj_res_kernel,
        grid=(b, s // tm),
        in_specs=[
            pl.BlockSpec((1, tm, d), lambda i, j: (i, j, 0)),
            pl.BlockSpec((1, tm, kdim), lambda i, j: (i, j, 0)),
            pl.BlockSpec((kdim, d), lambda i, j: (0, 0)),
            pl.BlockSpec((1, 1, d), lambda i, j: (i, 0, 0)),
        ],
        out_specs=pl.BlockSpec((1, tm, d), lambda i, j: (i, j, 0)),
        out_shape=jax.ShapeDtypeStruct((b, s, d), F32),
        compiler_params=_cparams(2),
        name="proj_residual",
    )(x, a, w_bf16, gate)


def _routing_gates_t(lt):
    ng, ne = N_EXPERT_GROUPS, EXPERTS_PER_GROUP
    gl = [lt[g:g + 1, :] for g in range(ng)]
    gmax = functools.reduce(jnp.maximum, gl)
    den = functools.reduce(lambda a, b: a + b, [jnp.exp(v - gmax) for v in gl])
    g_w = 1.0 / den
    sel = []
    taken = jnp.zeros(gmax.shape, jnp.bool_)
    for g in range(ng):
        s = jnp.logical_and(gl[g] == gmax, jnp.logical_not(taken))
        taken = jnp.logical_or(taken, s)
        sel.append(s)
    ev = []
    for e in range(ne):
        v = jnp.zeros_like(gmax)
        for g in range(ng):
            row = ng + ne * g + e
            v = jnp.where(sel[g], lt[row:row + 1, :], v)
        ev.append(v)
    top1 = functools.reduce(jnp.maximum, ev)
    is1 = []
    taken = jnp.zeros(gmax.shape, jnp.bool_)
    for e in range(ne):
        s = jnp.logical_and(ev[e] == top1, jnp.logical_not(taken))
        taken = jnp.logical_or(taken, s)
        is1.append(s)
    rest = [jnp.where(is1[e], -jnp.inf, ev[e]) for e in range(ne)]
    top2 = functools.reduce(jnp.maximum, rest)
    is2 = []
    taken = jnp.zeros(gmax.shape, jnp.bool_)
    for e in range(ne):
        s = jnp.logical_and(jnp.logical_and(rest[e] == top2, jnp.logical_not(is1[e])),
                            jnp.logical_not(taken))
        taken = jnp.logical_or(taken, s)
        is2.append(s)
    ex = jnp.exp(top2 - top1)
    w1 = 1.0 / (1.0 + ex)
    w2 = ex / (1.0 + ex)
    ew = [jnp.where(is1[e], w1, jnp.where(is2[e], w2, 0.0)) for e in range(ne)]
    rows = [jnp.where(sel[g], g_w * ew[e], 0.0) for g in range(ng) for e in range(ne)]
    return jnp.concatenate(rows, axis=0), sel


MOE_TM = 1024
MOE_CH = 128
MOE_NB = 11
MOE_POS_LANE = 3 * N_EXPERTS


def _route_sort_kernel(x_ref, g_ref, sh_ref, sc_ref, wr_ref, br_ref, tri_ref,
                       hs_ref, gs_ref, meta_ref, gid_ref):
    ng, ch, nb = N_EXPERT_GROUPS, MOE_CH, MOE_NB
    tm = x_ref.shape[0]
    h32 = _rms_mod(x_ref[...], g_ref[...], sh_ref[0], sc_ref[0])
    lt = lax.dot_general(wr_ref[...], h32, (((1,), (1,)), ((), ())),
                         precision=lax.Precision.HIGHEST,
                         preferred_element_type=F32) + br_ref[...]
    gates_t, sel = _routing_gates_t(lt)
    oh = jnp.concatenate([m.astype(F32) for m in sel] + [jnp.zeros((16 - ng, tm), F32)], axis=0)
    cnt = jnp.dot(oh.astype(BF16), tri_ref[...], preferred_element_type=F32)
    rank = jnp.sum(oh * cnt, axis=0, keepdims=True)
    n = jnp.sum(oh, axis=1, keepdims=True)
    nch = jnp.floor((n + float(ch - 1)) * (1.0 / ch))
    first = [jnp.zeros((1, 1), F32)]
    for g in range(ng):
        first.append(first[-1] + nch[g:g + 1, :])
    seg = functools.reduce(lambda a, b: a + b,
                           [oh[g:g + 1, :] * (first[g] * float(ch)) for g in range(ng)])
    pos = seg + rank
    rows = lax.broadcasted_iota(jnp.int32, (nb * ch, tm), 0)
    perm = jnp.where(rows == pos.astype(jnp.int32), 1.0, 0.0).astype(BF16)
    hs_ref[0] = jnp.dot(perm, h32.astype(BF16), preferred_element_type=F32).astype(BF16)
    hi = gates_t.astype(BF16).astype(F32)
    r1 = gates_t - hi
    mid = r1.astype(BF16).astype(F32)
    lo = r1 - mid
    meta_t = jnp.concatenate([hi, mid, lo, pos, jnp.zeros((LANES - MOE_POS_LANE - 1, tm), F32)], axis=0)
    meta = meta_t.T
    meta_ref[...] = meta
    gs_ref[0] = jnp.dot(perm, meta.astype(BF16), preferred_element_type=F32)
    lane = lax.broadcasted_iota(jnp.int32, (1, LANES), 1).astype(F32)
    gid = functools.reduce(lambda a, b: a + b,
                           [jnp.where(lane >= first[g + 1], 1.0, 0.0) for g in range(ng)])
    gid_ref[0] = gid.astype(jnp.int32)


def _route_sort(x2, g, shift, scale, wr, br, tri, tiles_per_batch):
    t, d = x2.shape
    tm, rows = MOE_TM, MOE_NB * MOE_CH
    nt = t // tm
    bvec = lambda i: (i // tiles_per_batch, 0, 0)
    return pl.pallas_call(
        _route_sort_kernel,
        grid=(nt,),
        in_specs=[
            pl.BlockSpec((tm, d), lambda i: (i, 0)),
            pl.BlockSpec((1, d), lambda i: (0, 0)),
            pl.BlockSpec((1, 1, d), bvec),
            pl.BlockSpec((1, 1, d), bvec),
            pl.BlockSpec((LANES, d), lambda i: (0, 0)),
            pl.BlockSpec((LANES, 1), lambda i: (0, 0)),
            pl.BlockSpec((tm, tm), lambda i: (0, 0)),
        ],
        out_specs=[
            pl.BlockSpec((1, rows, d), lambda i: (i, 0, 0)),
            pl.BlockSpec((1, rows, LANES), lambda i: (i, 0, 0)),
            pl.BlockSpec((tm, LANES), lambda i: (i, 0)),
            pl.BlockSpec((1, 1, LANES), lambda i: (i, 0, 0)),
        ],
        out_shape=[
            jax.ShapeDtypeStruct((nt, rows, d), BF16),
            jax.ShapeDtypeStruct((nt, rows, LANES), F32),
            jax.ShapeDtypeStruct((t, LANES), F32),
            jax.ShapeDtypeStruct((nt, 1, LANES), jnp.int32),
        ],
        compiler_params=_cparams(1),
        name="moe_route_sort",
    )(x2, g, shift, scale, wr, br, tri)


def _experts_kernel(order_ref, grp_ref, hs_ref, gs_ref, wg_ref, wu_ref, wd_ref, ys_ref):
    ne = EXPERTS_PER_GROUP
    grp = grp_ref[pl.program_id(0)]

    @pl.when(grp < N_EXPERT_GROUPS)
    def _():
        h = hs_ref[0]
        gs = gs_ref[0]
        lane = lax.broadcasted_iota(jnp.int32, gs.shape, 1)
        hid = []
        for e in range(ne):
            idx = grp * ne + e
            pick = jnp.logical_or(jnp.logical_or(lane == idx, lane == idx + N_EXPERTS),
                                  lane == idx + 2 * N_EXPERTS)
            gate = jnp.sum(jnp.where(pick, gs, 0.0), axis=1, keepdims=True)
            a = jnp.dot(h, wg_ref[0, e], preferred_element_type=F32)
            u = jnp.dot(h, wu_ref[0, e], preferred_element_type=F32)
            hid.append(((a / (1.0 + jnp.exp(-a))) * u * gate).astype(BF16))
        hid = jnp.concatenate(hid, axis=1)
        ys_ref[0] = jnp.dot(hid, wd_ref[0], preferred_element_type=F32).astype(ys_ref.dtype)

    @pl.when(grp >= N_EXPERT_GROUPS)
    def _():
        ys_ref[0] = jnp.zeros(ys_ref.shape[1:], ys_ref.dtype)


def _experts(order, grp, hs, gs, wg4, wu4, wd4):
    nchunks, ch, d = hs.shape
    ne, f = wg4.shape[1], wg4.shape[-1]
    wsel = lambda s, order, grp: (jnp.minimum(grp[s], N_EXPERT_GROUPS - 1), 0, 0, 0)
    grid_spec = pltpu.PrefetchScalarGridSpec(
        num_scalar_prefetch=2,
        grid=(nchunks,),
        in_specs=[
            pl.BlockSpec((1, ch, d), lambda s, order, grp: (order[s], 0, 0)),
            pl.BlockSpec((1, ch, LANES), lambda s, order, grp: (order[s], 0, 0)),
            pl.BlockSpec((1, ne, d, f), wsel),
            pl.BlockSpec((1, ne, d, f), wsel),
            pl.BlockSpec((1, ne * f, d),
                         lambda s, order, grp: (jnp.minimum(grp[s], N_EXPERT_GROUPS - 1), 0, 0)),
        ],
        out_specs=pl.BlockSpec((1, ch, d), lambda s, order, grp: (order[s], 0, 0)),
    )
    return pl.pallas_call(
        _experts_kernel,
        grid_spec=grid_spec,
        out_shape=jax.ShapeDtypeStruct((nchunks, ch, d), BF16),
        compiler_params=_cparams(1),
        name="moe_experts",
    )(order, grp, hs, gs, wg4, wu4, wd4)


def _unsort_kernel(x_ref, ys_ref, meta_ref, gate_ref, gf_ref, o_ref, *, final_norm):
    rows = ys_ref.shape[1]
    tm = x_ref.shape[0]
    pos = meta_ref[:, MOE_POS_LANE:MOE_POS_LANE + 1].astype(jnp.int32)
    cols = lax.broadcasted_iota(jnp.int32, (tm, rows), 1)
    perm_t = jnp.where(cols == pos, 1.0, 0.0).astype(BF16)
    y = jnp.dot(perm_t, ys_ref[0], preferred_element_type=F32)
    y = x_ref[...] + gate_ref[0] * y
    if final_norm:
        y = y * lax.rsqrt(jnp.mean(y * y, axis=-1, keepdims=True) + EPS) * gf_ref[...]
    o_ref[...] = y


def _unsort_residual(x2, ys, meta, gate, gfinal, final_norm, tiles_per_batch):
    t, d = x2.shape
    tm = MOE_TM
    nt, rows, _ = ys.shape
    kern = functools.partial(_unsort_kernel, final_norm=final_norm)
    return pl.pallas_call(
        kern,
        grid=(nt,),
        in_specs=[
            pl.BlockSpec((tm, d), lambda i: (i, 0)),
            pl.BlockSpec((1, rows, d), lambda i: (i, 0, 0)),
            pl.BlockSpec((tm, LANES), lambda i: (i, 0)),
            pl.BlockSpec((1, 1, d), lambda i: (i // tiles_per_batch, 0, 0)),
            pl.BlockSpec((1, d), lambda i: (0, 0)),
        ],
        out_specs=pl.BlockSpec((tm, d), lambda i: (i, 0)),
        out_shape=jax.ShapeDtypeStruct((t, d), F32),
        compiler_params=_cparams(1),
        name="moe_unsort_residual",
    )(x2, ys, meta, gate, gfinal)


def moe_layer(x, g, sh2, sc2, g2, w_rg, b_rg, w_re, b_re, w_gate, w_up, w_down, layer, gfinal, final_norm):
    b, s, d = x.shape
    ng, ne = N_EXPERT_GROUPS, EXPERTS_PER_GROUP
    f = w_gate.shape[-1]
    assert s % MOE_TM == 0
    nr = ng + N_EXPERTS
    wr = jnp.concatenate([w_rg.T, w_re.reshape(d, N_EXPERTS).T,
                          jnp.zeros((LANES - nr, d), F32)], axis=0)
    br = jnp.concatenate([b_rg, b_re.reshape(N_EXPERTS),
                          jnp.zeros((LANES - nr,), F32)]).reshape(LANES, 1)
    ti = np.arange(MOE_TM)
    tri = jnp.asarray(ti[:, None] < ti[None, :], F32).astype(BF16)
    x2 = x.reshape(b * s, d)
    tpb = s // MOE_TM
    hs, gs, meta, gid = _route_sort(x2, g.reshape(1, d), sh2, sc2, wr, br, tri, tpb)
    nt = hs.shape[0]
    gid_flat = gid[:, 0, :MOE_NB].reshape(-1)
    order = jnp.argsort(gid_flat, stable=True).astype(jnp.int32)
    grp = gid_flat[order]
    ys = _experts(order, grp,
                  hs.reshape(nt * MOE_NB, MOE_CH, d), gs.reshape(nt * MOE_NB, MOE_CH, LANES),
                  _to_bf16(w_gate, layer).reshape(ng, ne, d, f),
                  _to_bf16(w_up, layer).reshape(ng, ne, d, f),
                  _to_bf16(w_down, layer).reshape(ng, ne * f, d))
    out = _unsort_residual(x2, ys.reshape(nt, MOE_NB * MOE_CH, d), meta, g2,
                           gfinal.reshape(1, d), final_norm, tpb)
    return out.reshape(b, s, d)


def _fft1_kernel(u_ref, w_ref, twr_ref, twi_ref, o_ref, *, n_sub, width):
    n1 = FFT_N1
    y = jnp.dot(w_ref[...], u_ref[0], preferred_element_type=F32)
    for j in range(n_sub):
        cols = slice(j * width, (j + 1) * width)
        yr, yi = y[:n1, cols], y[n1:, cols]
        twr, twi = twr_ref[j], twi_ref[j]
        o_ref[0, :n1, cols] = (yr * twr - yi * twi).astype(o_ref.dtype)
        o_ref[0, n1:, cols] = (yr * twi + yi * twr).astype(o_ref.dtype)


def _fft_stage1(u3, w1, twr, twi, width):
    b, n1, cols = u3.shape
    n_sub = 8
    tn = n_sub * width
    kern = functools.partial(_fft1_kernel, n_sub=n_sub, width=width)
    return pl.pallas_call(
        kern,
        grid=(b, cols // tn),
        in_specs=[
            pl.BlockSpec((1, n1, tn), lambda i, j: (i, 0, j)),
            pl.BlockSpec((2 * n1, n1), lambda i, j: (0, 0)),
            pl.BlockSpec((n_sub, n1, 1), lambda i, j: (j, 0, 0)),
            pl.BlockSpec((n_sub, n1, 1), lambda i, j: (j, 0, 0)),
        ],
        out_specs=pl.BlockSpec((1, 2 * n1, tn), lambda i, j: (i, 0, j)),
        out_shape=jax.ShapeDtypeStruct((b, 2 * n1, cols), BF16),
        compiler_params=_cparams(2),
        name="fft_stage1",
    )(u3, w1, twr, twi)


def _fft2_kernel(y_ref, g_ref, o_ref):
    kp, n2 = FFT_K1_PER_STEP, FFT_N2
    width = y_ref.shape[-1]
    y = y_ref[0].reshape(2 * kp * n2, width)
    r = jnp.dot(g_ref[...], y, preferred_element_type=F32)
    o_ref[0] = r.reshape(2, n2, kp, width).astype(o_ref.dtype)


def _fft_stage2(y5, gmat):
    b, _, n1, n2, width = y5.shape
    kp = FFT_K1_PER_STEP
    return pl.pallas_call(
        _fft2_kernel,
        grid=(b, n1 // kp),
        in_specs=[
            pl.BlockSpec((1, 2, kp, n2, width), lambda i, j: (i, 0, j, 0, 0)),
            pl.BlockSpec((2 * kp * n2, 2 * kp * n2), lambda i, j: (0, 0)),
        ],
        out_specs=pl.BlockSpec((1, 2, n2, kp, width), lambda i, j: (i, 0, 0, j, 0)),
        out_shape=jax.ShapeDtypeStruct((b, 2, n2, n1, width), F32),
        compiler_params=_cparams(2),
        name="fft_stage2",
    )(y5, gmat)


def _fmix_out_kernel(x_ref, a_ref, cc_ref, sc_ref, w_ref, gate_ref, o_ref):
    gd = cc_ref.shape[0]
    ar, ai = a_ref[0, 0].astype(BF16), a_ref[0, 1].astype(BF16)
    parts = []
    for g in range(N_FOURIER_GROUPS):
        cols = slice(g * gd, (g + 1) * gd)
        parts.append(jnp.dot(ar[:, cols], cc_ref[...], preferred_element_type=F32)
                     + jnp.dot(ai[:, cols], sc_ref[...], preferred_element_type=F32))
    f = jnp.concatenate(parts, axis=1).astype(BF16)
    y = jnp.dot(f, w_ref[...], preferred_element_type=F32)
    o_ref[0] = x_ref[0] + gate_ref[0] * y


def _fmix_out(x, a4, cc, sc, w_bf16, gate, tm):
    b, s, d = x.shape
    gd = cc.shape[0]
    return pl.pallas_call(
        _fmix_out_kernel,
        grid=(b, s // tm),
        in_specs=[
            pl.BlockSpec((1, tm, d), lambda i, j: (i, j, 0)),
            pl.BlockSpec((1, 2, tm, d), lambda i, j: (i, 0, j, 0)),
            pl.BlockSpec((gd, gd), lambda i, j: (0, 0)),
            pl.BlockSpec((gd, gd), lambda i, j: (0, 0)),
            pl.BlockSpec((d, d), lambda i, j: (0, 0)),
            pl.BlockSpec((1, 1, d), lambda i, j: (i, 0, 0)),
        ],
        out_specs=pl.BlockSpec((1, tm, d), lambda i, j: (i, j, 0)),
        out_shape=jax.ShapeDtypeStruct((b, s, d), F32),
        compiler_params=_cparams(2),
        name="fourier_out",
    )(x, a4, cc, sc, w_bf16, gate)


def _dft_tables(seq, gd):
    n1, n2, kp = FFT_N1, FFT_N2, FFT_K1_PER_STEP

    def cs(num, den):
        ang = (2.0 * math.pi / den) * (num % den).astype(np.float64)
        return np.cos(ang), np.sin(ang)

    i1 = np.arange(n1)
    c1, s1 = cs(np.outer(i1, i1), n1)
    w1 = np.concatenate([c1, -s1], axis=0)
    ct, st = cs(np.outer(np.arange(n2), i1), seq)
    twr, twi = ct[:, :, None], -st[:, :, None]
    i2 = np.arange(n2)
    c2, s2 = cs(np.outer(i2, i2), n2)
    m3 = np.block([[c2, s2], [-s2, c2]])
    m3 = m3.reshape(2, n2, 2, n2)
    gmat = np.zeros((2, n2, kp, 2, kp, n2))
    for j in range(kp):
        gmat[:, :, j, :, j, :] = m3
    gmat = gmat.reshape(2 * n2 * kp, 2 * kp * n2)
    ic = np.arange(gd)
    cc, sc = cs(np.outer(ic, ic), gd)
    norm = 1.0 / math.sqrt(seq * gd)
    return tuple(jnp.asarray(t, F32) for t in (w1, twr, twi, gmat, cc * norm, sc * norm))


def kernel(x, c, ada_w, ada_b, norm_mix, norm_ffn, attn_w_in, attn_lam_q1, attn_lam_k1, attn_lam_q2, attn_lam_k2, attn_subln, attn_w_out, fourier_w_in, fourier_w_out, router_group_w, router_group_b, router_expert_w, router_expert_b, expert_w_gate, expert_w_up, expert_w_down, norm_final):
    b, s, d = x.shape
    depth = ada_w.shape[0]
    assert s == FFT_N1 * FFT_N2 and depth == 2

    c8 = jnp.zeros((8, d), F32).at[:b].set(c)
    mod = _ada(c8, ada_w, ada_b.reshape(depth, 1, N_ADA * d))[:, :b]
    mod = mod.reshape(depth, b, N_ADA, 1, d)

    slopes = (2.0 ** (-8.0 * jnp.arange(1, N_DIFF_HEADS + 1, dtype=F32) / N_DIFF_HEADS)
              ).reshape(N_DIFF_HEADS, 1, 1)
    gd = d // N_FOURIER_GROUPS
    w1, twr, twi, gmat, cc, sc = _dft_tables(s, gd)
    w1, gmat, cc, sc = (t.astype(BF16) for t in (w1, gmat, cc, sc))

    for i in range(depth):
        sh1, sc1, g1, sh2, sc2, g2 = [mod[i, :, k] for k in range(N_ADA)]
        j = i // 2
        if i % 2 == 0:
            qscale = jnp.concatenate([jnp.full((d,), LOG2E * DIFF_HEAD_DIM ** -0.5, F32),
                                      jnp.ones((2 * d,), F32)])
            w_in = (attn_w_in[j] * qscale).astype(BF16)
            qkv = _norm_mod_matmul(x, norm_mix[i].reshape(1, d), sh1, sc1, w_in, tm=512)
            lam_init = 0.8 - 0.6 * math.exp(-0.3 * i)
            o = _diff_attention(qkv, slopes,
                                attn_lam_q1[j].reshape(1, -1), attn_lam_k1[j].reshape(1, -1),
                                attn_lam_q2[j].reshape(1, -1), attn_lam_k2[j].reshape(1, -1),
                                attn_subln[j].reshape(-1, 1), lam_init)
            x = _proj_residual(x, o, attn_w_out[j].astype(BF16), g1, tm=512)
        else:
            u = _norm_mod_matmul(x, norm_mix[i].reshape(1, d), sh1, sc1,
                                 fourier_w_in[j].astype(BF16), tm=512)
            y = _fft_stage1(u.reshape(b, FFT_N1, FFT_N2 * d), w1, twr, twi, d)
            a = _fft_stage2(y.reshape(b, 2, FFT_N1, FFT_N2, d), gmat)
            x = _fmix_out(x, a.reshape(b, 2, s, d), cc, sc, fourier_w_out[j].astype(BF16), g1, tm=512)

        x = moe_layer(x, norm_ffn[i], sh2, sc2, g2, router_group_w[i], router_group_b[i],
                      router_expert_w[i], router_expert_b[i], expert_w_gate, expert_w_up,
                      expert_w_down, i, norm_final, i == depth - 1)
    return x
```

```python
import functools
import math

import numpy as np
import jax
import jax.numpy as jnp
from jax import lax
from jax.experimental import pallas as pl
from jax.experimental.pallas import tpu as pltpu

F32 = jnp.float32
BF16 = jnp.bfloat16

EPS = 1e-6
N_DIFF_HEADS = 8
DIFF_HEAD_DIM = 64
DIFF_V_DIM = 128
N_FOURIER_GROUPS = 4
N_EXPERT_GROUPS = 4
EXPERTS_PER_GROUP = 4
N_EXPERTS = 16
N_ADA = 6

LANES = 128
VMEM_LIMIT_BYTES = 56 * 1024 * 1024

FFT_N1 = 128
FFT_N2 = 64
FFT_K1_PER_STEP = 8


def _cparams(n_axes):
    return pltpu.CompilerParams(
        dimension_semantics=("arbitrary",) * n_axes,
        vmem_limit_bytes=VMEM_LIMIT_BYTES,
    )


def _rms_mod(x, g, shift, scale):
    y = x * lax.rsqrt(jnp.mean(x * x, axis=-1, keepdims=True) + EPS)
    return (y * g) * (1.0 + scale) + shift


CAST_BLOCK_BYTES = 4 * 1024 * 1024


def _cast_kernel(w_ref, o_ref):
    o_ref[...] = w_ref[...].astype(o_ref.dtype)


def _to_bf16(w_layers, layer):
    cols = w_layers.shape[-1]
    w3 = w_layers.reshape(w_layers.shape[0], -1, cols)
    rows = w3.shape[1]
    tr = min(CAST_BLOCK_BYTES // (4 * cols), rows)
    assert rows % tr == 0
    out = pl.pallas_call(
        _cast_kernel,
        grid=(rows // tr,),
        in_specs=[pl.BlockSpec((1, tr, cols), lambda i: (layer, i, 0))],
        out_specs=pl.BlockSpec((1, tr, cols), lambda i: (0, i, 0)),
        out_shape=jax.ShapeDtypeStruct((1, rows, cols), BF16),
        compiler_params=_cparams(1),
        name="cast_bf16",
    )(w3)
    return out.reshape(w_layers.shape[1:])


def _ada_kernel(c_ref, w_ref, b_ref, o_ref):
    c = c_ref[...]
    cond = c / (1.0 + jnp.exp(-c))
    o_ref[0] = jnp.dot(cond, w_ref[0], precision=lax.Precision.HIGHEST,
                       preferred_element_type=F32) + b_ref[0]


def _ada(c8, ada_w, ada_b3):
    depth, d, n = ada_w.shape
    tn = 1536
    return pl.pallas_call(
        _ada_kernel,
        grid=(depth, n // tn),
        in_specs=[
            pl.BlockSpec((8, d), lambda i, j: (0, 0)),
            pl.BlockSpec((1, d, tn), lambda i, j: (i, 0, j)),
            pl.BlockSpec((1, 1, tn), lambda i, j: (i, 0, j)),
        ],
        out_specs=pl.BlockSpec((1, 8, tn), lambda i, j: (i, 0, j)),
        out_shape=jax.ShapeDtypeStruct((depth, 8, n), F32),
        compiler_params=_cparams(2),
        name="ada_mod",
    )(c8, ada_w, ada_b3)


def _nmm_kernel(x_ref, g_ref, sh_ref, sc_ref, w_ref, o_ref):
    h = _rms_mod(x_ref[0], g_ref[...], sh_ref[0], sc_ref[0])
    o_ref[0] = jnp.dot(h.astype(BF16), w_ref[...],
                       preferred_element_type=F32).astype(o_ref.dtype)


def _norm_mod_matmul(x, g, shift, scale, w_bf16, tm):
    b, s, d = x.shape
    n = w_bf16.shape[1]
    return pl.pallas_call(
        _nmm_kernel,
        grid=(b, s // tm),
        in_specs=[
            pl.BlockSpec((1, tm, d), lambda i, j: (i, j, 0)),
            pl.BlockSpec((1, d), lambda i, j: (0, 0)),
            pl.BlockSpec((1, 1, d), lambda i, j: (i, 0, 0)),
            pl.BlockSpec((1, 1, d), lambda i, j: (i, 0, 0)),
            pl.BlockSpec((d, n), lambda i, j: (0, 0)),
        ],
        out_specs=pl.BlockSpec((1, tm, n), lambda i, j: (i, j, 0)),
        out_shape=jax.ShapeDtypeStruct((b, s, n), BF16),
        compiler_params=_cparams(2),
        name="norm_mod_matmul",
    )(x, g, shift, scale, w_bf16)


ATT_TQ = 256
ATT_TK = 512
ATT_PAIR = 2 * ATT_TK
ATT_NQP = ATT_PAIR // ATT_TQ
ATT_QSUB = 2
LOG2E = 1.4426950408889634
ATT_SKIP_BITS = 40.0
ATT_FAST_MAX_U = 36.0


def _attn_bounds_kernel(q_ref, k_ref, slope_ref, lo_ref, hi_ref, fast_ref, ub_ref, *, seq):
    tq, pair = ATT_TQ, ATT_PAIR
    nq = seq // tq
    dd = 2 * DIFF_HEAD_DIM
    r = lax.broadcasted_iota(jnp.int32, (dd, dd), 0)
    c = lax.broadcasted_iota(jnp.int32, (dd, dd), 1)
    halves = jnp.where(c == r // DIFF_HEAD_DIM, 1.0, 0.0).astype(BF16)

    def half_norm2(ref):
        x = ref[0].astype(F32)
        return jnp.dot((x * x).astype(BF16), halves, preferred_element_type=F32)

    qn2 = jnp.max(jnp.max(half_norm2(q_ref).reshape(nq, tq, dd), axis=1), axis=1, keepdims=True)
    kn2 = jnp.max(jnp.max(half_norm2(k_ref), axis=0, keepdims=True), axis=1, keepdims=True)
    ub = jnp.sqrt(qn2 * kn2) * (1.0 + 2.0 ** -6)
    slope = slope_ref[0] * LOG2E
    dist = (2.0 * ub + (ATT_SKIP_BITS + math.log2(seq))) / slope
    dist = jnp.ceil(jnp.minimum(dist, float(seq))).astype(jnp.int32)
    q0 = lax.broadcasted_iota(jnp.int32, (nq, 1), 0) * tq
    jmin = jnp.maximum(q0 - dist + 1, 0)
    jmax = jnp.minimum(q0 + (tq - 1) + dist - 1, seq - 1)
    shift = jnp.full((nq, 1), int(math.log2(pair)), jnp.int32)
    lo_ref[0, 0] = lax.shift_right_logical(jmin, shift)
    hi_ref[0, 0] = lax.shift_right_logical(jmax, shift) + 1
    fast_ref[0, 0] = (ub <= ATT_FAST_MAX_U).astype(jnp.int32)
    ub_ref[0, 0] = ub


def _attn_bounds(qkv, slopes):
    b, s, _ = qkv.shape
    h, dv = N_DIFF_HEADS, DIFF_V_DIM
    nq = s // ATT_TQ
    kern = functools.partial(_attn_bounds_kernel, seq=s)
    oblk = pl.BlockSpec((1, 1, nq, 1), lambda bi, hi: (bi, hi, 0, 0))
    ishape = jax.ShapeDtypeStruct((b, h, nq, 1), jnp.int32)
    return pl.pallas_call(
        kern,
        grid=(b, h),
        in_specs=[
            pl.BlockSpec((1, s, dv), lambda bi, hi: (bi, 0, hi)),
            pl.BlockSpec((1, s, dv), lambda bi, hi: (bi, 0, h + hi)),
            pl.BlockSpec((1, 1, 1), lambda bi, hi: (hi, 0, 0)),
        ],
        out_specs=[oblk, oblk, oblk, oblk],
        out_shape=[ishape, ishape, ishape, jax.ShapeDtypeStruct((b, h, nq, 1), F32)],
        compiler_params=_cparams(2),
        name="attn_bounds",
    )(qkv, qkv, slopes)


def _attn_kernel(lo_ref, hi_ref, fast_ref, q_ref, k_ref, v_ref, ub_ref, slope_ref,
                 lq1_ref, lk1_ref, lq2_ref, lk2_ref, sub_ref,
                 o_ref, vt_ref, bias_ref, u_ref, m_ref, l_ref, acc_ref, *, lam_init, seq):
    tq, tk, pair, nqp = ATT_TQ, ATT_TK, ATT_PAIR, ATT_NQP
    dh = DIFF_HEAD_DIM
    step = pl.program_id(2)
    slope = slope_ref[0] * LOG2E

    @pl.when(step == 0)
    def _():
        for c in range(seq // 512):
            vt_ref[:, c * 512:(c + 1) * 512] = v_ref[0, c * 512:(c + 1) * 512, :].T
        for s in range(2):
            rows = slice(s * tk, (s + 1) * tk)
            r = lax.broadcasted_iota(jnp.int32, (tk, 2 * tq), 0) + s * tk
            c = lax.broadcasted_iota(jnp.int32, (tk, 2 * tq), 1)
            c = jnp.where(c >= tq, c - tq, c)
            rc = (r - c).astype(F32)
            bias_ref[0, rows, :] = slope * rc
            bias_ref[1, rows, :] = -(slope * rc)
            for w in range(nqp):
                bias_ref[2 + w, rows, :] = -(slope * jnp.abs(rc - float(w * tq)))

    lam = (jnp.exp(jnp.sum(lq1_ref[...] * lk1_ref[...], axis=-1, keepdims=True))
           - jnp.exp(jnp.sum(lq2_ref[...] * lk2_ref[...], axis=-1, keepdims=True))
           + lam_init)

    def one_tile(t):
        qi = step * ATT_QSUB + t
        tile = (pl.program_id(0) * N_DIFF_HEADS + pl.program_id(1)) * (seq // tq) + qi
        lo, hi = lo_ref[tile], hi_ref[tile]
        lt_ref, at_ref = l_ref.at[t], acc_ref.at[t]

        qt = q_ref[0, t * tq:(t + 1) * tq, :].T
        z = jnp.zeros((dh, tq), BF16)
        qbd = jnp.concatenate([jnp.concatenate([qt[:dh], z], axis=1),
                               jnp.concatenate([z, qt[dh:]], axis=1)], axis=0)

        lt_ref[...] = jnp.zeros(lt_ref.shape, F32)
        at_ref[...] = jnp.zeros(at_ref.shape, F32)

        kd = qi // nqp
        w = qi % nqp

        def mode(kp):
            delta = (kp * pair - qi * tq).astype(F32)
            idx = jnp.where(kp < kd, 0, jnp.where(kp > kd, 1, 2 + w))
            sgn = jnp.where(kp < kd, delta, jnp.where(kp > kd, -delta, 0.0))
            return idx, slope * sgn

        def scores(kp, idx, s):
            k0 = pl.multiple_of(kp * pair + s * tk, tk)
            kblk = k_ref[0, pl.ds(k0, tk), :]
            u = jnp.dot(kblk, qbd, preferred_element_type=F32)
            return u + bias_ref[idx, s * tk:(s + 1) * tk, :]

        def values(kp, s):
            k0 = pl.multiple_of(kp * pair + s * tk, tk)
            return vt_ref[:, pl.ds(k0, tk)]

        def pipelined(qk_phase, sm_phase, carry0):
            def body(kp, carry_a):
                idx, cst = mode(kp)
                carry_b = qk_phase(kp, idx, 1)
                sm_phase(kp, cst, 0, carry_a)
                kn = jnp.minimum(kp + 1, hi - 1)
                idx_n, _ = mode(kn)
                carry_a_next = qk_phase(kn, idx_n, 0)
                sm_phase(kp, cst, 1, carry_b)
                return carry_a_next

            n_double = (hi - lo) // 2
            carry_a = lax.fori_loop(0, n_double,
                                    lambda i, c: body(lo + 2 * i + 1, body(lo + 2 * i, c)), carry0)
            lax.fori_loop(lo + 2 * n_double, hi, body, carry_a)

        def fixed_reference():
            ub = ub_ref[t]

            def qk_phase(kp, idx, s):
                u_ref[s] = scores(kp, idx, s)
                return 0

            def sm_phase(kp, cst, s, _):
                p = jnp.exp2(u_ref[s] + (cst - ub))
                lt_ref[...] += jnp.sum(p, axis=0, keepdims=True)
                at_ref[...] += jnp.dot(values(kp, s), p.astype(BF16), preferred_element_type=F32)

            pipelined(qk_phase, sm_phase, 0)

        def running_max():
            m_ref[...] = jnp.full(m_ref.shape, -jnp.inf, F32)

            def qk_phase(kp, idx, s):
                v = scores(kp, idx, s)
                u_ref[s] = v
                return jnp.max(v, axis=0, keepdims=True)

            def sm_phase(kp, cst, s, mx):
                m_old = m_ref[...]
                m_new = jnp.maximum(m_old, mx + cst)
                alpha = jnp.exp2(m_old - m_new)
                p = jnp.exp2(u_ref[s] - (m_new - cst))
                lt_ref[...] = alpha * lt_ref[...] + jnp.sum(p, axis=0, keepdims=True)
                m_ref[...] = m_new
                at_ref[...] = alpha * at_ref[...] + jnp.dot(values(kp, s), p.astype(BF16),
                                                            preferred_element_type=F32)

            pipelined(qk_phase, sm_phase, jnp.max(u_ref[0], axis=0, keepdims=True))

        idx0, _ = mode(lo)
        u_ref[0] = scores(lo, idx0, 0)
        lax.cond(fast_ref[tile] == 1, fixed_reference, running_max)

        acc = at_ref[...]
        l = lt_ref[...]
        ot = acc[:, :tq] / l[:, :tq] - lam * (acc[:, tq:] / l[:, tq:])
        ms = jnp.mean(ot * ot, axis=0, keepdims=True)
        ot = ot * lax.rsqrt(ms + EPS) * sub_ref[...] * (1.0 - lam_init)
        o_ref[0, t * tq:(t + 1) * tq, :] = ot.T.astype(o_ref.dtype)

    for t in range(ATT_QSUB):
        one_tile(t)


def _diff_attention(qkv, slopes, lq1, lk1, lq2, lk2, subln_col, lam_init):
    b, s, _ = qkv.shape
    h, dv = N_DIFF_HEADS, DIFF_V_DIM
    tq, tk, qsub = ATT_TQ, ATT_TK, ATT_QSUB
    nq = s // tq
    assert s % ATT_PAIR == 0 and nq % qsub == 0
    nsteps = nq // qsub
    lo, hi, fast, ub = _attn_bounds(qkv, slopes)
    kern = functools.partial(_attn_kernel, lam_init=lam_init, seq=s)
    vec = pl.BlockSpec((1, DIFF_HEAD_DIM), lambda bi, hi, qi, *_: (0, 0))
    grid_spec = pltpu.PrefetchScalarGridSpec(
        num_scalar_prefetch=3,
        grid=(b, h, nsteps),
        in_specs=[
            pl.BlockSpec((1, qsub * tq, dv), lambda bi, hi, qi, *_: (bi, qi, hi)),
            pl.BlockSpec((1, s, dv), lambda bi, hi, qi, *_: (bi, 0, h + hi)),
            pl.BlockSpec((1, s, dv), lambda bi, hi, qi, *_: (bi, 0, 2 * h + hi)),
            pl.BlockSpec((qsub, 1, 1), lambda bi, hi, qi, *_: ((bi * h + hi) * nsteps + qi, 0, 0)),
            pl.BlockSpec((1, 1, 1), lambda bi, hi, qi, *_: (hi, 0, 0)),
            vec, vec, vec, vec,
            pl.BlockSpec((dv, 1), lambda bi, hi, qi, *_: (0, 0)),
        ],
        out_specs=pl.BlockSpec((1, qsub * tq, dv), lambda bi, hi, qi, *_: (bi, qi, hi)),
        scratch_shapes=[
            pltpu.VMEM((dv, s), BF16),
            pltpu.VMEM((2 + ATT_NQP, ATT_PAIR, 2 * tq), F32),
            pltpu.VMEM((2, tk, 2 * tq), F32),
            pltpu.VMEM((1, 2 * tq), F32),
            pltpu.VMEM((qsub, 1, 2 * tq), F32),
            pltpu.VMEM((qsub, dv, 2 * tq), F32),
        ],
    )
    return pl.pallas_call(
        kern,
        grid_spec=grid_spec,
        out_shape=jax.ShapeDtypeStruct((b, s, h * dv), BF16),
        compiler_params=_cparams(3),
        name="diff_attention",
    )(lo.reshape(-1), hi.reshape(-1), fast.reshape(-1),
      qkv, qkv, qkv, ub.reshape(-1, 1, 1), slopes, lq1, lk1, lq2, lk2, subln_col)


def _proj_res_kernel(x_ref, a_ref, w_ref, gate_ref, o_ref):
    y = jnp.dot(a_ref[0], w_ref[...], preferred_element_type=F32)
    o_ref[0] = x_ref[0] + gate_ref[0] * y


def _proj_residual(x, a, w_bf16, gate, tm):
    b, s, d = x.shape
    kdim = a.shape[-1]
    return pl.pallas_call(
        _proj_res_kernel,
        grid=(b, s // tm),
        in_specs=[
            pl.BlockSpec((1, tm, d), lambda i, j: (i, j, 0)),
            pl.BlockSpec((1, tm, kdim), lambda i, j: (i, j, 0)),
            pl.BlockSpec((kdim, d), lambda i, j: (0, 0)),
            pl.BlockSpec((1, 1, d), lambda i, j: (i, 0, 0)),
        ],
        out_specs=pl.BlockSpec((1, tm, d), lambda i, j: (i, j, 0)),
        out_shape=jax.ShapeDtypeStruct((b, s, d), F32),
        compiler_params=_cparams(2),
        name="proj_residual",
    )(x, a, w_bf16, gate)


def _routing_gates_t(lt):
    ng, ne = N_EXPERT_GROUPS, EXPERTS_PER_GROUP
    gl = [lt[g:g + 1, :] for g in range(ng)]
    gmax = functools.reduce(jnp.maximum, gl)
    den = functools.reduce(lambda a, b: a + b, [jnp.exp(v - gmax) for v in gl])
    g_w = 1.0 / den
    sel = []
    taken = jnp.zeros(gmax.shape, jnp.bool_)
    for g in range(ng):
        s = jnp.logical_and(gl[g] == gmax, jnp.logical_not(taken))
        taken = jnp.logical_or(taken, s)
        sel.append(s)
    ev = []
    for e in range(ne):
        v = jnp.zeros_like(gmax)
        for g in range(ng):
            row = ng + ne * g + e
            v = jnp.where(sel[g], lt[row:row + 1, :], v)
        ev.append(v)
    top1 = functools.reduce(jnp.maximum, ev)
    is1 = []
    taken = jnp.zeros(gmax.shape, jnp.bool_)
    for e in range(ne):
        s = jnp.logical_and(ev[e] == top1, jnp.logical_not(taken))
        taken = jnp.logical_or(taken, s)
        is1.append(s)
    rest = [jnp.where(is1[e], -jnp.inf, ev[e]) for e in range(ne)]
    top2 = functools.reduce(jnp.maximum, rest)
    is2 = []
    taken = jnp.zeros(gmax.shape, jnp.bool_)
    for e in range(ne):
        s = jnp.logical_and(jnp.logical_and(rest[e] == top2, jnp.logical_not(is1[e])),
                            jnp.logical_not(taken))
        taken = jnp.logical_or(taken, s)
        is2.append(s)
    ex = jnp.exp(top2 - top1)
    w1 = 1.0 / (1.0 + ex)
    w2 = ex / (1.0 + ex)
    ew = [jnp.where(is1[e], w1, jnp.where(is2[e], w2, 0.0)) for e in range(ne)]
    rows = [jnp.where(sel[g], g_w * ew[e], 0.0) for g in range(ng) for e in range(ne)]
    return jnp.concatenate(rows, axis=0), sel


MOE_TM = 1024
MOE_CH = 128
MOE_NB = 11
MOE_POS_LANE = 3 * N_EXPERTS


def _route_sort_kernel(x_ref, g_ref, sh_ref, sc_ref, wr_ref, br_ref, tri_ref,
                       hs_ref, gs_ref, meta_ref, gid_ref):
    ng, ch, nb = N_EXPERT_GROUPS, MOE_CH, MOE_NB
    tm = x_ref.shape[0]
    h32 = _rms_mod(x_ref[...], g_ref[...], sh_ref[0], sc_ref[0])
    lt = lax.dot_general(wr_ref[...], h32, (((1,), (1,)), ((), ())),
                         precision=lax.Precision.HIGHEST,
                         preferred_element_type=F32) + br_ref[...]
    gates_t, sel = _routing_gates_t(lt)
    oh = jnp.concatenate([m.astype(F32) for m in sel] + [jnp.zeros((16 - ng, tm), F32)], axis=0)
    cnt = jnp.dot(oh.astype(BF16), tri_ref[...], preferred_element_type=F32)
    rank = jnp.sum(oh * cnt, axis=0, keepdims=True)
    n = jnp.sum(oh, axis=1, keepdims=True)
    nch = jnp.floor((n + float(ch - 1)) * (1.0 / ch))
    first = [jnp.zeros((1, 1), F32)]
    for g in range(ng):
        first.append(first[-1] + nch[g:g + 1, :])
    seg = functools.reduce(lambda a, b: a + b,
                           [oh[g:g + 1, :] * (first[g] * float(ch)) for g in range(ng)])
    pos = seg + rank
    rows = lax.broadcasted_iota(jnp.int32, (nb * ch, tm), 0)
    perm = jnp.where(rows == pos.astype(jnp.int32), 1.0, 0.0).astype(BF16)
    hs_ref[0] = jnp.dot(perm, h32.astype(BF16), preferred_element_type=F32).astype(BF16)
    hi = gates_t.astype(BF16).astype(F32)
    r1 = gates_t - hi
    mid = r1.astype(BF16).astype(F32)
    lo = r1 - mid
    meta_t = jnp.concatenate([hi, mid, lo, pos, jnp.zeros((LANES - MOE_POS_LANE - 1, tm), F32)], axis=0)
    meta = meta_t.T
    meta_ref[...] = meta
    gs_ref[0] = jnp.dot(perm, meta.astype(BF16), preferred_element_type=F32)
    lane = lax.broadcasted_iota(jnp.int32, (1, LANES), 1).astype(F32)
    gid = functools.reduce(lambda a, b: a + b,
                           [jnp.where(lane >= first[g + 1], 1.0, 0.0) for g in range(ng)])
    gid_ref[0] = gid.astype(jnp.int32)


def _route_sort(x2, g, shift, scale, wr, br, tri, tiles_per_batch):
    t, d = x2.shape
    tm, rows = MOE_TM, MOE_NB * MOE_CH
    nt = t // tm
    bvec = lambda i: (i // tiles_per_batch, 0, 0)
    return pl.pallas_call(
        _route_sort_kernel,
        grid=(nt,),
        in_specs=[
            pl.BlockSpec((tm, d), lambda i: (i, 0)),
            pl.BlockSpec((1, d), lambda i: (0, 0)),
            pl.BlockSpec((1, 1, d), bvec),
            pl.BlockSpec((1, 1, d), bvec),
            pl.BlockSpec((LANES, d), lambda i: (0, 0)),
            pl.BlockSpec((LANES, 1), lambda i: (0, 0)),
            pl.BlockSpec((tm, tm), lambda i: (0, 0)),
        ],
        out_specs=[
            pl.BlockSpec((1, rows, d), lambda i: (i, 0, 0)),
            pl.BlockSpec((1, rows, LANES), lambda i: (i, 0, 0)),
            pl.BlockSpec((tm, LANES), lambda i: (i, 0)),
            pl.BlockSpec((1, 1, LANES), lambda i: (i, 0, 0)),
        ],
        out_shape=[
            jax.ShapeDtypeStruct((nt, rows, d), BF16),
            jax.ShapeDtypeStruct((nt, rows, LANES), F32),
            jax.ShapeDtypeStruct((t, LANES), F32),
            jax.ShapeDtypeStruct((nt, 1, LANES), jnp.int32),
        ],
        compiler_params=_cparams(1),
        name="moe_route_sort",
    )(x2, g, shift, scale, wr, br, tri)


def _experts_kernel(order_ref, grp_ref, hs_ref, gs_ref, wg_ref, wu_ref, wd_ref, ys_ref):
    ne = EXPERTS_PER_GROUP
    grp = grp_ref[pl.program_id(0)]

    @pl.when(grp < N_EXPERT_GROUPS)
    def _():
        h = hs_ref[0]
        gs = gs_ref[0]
        lane = lax.broadcasted_iota(jnp.int32, gs.shape, 1)
        hid = []
        for e in range(ne):
            idx = grp * ne + e
            pick = jnp.logical_or(jnp.logical_or(lane == idx, lane == idx + N_EXPERTS),
                                  lane == idx + 2 * N_EXPERTS)
            gate = jnp.sum(jnp.where(pick, gs, 0.0), axis=1, keepdims=True)
            a = jnp.dot(h, wg_ref[0, e], preferred_element_type=F32)
            u = jnp.dot(h, wu_ref[0, e], preferred_element_type=F32)
            hid.append(((a / (1.0 + jnp.exp(-a))) * u * gate).astype(BF16))
        hid = jnp.concatenate(hid, axis=1)
        ys_ref[0] = jnp.dot(hid, wd_ref[0], preferred_element_type=F32).astype(ys_ref.dtype)

    @pl.when(grp >= N_EXPERT_GROUPS)
    def _():
        ys_ref[0] = jnp.zeros(ys_ref.shape[1:], ys_ref.dtype)


def _experts(order, grp, hs, gs, wg4, wu4, wd4):
    nchunks, ch, d = hs.shape
    ne, f = wg4.shape[1], wg4.shape[-1]
    wsel = lambda s, order, grp: (jnp.minimum(grp[s], N_EXPERT_GROUPS - 1), 0, 0, 0)
    grid_spec = pltpu.PrefetchScalarGridSpec(
        num_scalar_prefetch=2,
        grid=(nchunks,),
        in_specs=[
            pl.BlockSpec((1, ch, d), lambda s, order, grp: (order[s], 0, 0)),
            pl.BlockSpec((1, ch, LANES), lambda s, order, grp: (order[s], 0, 0)),
            pl.BlockSpec((1, ne, d, f), wsel),
            pl.BlockSpec((1, ne, d, f), wsel),
            pl.BlockSpec((1, ne * f, d),
                         lambda s, order, grp: (jnp.minimum(grp[s], N_EXPERT_GROUPS - 1), 0, 0)),
        ],
        out_specs=pl.BlockSpec((1, ch, d), lambda s, order, grp: (order[s], 0, 0)),
    )
    return pl.pallas_call(
        _experts_kernel,
        grid_spec=grid_spec,
        out_shape=jax.ShapeDtypeStruct((nchunks, ch, d), BF16),
        compiler_params=_cparams(1),
        name="moe_experts",
    )(order, grp, hs, gs, wg4, wu4, wd4)


def _unsort_kernel(x_ref, ys_ref, meta_ref, gate_ref, gf_ref, o_ref, *, final_norm):
    rows = ys_ref.shape[1]
    tm = x_ref.shape[0]
    pos = meta_ref[:, MOE_POS_LANE:MOE_POS_LANE + 1].astype(jnp.int32)
    cols = lax.broadcasted_iota(jnp.int32, (tm, rows), 1)
    perm_t = jnp.where(cols == pos, 1.0, 0.0).astype(BF16)
    y = jnp.dot(perm_t, ys_ref[0], preferred_element_type=F32)
    y = x_ref[...] + gate_ref[0] * y
    if final_norm:
        y = y * lax.rsqrt(jnp.mean(y * y, axis=-1, keepdims=True) + EPS) * gf_ref[...]
    o_ref[...] = y


def _unsort_residual(x2, ys, meta, gate, gfinal, final_norm, tiles_per_batch):
    t, d = x2.shape
    tm = MOE_TM
    nt, rows, _ = ys.shape
    kern = functools.partial(_unsort_kernel, final_norm=final_norm)
    return pl.pallas_call(
        kern,
        grid=(nt,),
        in_specs=[
            pl.BlockSpec((tm, d), lambda i: (i, 0)),
            pl.BlockSpec((1, rows, d), lambda i: (i, 0, 0)),
            pl.BlockSpec((tm, LANES), lambda i: (i, 0)),
            pl.BlockSpec((1, 1, d), lambda i: (i // tiles_per_batch, 0, 0)),
            pl.BlockSpec((1, d), lambda i: (0, 0)),
        ],
        out_specs=pl.BlockSpec((tm, d), lambda i: (i, 0)),
        out_shape=jax.ShapeDtypeStruct((t, d), F32),
        compiler_params=_cparams(1),
        name="moe_unsort_residual",
    )(x2, ys, meta, gate, gfinal)


def moe_layer(x, g, sh2, sc2, g2, w_rg, b_rg, w_re, b_re, w_gate, w_up, w_down, layer, gfinal, final_norm):
    b, s, d = x.shape
    ng, ne = N_EXPERT_GROUPS, EXPERTS_PER_GROUP
    f = w_gate.shape[-1]
    assert s % MOE_TM == 0
    nr = ng + N_EXPERTS
    wr = jnp.concatenate([w_rg.T, w_re.reshape(d, N_EXPERTS).T,
                          jnp.zeros((LANES - nr, d), F32)], axis=0)
    br = jnp.concatenate([b_rg, b_re.reshape(N_EXPERTS),
                          jnp.zeros((LANES - nr,), F32)]).reshape(LANES, 1)
    ti = np.arange(MOE_TM)
    tri = jnp.asarray(ti[:, None] < ti[None, :], F32).astype(BF16)
    x2 = x.reshape(b * s, d)
    tpb = s // MOE_TM
    hs, gs, meta, gid = _route_sort(x2, g.reshape(1, d), sh2, sc2, wr, br, tri, tpb)
    nt = hs.shape[0]
    gid_flat = gid[:, 0, :MOE_NB].reshape(-1)
    order = jnp.argsort(gid_flat, stable=True).astype(jnp.int32)
    grp = gid_flat[order]
    ys = _experts(order, grp,
                  hs.reshape(nt * MOE_NB, MOE_CH, d), gs.reshape(nt * MOE_NB, MOE_CH, LANES),
                  _to_bf16(w_gate, layer).reshape(ng, ne, d, f),
                  _to_bf16(w_up, layer).reshape(ng, ne, d, f),
                  _to_bf16(w_down, layer).reshape(ng, ne * f, d))
    out = _unsort_residual(x2, ys.reshape(nt, MOE_NB * MOE_CH, d), meta, g2,
                           gfinal.reshape(1, d), final_norm, tpb)
    return out.reshape(b, s, d)


def _fft1_kernel(u_ref, w_ref, twr_ref, twi_ref, o_ref, *, n_sub, width):
    n1 = FFT_N1
    y = jnp.dot(w_ref[...], u_ref[0], preferred_element_type=F32)
    for j in range(n_sub):
        cols = slice(j * width, (j + 1) * width)
        yr, yi = y[:n1, cols], y[n1:, cols]
        twr, twi = twr_ref[j], twi_ref[j]
        o_ref[0, :n1, cols] = (yr * twr - yi * twi).astype(o_ref.dtype)
        o_ref[0, n1:, cols] = (yr * twi + yi * twr).astype(o_ref.dtype)


def _fft_stage1(u3, w1, twr, twi, width):
    b, n1, cols = u3.shape
    n_sub = 8
    tn = n_sub * width
    kern = functools.partial(_fft1_kernel, n_sub=n_sub, width=width)
    return pl.pallas_call(
        kern,
        grid=(b, cols // tn),
        in_specs=[
            pl.BlockSpec((1, n1, tn), lambda i, j: (i, 0, j)),
            pl.BlockSpec((2 * n1, n1), lambda i, j: (0, 0)),
            pl.BlockSpec((n_sub, n1, 1), lambda i, j: (j, 0, 0)),
            pl.BlockSpec((n_sub, n1, 1), lambda i, j: (j, 0, 0)),
        ],
        out_specs=pl.BlockSpec((1, 2 * n1, tn), lambda i, j: (i, 0, j)),
        out_shape=jax.ShapeDtypeStruct((b, 2 * n1, cols), BF16),
        compiler_params=_cparams(2),
        name="fft_stage1",
    )(u3, w1, twr, twi)


def _fft2_mix_kernel(y_ref, g_ref, x_ref, cc_ref, sc_ref, w_ref, gate_ref, o_ref):
    kp, n2 = FFT_K1_PER_STEP, FFT_N2
    width = y_ref.shape[-1]
    gd = cc_ref.shape[0]
    half = kp * n2
    y = y_ref[0].reshape(2 * half, width)
    r = jnp.dot(g_ref[...], y, preferred_element_type=F32)
    ar, ai = r[:half].astype(BF16), r[half:].astype(BF16)
    parts = []
    for g in range(N_FOURIER_GROUPS):
        cols = slice(g * gd, (g + 1) * gd)
        parts.append(jnp.dot(ar[:, cols], cc_ref[...], preferred_element_type=F32)
                     + jnp.dot(ai[:, cols], sc_ref[...], preferred_element_type=F32))
    f = jnp.concatenate(parts, axis=1).astype(BF16)
    yv = jnp.dot(f, w_ref[...], preferred_element_type=F32)
    out = x_ref[0].reshape(half, width) + gate_ref[0] * yv
    o_ref[0] = out.reshape(n2, kp, width)


def _fft2_mix(y5, gmat, x4, cc, sc, w_bf16, gate):
    b, _, n1, n2, width = y5.shape
    kp = FFT_K1_PER_STEP
    gd = cc.shape[0]
    return pl.pallas_call(
        _fft2_mix_kernel,
        grid=(b, n1 // kp),
        in_specs=[
            pl.BlockSpec((1, 2, kp, n2, width), lambda i, j: (i, 0, j, 0, 0)),
            pl.BlockSpec((2 * kp * n2, 2 * kp * n2), lambda i, j: (0, 0)),
            pl.BlockSpec((1, n2, kp, width), lambda i, j: (i, 0, j, 0)),
            pl.BlockSpec((gd, gd), lambda i, j: (0, 0)),
            pl.BlockSpec((gd, gd), lambda i, j: (0, 0)),
            pl.BlockSpec((width, width), lambda i, j: (0, 0)),
            pl.BlockSpec((1, 1, width), lambda i, j: (i, 0, 0)),
        ],
        out_specs=pl.BlockSpec((1, n2, kp, width), lambda i, j: (i, 0, j, 0)),
        out_shape=jax.ShapeDtypeStruct((b, n2, n1, width), F32),
        compiler_params=_cparams(2),
        name="fft_stage2_mix",
    )(y5, gmat, x4, cc, sc, w_bf16, gate)


def _dft_tables(seq, gd):
    n1, n2, kp = FFT_N1, FFT_N2, FFT_K1_PER_STEP

    def cs(num, den):
        ang = (2.0 * math.pi / den) * (num % den).astype(np.float64)
        return np.cos(ang), np.sin(ang)

    i1 = np.arange(n1)
    c1, s1 = cs(np.outer(i1, i1), n1)
    w1 = np.concatenate([c1, -s1], axis=0)
    ct, st = cs(np.outer(np.arange(n2), i1), seq)
    twr, twi = ct[:, :, None], -st[:, :, None]
    i2 = np.arange(n2)
    c2, s2 = cs(np.outer(i2, i2), n2)
    m3 = np.block([[c2, s2], [-s2, c2]])
    m3 = m3.reshape(2, n2, 2, n2)
    gmat = np.zeros((2, n2, kp, 2, kp, n2))
    for j in range(kp):
        gmat[:, :, j, :, j, :] = m3
    gmat = gmat.reshape(2 * n2 * kp, 2 * kp * n2)
    ic = np.arange(gd)
    cc, sc = cs(np.outer(ic, ic), gd)
    norm = 1.0 / math.sqrt(seq * gd)
    return tuple(jnp.asarray(t, F32) for t in (w1, twr, twi, gmat, cc * norm, sc * norm))


def kernel(x, c, ada_w, ada_b, norm_mix, norm_ffn, attn_w_in, attn_lam_q1, attn_lam_k1, attn_lam_q2, attn_lam_k2, attn_subln, attn_w_out, fourier_w_in, fourier_w_out, router_group_w, router_group_b, router_expert_w, router_expert_b, expert_w_gate, expert_w_up, expert_w_down, norm_final):
    b, s, d = x.shape
    depth = ada_w.shape[0]
    assert s == FFT_N1 * FFT_N2 and depth == 2

    c8 = jnp.zeros((8, d), F32).at[:b].set(c)
    mod = _ada(c8, ada_w, ada_b.reshape(depth, 1, N_ADA * d))[:, :b]
    mod = mod.reshape(depth, b, N_ADA, 1, d)

    slopes = (2.0 ** (-8.0 * jnp.arange(1, N_DIFF_HEADS + 1, dtype=F32) / N_DIFF_HEADS)
              ).reshape(N_DIFF_HEADS, 1, 1)
    gd = d // N_FOURIER_GROUPS
    w1, twr, twi, gmat, cc, sc = _dft_tables(s, gd)
    w1, gmat, cc, sc = (t.astype(BF16) for t in (w1, gmat, cc, sc))

    for i in range(depth):
        sh1, sc1, g1, sh2, sc2, g2 = [mod[i, :, k] for k in range(N_ADA)]
        j = i // 2
        if i % 2 == 0:
            qscale = jnp.concatenate([jnp.full((d,), LOG2E * DIFF_HEAD_DIM ** -0.5, F32),
                                      jnp.ones((2 * d,), F32)])
            w_in = (attn_w_in[j] * qscale).astype(BF16)
            qkv = _norm_mod_matmul(x, norm_mix[i].reshape(1, d), sh1, sc1, w_in, tm=512)
            lam_init = 0.8 - 0.6 * math.exp(-0.3 * i)
            o = _diff_attention(qkv, slopes,
                                attn_lam_q1[j].reshape(1, -1), attn_lam_k1[j].reshape(1, -1),
                                attn_lam_q2[j].reshape(1, -1), attn_lam_k2[j].reshape(1, -1),
                                attn_subln[j].reshape(-1, 1), lam_init)
            x = _proj_residual(x, o, attn_w_out[j].astype(BF16), g1, tm=512)
        else:
            u = _norm_mod_matmul(x, norm_mix[i].reshape(1, d), sh1, sc1,
                                 fourier_w_in[j].astype(BF16), tm=512)
            y = _fft_stage1(u.reshape(b, FFT_N1, FFT_N2 * d), w1, twr, twi, d)
            x = _fft2_mix(y.reshape(b, 2, FFT_N1, FFT_N2, d), gmat, x.reshape(b, FFT_N2, FFT_N1, d),
                          cc, sc, fourier_w_out[j].astype(BF16), g1).reshape(b, s, d)

        x = moe_layer(x, norm_ffn[i], sh2, sc2, g2, router_group_w[i], router_group_b[i],
                      router_expert_w[i], router_expert_b[i], expert_w_gate, expert_w_up,
                      expert_w_down, i, norm_final, i == depth - 1)
    return x
```

```python
import functools
import math

import numpy as np
import jax
import jax.numpy as jnp
from jax import lax
from jax.experimental import pallas as pl
from jax.experimental.pallas import tpu as pltpu

F32 = jnp.float32
BF16 = jnp.bfloat16

EPS = 1e-6
N_DIFF_HEADS = 8
DIFF_HEAD_DIM = 64
DIFF_V_DIM = 128
N_FOURIER_GROUPS = 4
N_EXPERT_GROUPS = 4
EXPERTS_PER_GROUP = 4
N_EXPERTS = 16
N_ADA = 6

LANES = 128
VMEM_LIMIT_BYTES = 56 * 1024 * 1024

FFT_N1 = 128
FFT_N2 = 64
FFT_K1_PER_STEP = 8


def _cparams(n_axes):
    return pltpu.CompilerParams(
        dimension_semantics=("arbitrary",) * n_axes,
        vmem_limit_bytes=VMEM_LIMIT_BYTES,
    )


def _rms_mod(x, g, shift, scale):
    y = x * lax.rsqrt(jnp.mean(x * x, axis=-1, keepdims=True) + EPS)
    return (y * g) * (1.0 + scale) + shift


CAST_BLOCK_BYTES = 4 * 1024 * 1024


def _cast_kernel(w_ref, o_ref):
    o_ref[...] = w_ref[...].astype(o_ref.dtype)


def _to_bf16(w_layers, layer):
    cols = w_layers.shape[-1]
    w3 = w_layers.reshape(w_layers.shape[0], -1, cols)
    rows = w3.shape[1]
    tr = min(CAST_BLOCK_BYTES // (4 * cols), rows)
    assert rows % tr == 0
    out = pl.pallas_call(
        _cast_kernel,
        grid=(rows // tr,),
        in_specs=[pl.BlockSpec((1, tr, cols), lambda i: (layer, i, 0))],
        out_specs=pl.BlockSpec((1, tr, cols), lambda i: (0, i, 0)),
        out_shape=jax.ShapeDtypeStruct((1, rows, cols), BF16),
        compiler_params=_cparams(1),
        name="cast_bf16",
    )(w3)
    return out.reshape(w_layers.shape[1:])


def _ada_kernel(c_ref, w_ref, b_ref, o_ref):
    c = c_ref[...]
    cond = c / (1.0 + jnp.exp(-c))
    o_ref[0] = jnp.dot(cond, w_ref[0], precision=lax.Precision.HIGHEST,
                       preferred_element_type=F32) + b_ref[0]


def _ada(c8, ada_w, ada_b3):
    depth, d, n = ada_w.shape
    tn = 1536
    return pl.pallas_call(
        _ada_kernel,
        grid=(depth, n // tn),
        in_specs=[
            pl.BlockSpec((8, d), lambda i, j: (0, 0)),
            pl.BlockSpec((1, d, tn), lambda i, j: (i, 0, j)),
            pl.BlockSpec((1, 1, tn), lambda i, j: (i, 0, j)),
        ],
        out_specs=pl.BlockSpec((1, 8, tn), lambda i, j: (i, 0, j)),
        out_shape=jax.ShapeDtypeStruct((depth, 8, n), F32),
        compiler_params=_cparams(2),
        name="ada_mod",
    )(c8, ada_w, ada_b3)


def _nmm_kernel(x_ref, g_ref, sh_ref, sc_ref, w_ref, o_ref):
    h = _rms_mod(x_ref[0], g_ref[...], sh_ref[0], sc_ref[0])
    o_ref[0] = jnp.dot(h.astype(BF16), w_ref[...],
                       preferred_element_type=F32).astype(o_ref.dtype)


def _norm_mod_matmul(x, g, shift, scale, w_bf16, tm):
    b, s, d = x.shape
    n = w_bf16.shape[1]
    return pl.pallas_call(
        _nmm_kernel,
        grid=(b, s // tm),
        in_specs=[
            pl.BlockSpec((1, tm, d), lambda i, j: (i, j, 0)),
            pl.BlockSpec((1, d), lambda i, j: (0, 0)),
            pl.BlockSpec((1, 1, d), lambda i, j: (i, 0, 0)),
            pl.BlockSpec((1, 1, d), lambda i, j: (i, 0, 0)),
            pl.BlockSpec((d, n), lambda i, j: (0, 0)),
        ],
        out_specs=pl.BlockSpec((1, tm, n), lambda i, j: (i, j, 0)),
        out_shape=jax.ShapeDtypeStruct((b, s, n), BF16),
        compiler_params=_cparams(2),
        name="norm_mod_matmul",
    )(x, g, shift, scale, w_bf16)


ATT_TQ = 256
ATT_TK = 512
ATT_PAIR = 2 * ATT_TK
ATT_NQP = ATT_PAIR // ATT_TQ
ATT_QSUB = 4
ATT_TRIPS_FAST = (4, 2)
ATT_TRIPS_SAFE = (2,)
LOG2E = 1.4426950408889634
ATT_SKIP_BITS = 40.0
ATT_FAST_MAX_U = 36.0


def _attn_bounds_kernel(q_ref, k_ref, slope_ref, lo_ref, hi_ref, fast_ref, ub_ref, *, seq):
    tq, pair = ATT_TQ, ATT_PAIR
    nq = seq // tq
    dd = 2 * DIFF_HEAD_DIM
    r = lax.broadcasted_iota(jnp.int32, (dd, dd), 0)
    c = lax.broadcasted_iota(jnp.int32, (dd, dd), 1)
    halves = jnp.where(c == r // DIFF_HEAD_DIM, 1.0, 0.0).astype(BF16)

    def half_norm2(ref):
        x = ref[0].astype(F32)
        return jnp.dot((x * x).astype(BF16), halves, preferred_element_type=F32)

    qn2 = jnp.max(jnp.max(half_norm2(q_ref).reshape(nq, tq, dd), axis=1), axis=1, keepdims=True)
    kn2 = jnp.max(jnp.max(half_norm2(k_ref), axis=0, keepdims=True), axis=1, keepdims=True)
    ub = jnp.sqrt(qn2 * kn2) * (1.0 + 2.0 ** -6)
    slope = slope_ref[0] * LOG2E
    dist = (2.0 * ub + (ATT_SKIP_BITS + math.log2(seq))) / slope
    dist = jnp.ceil(jnp.minimum(dist, float(seq))).astype(jnp.int32)
    q0 = lax.broadcasted_iota(jnp.int32, (nq, 1), 0) * tq
    jmin = jnp.maximum(q0 - dist + 1, 0)
    jmax = jnp.minimum(q0 + (tq - 1) + dist - 1, seq - 1)
    shift = jnp.full((nq, 1), int(math.log2(pair)), jnp.int32)
    lo_ref[0, 0] = lax.shift_right_logical(jmin, shift)
    hi_ref[0, 0] = lax.shift_right_logical(jmax, shift) + 1
    fast_ref[0, 0] = (ub <= ATT_FAST_MAX_U).astype(jnp.int32)
    ub_ref[0, 0] = ub


def _attn_bounds(qkv, slopes):
    b, s, _ = qkv.shape
    h, dv = N_DIFF_HEADS, DIFF_V_DIM
    nq = s // ATT_TQ
    kern = functools.partial(_attn_bounds_kernel, seq=s)
    oblk = pl.BlockSpec((1, 1, nq, 1), lambda bi, hi: (bi, hi, 0, 0))
    ishape = jax.ShapeDtypeStruct((b, h, nq, 1), jnp.int32)
    return pl.pallas_call(
        kern,
        grid=(b, h),
        in_specs=[
            pl.BlockSpec((1, s, dv), lambda bi, hi: (bi, 0, hi)),
            pl.BlockSpec((1, s, dv), lambda bi, hi: (bi, 0, h + hi)),
            pl.BlockSpec((1, 1, 1), lambda bi, hi: (hi, 0, 0)),
        ],
        out_specs=[oblk, oblk, oblk, oblk],
        out_shape=[ishape, ishape, ishape, jax.ShapeDtypeStruct((b, h, nq, 1), F32)],
        compiler_params=_cparams(2),
        name="attn_bounds",
    )(qkv, qkv, slopes)


def _attn_kernel(lo_ref, hi_ref, fast_ref, q_ref, k_ref, v_ref, ub_ref, slope_ref,
                 lq1_ref, lk1_ref, lq2_ref, lk2_ref, sub_ref,
                 o_ref, vt_ref, bias_ref, u_ref, m_ref, l_ref, acc_ref, *, lam_init, seq):
    tq, tk, pair, nqp = ATT_TQ, ATT_TK, ATT_PAIR, ATT_NQP
    dh = DIFF_HEAD_DIM
    step = pl.program_id(2)
    slope = slope_ref[0] * LOG2E

    @pl.when(step == 0)
    def _():
        for c in range(seq // 512):
            vt_ref[:, c * 512:(c + 1) * 512] = v_ref[0, c * 512:(c + 1) * 512, :].T
        for s in range(2):
            rows = slice(s * tk, (s + 1) * tk)
            r = lax.broadcasted_iota(jnp.int32, (tk, 2 * tq), 0) + s * tk
            c = lax.broadcasted_iota(jnp.int32, (tk, 2 * tq), 1)
            c = jnp.where(c >= tq, c - tq, c)
            rc = (r - c).astype(F32)
            bias_ref[0, rows, :] = slope * rc
            bias_ref[1, rows, :] = -(slope * rc)
            for w in range(nqp):
                bias_ref[2 + w, rows, :] = -(slope * jnp.abs(rc - float(w * tq)))

    lam = (jnp.exp(jnp.sum(lq1_ref[...] * lk1_ref[...], axis=-1, keepdims=True))
           - jnp.exp(jnp.sum(lq2_ref[...] * lk2_ref[...], axis=-1, keepdims=True))
           + lam_init)

    def one_tile(t):
        qi = step * ATT_QSUB + t
        tile = (pl.program_id(0) * N_DIFF_HEADS + pl.program_id(1)) * (seq // tq) + qi
        lo, hi = lo_ref[tile], hi_ref[tile]
        lt_ref, at_ref = l_ref.at[t], acc_ref.at[t]

        qt = q_ref[0, t * tq:(t + 1) * tq, :].T
        z = jnp.zeros((dh, tq), BF16)
        qbd = jnp.concatenate([jnp.concatenate([qt[:dh], z], axis=1),
                               jnp.concatenate([z, qt[dh:]], axis=1)], axis=0)

        lt_ref[...] = jnp.zeros(lt_ref.shape, F32)
        at_ref[...] = jnp.zeros(at_ref.shape, F32)

        kd = qi // nqp
        w = qi % nqp

        def mode(kp):
            delta = (kp * pair - qi * tq).astype(F32)
            idx = jnp.where(kp < kd, 0, jnp.where(kp > kd, 1, 2 + w))
            sgn = jnp.where(kp < kd, delta, jnp.where(kp > kd, -delta, 0.0))
            return idx, slope * sgn

        def scores(kp, idx, s):
            k0 = pl.multiple_of(kp * pair + s * tk, tk)
            kblk = k_ref[0, pl.ds(k0, tk), :]
            u = jnp.dot(kblk, qbd, preferred_element_type=F32)
            return u + bias_ref[idx, s * tk:(s + 1) * tk, :]

        def values(kp, s):
            k0 = pl.multiple_of(kp * pair + s * tk, tk)
            return vt_ref[:, pl.ds(k0, tk)]

        def pipelined(qk_phase, sm_phase, carry0, trips):
            def body(kp, carry_a):
                idx, cst = mode(kp)
                carry_b = qk_phase(kp, idx, 1)
                sm_phase(kp, cst, 0, carry_a)
                kn = jnp.minimum(kp + 1, hi - 1)
                idx_n, _ = mode(kn)
                carry_a_next = qk_phase(kn, idx_n, 0)
                sm_phase(kp, cst, 1, carry_b)
                return carry_a_next

            start, carry = lo, carry0
            for npt in trips:
                ntrip = (hi - start) // npt

                def trip(i, c, start=start, npt=npt):
                    for j in range(npt):
                        c = body(start + npt * i + j, c)
                    return c

                carry = lax.fori_loop(0, ntrip, trip, carry)
                start = start + npt * ntrip
            lax.fori_loop(start, hi, body, carry)

        def fixed_reference():
            ub = ub_ref[t]

            def qk_phase(kp, idx, s):
                u_ref[s] = scores(kp, idx, s)
                return 0

            def sm_phase(kp, cst, s, _):
                p = jnp.exp2(u_ref[s] + (cst - ub))
                lt_ref[...] += jnp.sum(p, axis=0, keepdims=True)
                at_ref[...] += jnp.dot(values(kp, s), p.astype(BF16), preferred_element_type=F32)

            pipelined(qk_phase, sm_phase, 0, ATT_TRIPS_FAST)

        def running_max():
            m_ref[...] = jnp.full(m_ref.shape, -jnp.inf, F32)

            def qk_phase(kp, idx, s):
                v = scores(kp, idx, s)
                u_ref[s] = v
                return jnp.max(v, axis=0, keepdims=True)

            def sm_phase(kp, cst, s, mx):
                m_old = m_ref[...]
                m_new = jnp.maximum(m_old, mx + cst)
                alpha = jnp.exp2(m_old - m_new)
                p = jnp.exp2(u_ref[s] - (m_new - cst))
                lt_ref[...] = alpha * lt_ref[...] + jnp.sum(p, axis=0, keepdims=True)
                m_ref[...] = m_new
                at_ref[...] = alpha * at_ref[...] + jnp.dot(values(kp, s), p.astype(BF16),
                                                            preferred_element_type=F32)

            pipelined(qk_phase, sm_phase, jnp.max(u_ref[0], axis=0, keepdims=True), ATT_TRIPS_SAFE)

        idx0, _ = mode(lo)
        u_ref[0] = scores(lo, idx0, 0)
        lax.cond(fast_ref[tile] == 1, fixed_reference, running_max)

        acc = at_ref[...]
        l = lt_ref[...]
        ot = acc[:, :tq] / l[:, :tq] - lam * (acc[:, tq:] / l[:, tq:])
        ms = jnp.mean(ot * ot, axis=0, keepdims=True)
        ot = ot * lax.rsqrt(ms + EPS) * sub_ref[...] * (1.0 - lam_init)
        o_ref[0, t * tq:(t + 1) * tq, :] = ot.T.astype(o_ref.dtype)

    for t in range(ATT_QSUB):
        one_tile(t)


def _diff_attention(qkv, slopes, lq1, lk1, lq2, lk2, subln_col, lam_init):
    b, s, _ = qkv.shape
    h, dv = N_DIFF_HEADS, DIFF_V_DIM
    tq, tk, qsub = ATT_TQ, ATT_TK, ATT_QSUB
    nq = s // tq
    assert s % ATT_PAIR == 0 and nq % qsub == 0
    nsteps = nq // qsub
    lo, hi, fast, ub = _attn_bounds(qkv, slopes)
    kern = functools.partial(_attn_kernel, lam_init=lam_init, seq=s)
    vec = pl.BlockSpec((1, DIFF_HEAD_DIM), lambda bi, hi, qi, *_: (0, 0))
    grid_spec = pltpu.PrefetchScalarGridSpec(
        num_scalar_prefetch=3,
        grid=(b, h, nsteps),
        in_specs=[
            pl.BlockSpec((1, qsub * tq, dv), lambda bi, hi, qi, *_: (bi, qi, hi)),
            pl.BlockSpec((1, s, dv), lambda bi, hi, qi, *_: (bi, 0, h + hi)),
            pl.BlockSpec((1, s, dv), lambda bi, hi, qi, *_: (bi, 0, 2 * h + hi)),
            pl.BlockSpec((qsub, 1, 1), lambda bi, hi, qi, *_: ((bi * h + hi) * nsteps + qi, 0, 0)),
            pl.BlockSpec((1, 1, 1), lambda bi, hi, qi, *_: (hi, 0, 0)),
            vec, vec, vec, vec,
            pl.BlockSpec((dv, 1), lambda bi, hi, qi, *_: (0, 0)),
        ],
        out_specs=pl.BlockSpec((1, qsub * tq, dv), lambda bi, hi, qi, *_: (bi, qi, hi)),
        scratch_shapes=[
            pltpu.VMEM((dv, s), BF16),
            pltpu.VMEM((2 + ATT_NQP, ATT_PAIR, 2 * tq), F32),
            pltpu.VMEM((2, tk, 2 * tq), F32),
            pltpu.VMEM((1, 2 * tq), F32),
            pltpu.VMEM((qsub, 1, 2 * tq), F32),
            pltpu.VMEM((qsub, dv, 2 * tq), F32),
        ],
    )
    return pl.pallas_call(
        kern,
        grid_spec=grid_spec,
        out_shape=jax.ShapeDtypeStruct((b, s, h * dv), BF16),
        compiler_params=_cparams(3),
        name="diff_attention",
    )(lo.reshape(-1), hi.reshape(-1), fast.reshape(-1),
      qkv, qkv, qkv, ub.reshape(-1, 1, 1), slopes, lq1, lk1, lq2, lk2, subln_col)


def _proj_res_kernel(x_ref, a_ref, w_ref, gate_ref, o_ref):
    y = jnp.dot(a_ref[0], w_ref[...], preferred_element_type=F32)
    o_ref[0] = x_ref[0] + gate_ref[0] * y


def _proj_residual(x, a, w_bf16, gate, tm):
    b, s, d = x.shape
    kdim = a.shape[-1]
    return pl.pallas_call(
        _proj_res_kernel,
        grid=(b, s // tm),
        in_specs=[
            pl.BlockSpec((1, tm, d), lambda i, j: (i, j, 0)),
            pl.BlockSpec((1, tm, kdim), lambda i, j: (i, j, 0)),
            pl.BlockSpec((kdim, d), lambda i, j: (0, 0)),
            pl.BlockSpec((1, 1, d), lambda i, j: (i, 0, 0)),
        ],
        out_specs=pl.BlockSpec((1, tm, d), lambda i, j: (i, j, 0)),
        out_shape=jax.ShapeDtypeStruct((b, s, d), F32),
        compiler_params=_cparams(2),
        name="proj_residual",
    )(x, a, w_bf16, gate)


def _routing_gates_t(lt):
    ng, ne = N_EXPERT_GROUPS, EXPERTS_PER_GROUP
    gl = [lt[g:g + 1, :] for g in range(ng)]
    gmax = functools.reduce(jnp.maximum, gl)
    den = functools.reduce(lambda a, b: a + b, [jnp.exp(v - gmax) for v in gl])
    g_w = 1.0 / den
    sel = []
    taken = jnp.zeros(gmax.shape, jnp.bool_)
    for g in range(ng):
        s = jnp.logical_and(gl[g] == gmax, jnp.logical_not(taken))
        taken = jnp.logical_or(taken, s)
        sel.append(s)
    ev = []
    for e in range(ne):
        v = jnp.zeros_like(gmax)
        for g in range(ng):
            row = ng + ne * g + e
            v = jnp.where(sel[g], lt[row:row + 1, :], v)
        ev.append(v)
    top1 = functools.reduce(jnp.maximum, ev)
    is1 = []
    taken = jnp.zeros(gmax.shape, jnp.bool_)
    for e in range(ne):
        s = jnp.logical_and(ev[e] == top1, jnp.logical_not(taken))
        taken = jnp.logical_or(taken, s)
        is1.append(s)
    rest = [jnp.where(is1[e], -jnp.inf, ev[e]) for e in range(ne)]
    top2 = functools.reduce(jnp.maximum, rest)
    is2 = []
    taken = jnp.zeros(gmax.shape, jnp.bool_)
    for e in range(ne):
        s = jnp.logical_and(jnp.logical_and(rest[e] == top2, jnp.logical_not(is1[e])),
                            jnp.logical_not(taken))
        taken = jnp.logical_or(taken, s)
        is2.append(s)
    ex = jnp.exp(top2 - top1)
    w1 = 1.0 / (1.0 + ex)
    w2 = ex / (1.0 + ex)
    ew = [jnp.where(is1[e], w1, jnp.where(is2[e], w2, 0.0)) for e in range(ne)]
    rows = [jnp.where(sel[g], g_w * ew[e], 0.0) for g in range(ng) for e in range(ne)]
    return jnp.concatenate(rows, axis=0), sel


MOE_TM = 1024
MOE_CH = 128
MOE_NB = 11
MOE_POS_LANE = 3 * N_EXPERTS


def _route_sort_kernel(x_ref, g_ref, sh_ref, sc_ref, wr_ref, br_ref, tri_ref,
                       hs_ref, gs_ref, meta_ref, gid_ref):
    ng, ch, nb = N_EXPERT_GROUPS, MOE_CH, MOE_NB
    tm = x_ref.shape[0]
    h32 = _rms_mod(x_ref[...], g_ref[...], sh_ref[0], sc_ref[0])
    hb = h32.astype(BF16)
    hl = (h32 - hb.astype(F32)).astype(BF16)
    wr = wr_ref[...]
    wb = wr.astype(BF16)
    wl = (wr - wb.astype(F32)).astype(BF16)
    nt = (((1,), (1,)), ((), ()))
    l2 = lax.dot_general(jnp.concatenate([wb, wl], axis=0), hb, nt, preferred_element_type=F32)
    lt = (l2[:LANES] + l2[LANES:] + lax.dot_general(wb, hl, nt, preferred_element_type=F32)
          + br_ref[...])
    gates_t, sel = _routing_gates_t(lt)
    oh = jnp.concatenate([m.astype(F32) for m in sel] + [jnp.zeros((16 - ng, tm), F32)], axis=0)
    cnt = jnp.dot(oh.astype(BF16), tri_ref[...], preferred_element_type=F32)
    rank = jnp.sum(oh * cnt, axis=0, keepdims=True)
    n = jnp.sum(oh, axis=1, keepdims=True)
    nch = jnp.floor((n + float(ch - 1)) * (1.0 / ch))
    first = [jnp.zeros((1, 1), F32)]
    for g in range(ng):
        first.append(first[-1] + nch[g:g + 1, :])
    seg = functools.reduce(lambda a, b: a + b,
                           [oh[g:g + 1, :] * (first[g] * float(ch)) for g in range(ng)])
    pos = seg + rank
    rows = lax.broadcasted_iota(jnp.int32, (nb * ch, tm), 0)
    perm = jnp.where(rows == pos.astype(jnp.int32), 1.0, 0.0).astype(BF16)
    hs_ref[0] = jnp.dot(perm, hb, preferred_element_type=F32).astype(BF16)
    hi = gates_t.astype(BF16).astype(F32)
    r1 = gates_t - hi
    mid = r1.astype(BF16).astype(F32)
    lo = r1 - mid
    meta_t = jnp.concatenate([hi, mid, lo, pos, jnp.zeros((LANES - MOE_POS_LANE - 1, tm), F32)], axis=0)
    meta = meta_t.T
    meta_ref[...] = meta
    gs_ref[0] = jnp.dot(perm, meta.astype(BF16), preferred_element_type=F32)
    lane = lax.broadcasted_iota(jnp.int32, (1, LANES), 1).astype(F32)
    gid = functools.reduce(lambda a, b: a + b,
                           [jnp.where(lane >= first[g + 1], 1.0, 0.0) for g in range(ng)])
    gid_ref[0] = gid.astype(jnp.int32)


def _route_sort(x2, g, shift, scale, wr, br, tri, tiles_per_batch):
    t, d = x2.shape
    tm, rows = MOE_TM, MOE_NB * MOE_CH
    nt = t // tm
    bvec = lambda i: (i // tiles_per_batch, 0, 0)
    return pl.pallas_call(
        _route_sort_kernel,
        grid=(nt,),
        in_specs=[
            pl.BlockSpec((tm, d), lambda i: (i, 0)),
            pl.BlockSpec((1, d), lambda i: (0, 0)),
            pl.BlockSpec((1, 1, d), bvec),
            pl.BlockSpec((1, 1, d), bvec),
            pl.BlockSpec((LANES, d), lambda i: (0, 0)),
            pl.BlockSpec((LANES, 1), lambda i: (0, 0)),
            pl.BlockSpec((tm, tm), lambda i: (0, 0)),
        ],
        out_specs=[
            pl.BlockSpec((1, rows, d), lambda i: (i, 0, 0)),
            pl.BlockSpec((1, rows, LANES), lambda i: (i, 0, 0)),
            pl.BlockSpec((tm, LANES), lambda i: (i, 0)),
            pl.BlockSpec((1, 1, LANES), lambda i: (i, 0, 0)),
        ],
        out_shape=[
            jax.ShapeDtypeStruct((nt, rows, d), BF16),
            jax.ShapeDtypeStruct((nt, rows, LANES), F32),
            jax.ShapeDtypeStruct((t, LANES), F32),
            jax.ShapeDtypeStruct((nt, 1, LANES), jnp.int32),
        ],
        compiler_params=_cparams(1),
        name="moe_route_sort",
    )(x2, g, shift, scale, wr, br, tri)


def _experts_kernel(order_ref, grp_ref, hs_ref, gs_ref, wg_ref, wu_ref, wd_ref, ys_ref):
    ne = EXPERTS_PER_GROUP
    grp = grp_ref[pl.program_id(0)]

    @pl.when(grp < N_EXPERT_GROUPS)
    def _():
        h = hs_ref[0]
        gs = gs_ref[0]
        lane = lax.broadcasted_iota(jnp.int32, gs.shape, 1)
        hid = []
        for e in range(ne):
            idx = grp * ne + e
            pick = jnp.logical_or(jnp.logical_or(lane == idx, lane == idx + N_EXPERTS),
                                  lane == idx + 2 * N_EXPERTS)
            gate = jnp.sum(jnp.where(pick, gs, 0.0), axis=1, keepdims=True)
            a = jnp.dot(h, wg_ref[0, e], preferred_element_type=F32)
            u = jnp.dot(h, wu_ref[0, e], preferred_element_type=F32)
            hid.append(((a / (1.0 + jnp.exp(-a))) * u * gate).astype(BF16))
        hid = jnp.concatenate(hid, axis=1)
        ys_ref[0] = jnp.dot(hid, wd_ref[0], preferred_element_type=F32).astype(ys_ref.dtype)

    @pl.when(grp >= N_EXPERT_GROUPS)
    def _():
        ys_ref[0] = jnp.zeros(ys_ref.shape[1:], ys_ref.dtype)


def _experts(order, grp, hs, gs, wg4, wu4, wd4):
    nchunks, ch, d = hs.shape
    ne, f = wg4.shape[1], wg4.shape[-1]
    wsel = lambda s, order, grp: (jnp.minimum(grp[s], N_EXPERT_GROUPS - 1), 0, 0, 0)
    grid_spec = pltpu.PrefetchScalarGridSpec(
        num_scalar_prefetch=2,
        grid=(nchunks,),
        in_specs=[
            pl.BlockSpec((1, ch, d), lambda s, order, grp: (order[s], 0, 0)),
            pl.BlockSpec((1, ch, LANES), lambda s, order, grp: (order[s], 0, 0)),
            pl.BlockSpec((1, ne, d, f), wsel),
            pl.BlockSpec((1, ne, d, f), wsel),
            pl.BlockSpec((1, ne * f, d),
                         lambda s, order, grp: (jnp.minimum(grp[s], N_EXPERT_GROUPS - 1), 0, 0)),
        ],
        out_specs=pl.BlockSpec((1, ch, d), lambda s, order, grp: (order[s], 0, 0)),
    )
    return pl.pallas_call(
        _experts_kernel,
        grid_spec=grid_spec,
        out_shape=jax.ShapeDtypeStruct((nchunks, ch, d), BF16),
        compiler_params=_cparams(1),
        name="moe_experts",
    )(order, grp, hs, gs, wg4, wu4, wd4)


def _unsort_kernel(x_ref, ys_ref, meta_ref, gate_ref, gf_ref, o_ref, *, final_norm):
    rows = ys_ref.shape[1]
    tm = x_ref.shape[0]
    pos = meta_ref[:, MOE_POS_LANE:MOE_POS_LANE + 1].astype(jnp.int32)
    cols = lax.broadcasted_iota(jnp.int32, (tm, rows), 1)
    perm_t = jnp.where(cols == pos, 1.0, 0.0).astype(BF16)
    y = jnp.dot(perm_t, ys_ref[0], preferred_element_type=F32)
    y = x_ref[...] + gate_ref[0] * y
    if final_norm:
        y = y * lax.rsqrt(jnp.mean(y * y, axis=-1, keepdims=True) + EPS) * gf_ref[...]
    o_ref[...] = y


def _unsort_residual(x2, ys, meta, gate, gfinal, final_norm, tiles_per_batch):
    t, d = x2.shape
    tm = MOE_TM
    nt, rows, _ = ys.shape
    kern = functools.partial(_unsort_kernel, final_norm=final_norm)
    return pl.pallas_call(
        kern,
        grid=(nt,),
        in_specs=[
            pl.BlockSpec((tm, d), lambda i: (i, 0)),
            pl.BlockSpec((1, rows, d), lambda i: (i, 0, 0)),
            pl.BlockSpec((tm, LANES), lambda i: (i, 0)),
            pl.BlockSpec((1, 1, d), lambda i: (i // tiles_per_batch, 0, 0)),
            pl.BlockSpec((1, d), lambda i: (0, 0)),
        ],
        out_specs=pl.BlockSpec((tm, d), lambda i: (i, 0)),
        out_shape=jax.ShapeDtypeStruct((t, d), F32),
        compiler_params=_cparams(1),
        name="moe_unsort_residual",
    )(x2, ys, meta, gate, gfinal)


def moe_layer(x, g, sh2, sc2, g2, w_rg, b_rg, w_re, b_re, w_gate, w_up, w_down, layer, gfinal, final_norm):
    b, s, d = x.shape
    ng, ne = N_EXPERT_GROUPS, EXPERTS_PER_GROUP
    f = w_gate.shape[-1]
    assert s % MOE_TM == 0
    nr = ng + N_EXPERTS
    wr = jnp.concatenate([w_rg.T, w_re.reshape(d, N_EXPERTS).T,
                          jnp.zeros((LANES - nr, d), F32)], axis=0)
    br = jnp.concatenate([b_rg, b_re.reshape(N_EXPERTS),
                          jnp.zeros((LANES - nr,), F32)]).reshape(LANES, 1)
    ti = np.arange(MOE_TM)
    tri = jnp.asarray(ti[:, None] < ti[None, :], F32).astype(BF16)
    x2 = x.reshape(b * s, d)
    tpb = s // MOE_TM
    hs, gs, meta, gid = _route_sort(x2, g.reshape(1, d), sh2, sc2, wr, br, tri, tpb)
    nt = hs.shape[0]
    gid_flat = gid[:, 0, :MOE_NB].reshape(-1)
    order = jnp.argsort(gid_flat, stable=True).astype(jnp.int32)
    grp = gid_flat[order]
    ys = _experts(order, grp,
                  hs.reshape(nt * MOE_NB, MOE_CH, d), gs.reshape(nt * MOE_NB, MOE_CH, LANES),
                  _to_bf16(w_gate, layer).reshape(ng, ne, d, f),
                  _to_bf16(w_up, layer).reshape(ng, ne, d, f),
                  _to_bf16(w_down, layer).reshape(ng, ne * f, d))
    out = _unsort_residual(x2, ys.reshape(nt, MOE_NB * MOE_CH, d), meta, g2,
                           gfinal.reshape(1, d), final_norm, tpb)
    return out.reshape(b, s, d)


def _fft1_kernel(u_ref, w_ref, twr_ref, twi_ref, o_ref, *, n_sub, width):
    n1 = FFT_N1
    y = jnp.dot(w_ref[...], u_ref[0], preferred_element_type=F32)
    for j in range(n_sub):
        cols = slice(j * width, (j + 1) * width)
        yr, yi = y[:n1, cols], y[n1:, cols]
        twr, twi = twr_ref[j], twi_ref[j]
        o_ref[0, :n1, cols] = (yr * twr - yi * twi).astype(o_ref.dtype)
        o_ref[0, n1:, cols] = (yr * twi + yi * twr).astype(o_ref.dtype)


def _fft_stage1(u3, w1, twr, twi, width):
    b, n1, cols = u3.shape
    n_sub = 8
    tn = n_sub * width
    kern = functools.partial(_fft1_kernel, n_sub=n_sub, width=width)
    return pl.pallas_call(
        kern,
        grid=(b, cols // tn),
        in_specs=[
            pl.BlockSpec((1, n1, tn), lambda i, j: (i, 0, j)),
            pl.BlockSpec((2 * n1, n1), lambda i, j: (0, 0)),
            pl.BlockSpec((n_sub, n1, 1), lambda i, j: (j, 0, 0)),
            pl.BlockSpec((n_sub, n1, 1), lambda i, j: (j, 0, 0)),
        ],
        out_specs=pl.BlockSpec((1, 2 * n1, tn), lambda i, j: (i, 0, j)),
        out_shape=jax.ShapeDtypeStruct((b, 2 * n1, cols), BF16),
        compiler_params=_cparams(2),
        name="fft_stage1",
    )(u3, w1, twr, twi)


def _fft2_mix_kernel(y_ref, g_ref, x_ref, cc_ref, sc_ref, w_ref, gate_ref, o_ref):
    kp, n2 = FFT_K1_PER_STEP, FFT_N2
    width = y_ref.shape[-1]
    gd = cc_ref.shape[0]
    half = kp * n2
    y = y_ref[0].reshape(2 * half, width)
    r = jnp.dot(g_ref[...], y, preferred_element_type=F32)
    ar, ai = r[:half].astype(BF16), r[half:].astype(BF16)
    parts = []
    for g in range(N_FOURIER_GROUPS):
        cols = slice(g * gd, (g + 1) * gd)
        parts.append(jnp.dot(ar[:, cols], cc_ref[...], preferred_element_type=F32)
                     + jnp.dot(ai[:, cols], sc_ref[...], preferred_element_type=F32))
    f = jnp.concatenate(parts, axis=1).astype(BF16)
    yv = jnp.dot(f, w_ref[...], preferred_element_type=F32)
    out = x_ref[0].reshape(half, width) + gate_ref[0] * yv
    o_ref[0] = out.reshape(n2, kp, width)


def _fft2_mix(y5, gmat, x4, cc, sc, w_bf16, gate):
    b, _, n1, n2, width = y5.shape
    kp = FFT_K1_PER_STEP
    gd = cc.shape[0]
    return pl.pallas_call(
        _fft2_mix_kernel,
        grid=(b, n1 // kp),
        in_specs=[
            pl.BlockSpec((1, 2, kp, n2, width), lambda i, j: (i, 0, j, 0, 0)),
            pl.BlockSpec((2 * kp * n2, 2 * kp * n2), lambda i, j: (0, 0)),
            pl.BlockSpec((1, n2, kp, width), lambda i, j: (i, 0, j, 0)),
            pl.BlockSpec((gd, gd), lambda i, j: (0, 0)),
            pl.BlockSpec((gd, gd), lambda i, j: (0, 0)),
            pl.BlockSpec((width, width), lambda i, j: (0, 0)),
            pl.BlockSpec((1, 1, width), lambda i, j: (i, 0, 0)),
        ],
        out_specs=pl.BlockSpec((1, n2, kp, width), lambda i, j: (i, 0, j, 0)),
        out_shape=jax.ShapeDtypeStruct((b, n2, n1, width), F32),
        compiler_params=_cparams(2),
        name="fft_stage2_mix",
    )(y5, gmat, x4, cc, sc, w_bf16, gate)


def _dft_tables(seq, gd):
    n1, n2, kp = FFT_N1, FFT_N2, FFT_K1_PER_STEP

    def cs(num, den):
        ang = (2.0 * math.pi / den) * (num % den).astype(np.float64)
        return np.cos(ang), np.sin(ang)

    i1 = np.arange(n1)
    c1, s1 = cs(np.outer(i1, i1), n1)
    w1 = np.concatenate([c1, -s1], axis=0)
    ct, st = cs(np.outer(np.arange(n2), i1), seq)
    twr, twi = ct[:, :, None], -st[:, :, None]
    i2 = np.arange(n2)
    c2, s2 = cs(np.outer(i2, i2), n2)
    m3 = np.block([[c2, s2], [-s2, c2]])
    m3 = m3.reshape(2, n2, 2, n2)
    gmat = np.zeros((2, n2, kp, 2, kp, n2))
    for j in range(kp):
        gmat[:, :, j, :, j, :] = m3
    gmat = gmat.reshape(2 * n2 * kp, 2 * kp * n2)
    ic = np.arange(gd)
    cc, sc = cs(np.outer(ic, ic), gd)
    norm = 1.0 / math.sqrt(seq * gd)
    return tuple(jnp.asarray(t, F32) for t in (w1, twr, twi, gmat, cc * norm, sc * norm))


def kernel(x, c, ada_w, ada_b, norm_mix, norm_ffn, attn_w_in, attn_lam_q1, attn_lam_k1, attn_lam_q2, attn_lam_k2, attn_subln, attn_w_out, fourier_w_in, fourier_w_out, router_group_w, router_group_b, router_expert_w, router_expert_b, expert_w_gate, expert_w_up, expert_w_down, norm_final):
    b, s, d = x.shape
    depth = ada_w.shape[0]
    assert s == FFT_N1 * FFT_N2 and depth == 2

    c8 = jnp.zeros((8, d), F32).at[:b].set(c)
    mod = _ada(c8, ada_w, ada_b.reshape(depth, 1, N_ADA * d))[:, :b]
    mod = mod.reshape(depth, b, N_ADA, 1, d)

    slopes = (2.0 ** (-8.0 * jnp.arange(1, N_DIFF_HEADS + 1, dtype=F32) / N_DIFF_HEADS)
              ).reshape(N_DIFF_HEADS, 1, 1)
    gd = d // N_FOURIER_GROUPS
    w1, twr, twi, gmat, cc, sc = _dft_tables(s, gd)
    w1, gmat, cc, sc = (t.astype(BF16) for t in (w1, gmat, cc, sc))

    for i in range(depth):
        sh1, sc1, g1, sh2, sc2, g2 = [mod[i, :, k] for k in range(N_ADA)]
        j = i // 2
        if i % 2 == 0:
            qscale = jnp.concatenate([jnp.full((d,), LOG2E * DIFF_HEAD_DIM ** -0.5, F32),
                                      jnp.ones((2 * d,), F32)])
            w_in = (attn_w_in[j] * qscale).astype(BF16)
            qkv = _norm_mod_matmul(x, norm_mix[i].reshape(1, d), sh1, sc1, w_in, tm=512)
            lam_init = 0.8 - 0.6 * math.exp(-0.3 * i)
            o = _diff_attention(qkv, slopes,
                                attn_lam_q1[j].reshape(1, -1), attn_lam_k1[j].reshape(1, -1),
                                attn_lam_q2[j].reshape(1, -1), attn_lam_k2[j].reshape(1, -1),
                                attn_subln[j].reshape(-1, 1), lam_init)
            x = _proj_residual(x, o, attn_w_out[j].astype(BF16), g1, tm=512)
        else:
            u = _norm_mod_matmul(x, norm_mix[i].reshape(1, d), sh1, sc1,
                                 fourier_w_in[j].astype(BF16), tm=512)
            y = _fft_stage1(u.reshape(b, FFT_N1, FFT_N2 * d), w1, twr, twi, d)
            x = _fft2_mix(y.reshape(b, 2, FFT_N1, FFT_N2, d), gmat, x.reshape(b, FFT_N2, FFT_N1, d),
                          cc, sc, fourier_w_out[j].astype(BF16), g1).reshape(b, s, d)

        x = moe_layer(x, norm_ffn[i], sh2, sc2, g2, router_group_w[i], router_group_b[i],
                      router_expert_w[i], router_expert_b[i], expert_w_gate, expert_w_up,
                      expert_w_down, i, norm_final, i == depth - 1)
    return x
```

```python
import functools
import math

import numpy as np
import jax
import jax.numpy as jnp
from jax import lax
from jax.experimental import pallas as pl
from jax.experimental.pallas import tpu as pltpu

F32 = jnp.float32
BF16 = jnp.bfloat16

EPS = 1e-6
N_DIFF_HEADS = 8
DIFF_HEAD_DIM = 64
DIFF_V_DIM = 128
N_FOURIER_GROUPS = 4
N_EXPERT_GROUPS = 4
EXPERTS_PER_GROUP = 4
N_EXPERTS = 16
N_ADA = 6

LANES = 128
VMEM_LIMIT_BYTES = 56 * 1024 * 1024

FFT_N1 = 128
FFT_N2 = 64
FFT_K1_PER_STEP = 8


def _cparams(n_axes):
    return pltpu.CompilerParams(
        dimension_semantics=("arbitrary",) * n_axes,
        vmem_limit_bytes=VMEM_LIMIT_BYTES,
    )


def _rms_mod(x, g, shift, scale):
    y = x * lax.rsqrt(jnp.mean(x * x, axis=-1, keepdims=True) + EPS)
    return (y * g) * (1.0 + scale) + shift


def _ada_kernel(c_ref, w_ref, b_ref, o_ref):
    c = c_ref[...]
    cond = c / (1.0 + jnp.exp(-c))
    o_ref[0] = jnp.dot(cond, w_ref[0], precision=lax.Precision.HIGHEST,
                       preferred_element_type=F32) + b_ref[0]


def _ada(c8, ada_w, ada_b3):
    depth, d, n = ada_w.shape
    tn = 1536
    return pl.pallas_call(
        _ada_kernel,
        grid=(depth, n // tn),
        in_specs=[
            pl.BlockSpec((8, d), lambda i, j: (0, 0)),
            pl.BlockSpec((1, d, tn), lambda i, j: (i, 0, j)),
            pl.BlockSpec((1, 1, tn), lambda i, j: (i, 0, j)),
        ],
        out_specs=pl.BlockSpec((1, 8, tn), lambda i, j: (i, 0, j)),
        out_shape=jax.ShapeDtypeStruct((depth, 8, n), F32),
        compiler_params=_cparams(2),
        name="ada_mod",
    )(c8, ada_w, ada_b3)


def _nmm_kernel(x_ref, g_ref, sh_ref, sc_ref, w_ref, o_ref):
    h = _rms_mod(x_ref[0], g_ref[...], sh_ref[0], sc_ref[0])
    o_ref[0] = jnp.dot(h.astype(BF16), w_ref[...],
                       preferred_element_type=F32).astype(o_ref.dtype)


def _norm_mod_matmul(x, g, shift, scale, w_bf16, tm):
    b, s, d = x.shape
    n = w_bf16.shape[1]
    return pl.pallas_call(
        _nmm_kernel,
        grid=(b, s // tm),
        in_specs=[
            pl.BlockSpec((1, tm, d), lambda i, j: (i, j, 0)),
            pl.BlockSpec((1, d), lambda i, j: (0, 0)),
            pl.BlockSpec((1, 1, d), lambda i, j: (i, 0, 0)),
            pl.BlockSpec((1, 1, d), lambda i, j: (i, 0, 0)),
            pl.BlockSpec((d, n), lambda i, j: (0, 0)),
        ],
        out_specs=pl.BlockSpec((1, tm, n), lambda i, j: (i, j, 0)),
        out_shape=jax.ShapeDtypeStruct((b, s, n), BF16),
        compiler_params=_cparams(2),
        name="norm_mod_matmul",
    )(x, g, shift, scale, w_bf16)


ATT_TQ = 256
ATT_TK = 512
ATT_PAIR = 2 * ATT_TK
ATT_NQP = ATT_PAIR // ATT_TQ
ATT_QSUB = 4
ATT_TRIPS_FAST = (4, 2)
ATT_TRIPS_SAFE = (2,)
LOG2E = 1.4426950408889634
ATT_SKIP_BITS = 40.0
ATT_FAST_MAX_U = 36.0


def _attn_bounds_kernel(q_ref, k_ref, slope_ref, lo_ref, hi_ref, fast_ref, ub_ref, *, seq):
    tq, pair = ATT_TQ, ATT_PAIR
    nq = seq // tq
    dd = 2 * DIFF_HEAD_DIM
    r = lax.broadcasted_iota(jnp.int32, (dd, dd), 0)
    c = lax.broadcasted_iota(jnp.int32, (dd, dd), 1)
    halves = jnp.where(c == r // DIFF_HEAD_DIM, 1.0, 0.0).astype(BF16)

    def half_norm2(ref):
        x = ref[0].astype(F32)
        return jnp.dot((x * x).astype(BF16), halves, preferred_element_type=F32)

    qn2 = jnp.max(jnp.max(half_norm2(q_ref).reshape(nq, tq, dd), axis=1), axis=1, keepdims=True)
    kn2 = jnp.max(jnp.max(half_norm2(k_ref), axis=0, keepdims=True), axis=1, keepdims=True)
    ub = jnp.sqrt(qn2 * kn2) * (1.0 + 2.0 ** -6)
    slope = slope_ref[0] * LOG2E
    dist = (2.0 * ub + (ATT_SKIP_BITS + math.log2(seq))) / slope
    dist = jnp.ceil(jnp.minimum(dist, float(seq))).astype(jnp.int32)
    q0 = lax.broadcasted_iota(jnp.int32, (nq, 1), 0) * tq
    jmin = jnp.maximum(q0 - dist + 1, 0)
    jmax = jnp.minimum(q0 + (tq - 1) + dist - 1, seq - 1)
    shift = jnp.full((nq, 1), int(math.log2(pair)), jnp.int32)
    lo_ref[0, 0] = lax.shift_right_logical(jmin, shift)
    hi_ref[0, 0] = lax.shift_right_logical(jmax, shift) + 1
    fast_ref[0, 0] = (ub <= ATT_FAST_MAX_U).astype(jnp.int32)
    ub_ref[0, 0] = ub


def _attn_bounds(qkv, slopes):
    b, s, _ = qkv.shape
    h, dv = N_DIFF_HEADS, DIFF_V_DIM
    nq = s // ATT_TQ
    kern = functools.partial(_attn_bounds_kernel, seq=s)
    oblk = pl.BlockSpec((1, 1, nq, 1), lambda bi, hi: (bi, hi, 0, 0))
    ishape = jax.ShapeDtypeStruct((b, h, nq, 1), jnp.int32)
    return pl.pallas_call(
        kern,
        grid=(b, h),
        in_specs=[
            pl.BlockSpec((1, s, dv), lambda bi, hi: (bi, 0, hi)),
            pl.BlockSpec((1, s, dv), lambda bi, hi: (bi, 0, h + hi)),
            pl.BlockSpec((1, 1, 1), lambda bi, hi: (hi, 0, 0)),
        ],
        out_specs=[oblk, oblk, oblk, oblk],
        out_shape=[ishape, ishape, ishape, jax.ShapeDtypeStruct((b, h, nq, 1), F32)],
        compiler_params=_cparams(2),
        name="attn_bounds",
    )(qkv, qkv, slopes)


def _attn_kernel(lo_ref, hi_ref, fast_ref, q_ref, k_ref, v_ref, ub_ref, slope_ref,
                 lq1_ref, lk1_ref, lq2_ref, lk2_ref, sub_ref,
                 o_ref, vt_ref, bias_ref, u_ref, m_ref, l_ref, acc_ref, *, lam_init, seq):
    tq, tk, pair, nqp = ATT_TQ, ATT_TK, ATT_PAIR, ATT_NQP
    dh = DIFF_HEAD_DIM
    step = pl.program_id(2)
    slope = slope_ref[0] * LOG2E

    @pl.when(step == 0)
    def _():
        for c in range(seq // 512):
            vt_ref[:, c * 512:(c + 1) * 512] = v_ref[0, c * 512:(c + 1) * 512, :].T
        for s in range(2):
            rows = slice(s * tk, (s + 1) * tk)
            r = lax.broadcasted_iota(jnp.int32, (tk, 2 * tq), 0) + s * tk
            c = lax.broadcasted_iota(jnp.int32, (tk, 2 * tq), 1)
            c = jnp.where(c >= tq, c - tq, c)
            rc = (r - c).astype(F32)
            bias_ref[0, rows, :] = slope * rc
            bias_ref[1, rows, :] = -(slope * rc)
            for w in range(nqp):
                bias_ref[2 + w, rows, :] = -(slope * jnp.abs(rc - float(w * tq)))

    lam = (jnp.exp(jnp.sum(lq1_ref[...] * lk1_ref[...], axis=-1, keepdims=True))
           - jnp.exp(jnp.sum(lq2_ref[...] * lk2_ref[...], axis=-1, keepdims=True))
           + lam_init)

    def one_tile(t):
        qi = step * ATT_QSUB + t
        tile = (pl.program_id(0) * N_DIFF_HEADS + pl.program_id(1)) * (seq // tq) + qi
        lo, hi = lo_ref[tile], hi_ref[tile]
        lt_ref, at_ref = l_ref.at[t], acc_ref.at[t]

        qt = q_ref[0, t * tq:(t + 1) * tq, :].T
        z = jnp.zeros((dh, tq), BF16)
        qbd = jnp.concatenate([jnp.concatenate([qt[:dh], z], axis=1),
                               jnp.concatenate([z, qt[dh:]], axis=1)], axis=0)

        lt_ref[...] = jnp.zeros(lt_ref.shape, F32)
        at_ref[...] = jnp.zeros(at_ref.shape, F32)

        kd = qi // nqp
        w = qi % nqp

        def mode(kp):
            delta = (kp * pair - qi * tq).astype(F32)
            idx = jnp.where(kp < kd, 0, jnp.where(kp > kd, 1, 2 + w))
            sgn = jnp.where(kp < kd, delta, jnp.where(kp > kd, -delta, 0.0))
            return idx, slope * sgn

        def scores(kp, idx, s):
            k0 = pl.multiple_of(kp * pair + s * tk, tk)
            kblk = k_ref[0, pl.ds(k0, tk), :]
            u = jnp.dot(kblk, qbd, preferred_element_type=F32)
            return u + bias_ref[idx, s * tk:(s + 1) * tk, :]

        def values(kp, s):
            k0 = pl.multiple_of(kp * pair + s * tk, tk)
            return vt_ref[:, pl.ds(k0, tk)]

        def pipelined(qk_phase, sm_phase, carry0, trips):
            def body(kp, carry_a):
                idx, cst = mode(kp)
                carry_b = qk_phase(kp, idx, 1)
                sm_phase(kp, cst, 0, carry_a)
                kn = jnp.minimum(kp + 1, hi - 1)
                idx_n, _ = mode(kn)
                carry_a_next = qk_phase(kn, idx_n, 0)
                sm_phase(kp, cst, 1, carry_b)
                return carry_a_next

            start, carry = lo, carry0
            for npt in trips:
                ntrip = (hi - start) // npt

                def trip(i, c, start=start, npt=npt):
                    for j in range(npt):
                        c = body(start + npt * i + j, c)
                    return c

                carry = lax.fori_loop(0, ntrip, trip, carry)
                start = start + npt * ntrip
            lax.fori_loop(start, hi, body, carry)

        def fixed_reference():
            ub = ub_ref[t]

            def qk_phase(kp, idx, s):
                u_ref[s] = scores(kp, idx, s)
                return 0

            def sm_phase(kp, cst, s, _):
                p = jnp.exp2(u_ref[s] + (cst - ub))
                lt_ref[...] += jnp.sum(p, axis=0, keepdims=True)
                at_ref[...] += jnp.dot(values(kp, s), p.astype(BF16), preferred_element_type=F32)

            pipelined(qk_phase, sm_phase, 0, ATT_TRIPS_FAST)

        def running_max():
            m_ref[...] = jnp.full(m_ref.shape, -jnp.inf, F32)

            def qk_phase(kp, idx, s):
                v = scores(kp, idx, s)
                u_ref[s] = v
                return jnp.max(v, axis=0, keepdims=True)

            def sm_phase(kp, cst, s, mx):
                m_old = m_ref[...]
                m_new = jnp.maximum(m_old, mx + cst)
                alpha = jnp.exp2(m_old - m_new)
                p = jnp.exp2(u_ref[s] - (m_new - cst))
                lt_ref[...] = alpha * lt_ref[...] + jnp.sum(p, axis=0, keepdims=True)
                m_ref[...] = m_new
                at_ref[...] = alpha * at_ref[...] + jnp.dot(values(kp, s), p.astype(BF16),
                                                            preferred_element_type=F32)

            pipelined(qk_phase, sm_phase, jnp.max(u_ref[0], axis=0, keepdims=True), ATT_TRIPS_SAFE)

        idx0, _ = mode(lo)
        u_ref[0] = scores(lo, idx0, 0)
        lax.cond(fast_ref[tile] == 1, fixed_reference, running_max)

        acc = at_ref[...]
        l = lt_ref[...]
        ot = acc[:, :tq] / l[:, :tq] - lam * (acc[:, tq:] / l[:, tq:])
        ms = jnp.mean(ot * ot, axis=0, keepdims=True)
        ot = ot * lax.rsqrt(ms + EPS) * sub_ref[...] * (1.0 - lam_init)
        o_ref[0, t * tq:(t + 1) * tq, :] = ot.T.astype(o_ref.dtype)

    for t in range(ATT_QSUB):
        one_tile(t)


def _diff_attention(qkv, slopes, lq1, lk1, lq2, lk2, subln_col, lam_init):
    b, s, _ = qkv.shape
    h, dv = N_DIFF_HEADS, DIFF_V_DIM
    tq, tk, qsub = ATT_TQ, ATT_TK, ATT_QSUB
    nq = s // tq
    assert s % ATT_PAIR == 0 and nq % qsub == 0
    nsteps = nq // qsub
    lo, hi, fast, ub = _attn_bounds(qkv, slopes)
    kern = functools.partial(_attn_kernel, lam_init=lam_init, seq=s)
    vec = pl.BlockSpec((1, DIFF_HEAD_DIM), lambda bi, hi, qi, *_: (0, 0))
    grid_spec = pltpu.PrefetchScalarGridSpec(
        num_scalar_prefetch=3,
        grid=(b, h, nsteps),
        in_specs=[
            pl.BlockSpec((1, qsub * tq, dv), lambda bi, hi, qi, *_: (bi, qi, hi)),
            pl.BlockSpec((1, s, dv), lambda bi, hi, qi, *_: (bi, 0, h + hi)),
            pl.BlockSpec((1, s, dv), lambda bi, hi, qi, *_: (bi, 0, 2 * h + hi)),
            pl.BlockSpec((qsub, 1, 1), lambda bi, hi, qi, *_: ((bi * h + hi) * nsteps + qi, 0, 0)),
            pl.BlockSpec((1, 1, 1), lambda bi, hi, qi, *_: (hi, 0, 0)),
            vec, vec, vec, vec,
            pl.BlockSpec((dv, 1), lambda bi, hi, qi, *_: (0, 0)),
        ],
        out_specs=pl.BlockSpec((1, qsub * tq, dv), lambda bi, hi, qi, *_: (bi, qi, hi)),
        scratch_shapes=[
            pltpu.VMEM((dv, s), BF16),
            pltpu.VMEM((2 + ATT_NQP, ATT_PAIR, 2 * tq), F32),
            pltpu.VMEM((2, tk, 2 * tq), F32),
            pltpu.VMEM((1, 2 * tq), F32),
            pltpu.VMEM((qsub, 1, 2 * tq), F32),
            pltpu.VMEM((qsub, dv, 2 * tq), F32),
        ],
    )
    return pl.pallas_call(
        kern,
        grid_spec=grid_spec,
        out_shape=jax.ShapeDtypeStruct((b, s, h * dv), BF16),
        compiler_params=_cparams(3),
        name="diff_attention",
    )(lo.reshape(-1), hi.reshape(-1), fast.reshape(-1),
      qkv, qkv, qkv, ub.reshape(-1, 1, 1), slopes, lq1, lk1, lq2, lk2, subln_col)


def _proj_res_kernel(x_ref, a_ref, w_ref, gate_ref, o_ref):
    y = jnp.dot(a_ref[0], w_ref[...], preferred_element_type=F32)
    o_ref[0] = x_ref[0] + gate_ref[0] * y


def _proj_residual(x, a, w_bf16, gate, tm):
    b, s, d = x.shape
    kdim = a.shape[-1]
    return pl.pallas_call(
        _proj_res_kernel,
        grid=(b, s // tm),
        in_specs=[
            pl.BlockSpec((1, tm, d), lambda i, j: (i, j, 0)),
            pl.BlockSpec((1, tm, kdim), lambda i, j: (i, j, 0)),
            pl.BlockSpec((kdim, d), lambda i, j: (0, 0)),
            pl.BlockSpec((1, 1, d), lambda i, j: (i, 0, 0)),
        ],
        out_specs=pl.BlockSpec((1, tm, d), lambda i, j: (i, j, 0)),
        out_shape=jax.ShapeDtypeStruct((b, s, d), F32),
        compiler_params=_cparams(2),
        name="proj_residual",
    )(x, a, w_bf16, gate)


def _routing_gates_t(lt):
    ng, ne = N_EXPERT_GROUPS, EXPERTS_PER_GROUP
    gl = [lt[g:g + 1, :] for g in range(ng)]
    gmax = functools.reduce(jnp.maximum, gl)
    den = functools.reduce(lambda a, b: a + b, [jnp.exp(v - gmax) for v in gl])
    g_w = 1.0 / den
    sel = []
    taken = jnp.zeros(gmax.shape, jnp.bool_)
    for g in range(ng):
        s = jnp.logical_and(gl[g] == gmax, jnp.logical_not(taken))
        taken = jnp.logical_or(taken, s)
        sel.append(s)
    ev = []
    for e in range(ne):
        v = jnp.zeros_like(gmax)
        for g in range(ng):
            row = ng + ne * g + e
            v = jnp.where(sel[g], lt[row:row + 1, :], v)
        ev.append(v)
    top1 = functools.reduce(jnp.maximum, ev)
    is1 = []
    taken = jnp.zeros(gmax.shape, jnp.bool_)
    for e in range(ne):
        s = jnp.logical_and(ev[e] == top1, jnp.logical_not(taken))
        taken = jnp.logical_or(taken, s)
        is1.append(s)
    rest = [jnp.where(is1[e], -jnp.inf, ev[e]) for e in range(ne)]
    top2 = functools.reduce(jnp.maximum, rest)
    is2 = []
    taken = jnp.zeros(gmax.shape, jnp.bool_)
    for e in range(ne):
        s = jnp.logical_and(jnp.logical_and(rest[e] == top2, jnp.logical_not(is1[e])),
                            jnp.logical_not(taken))
        taken = jnp.logical_or(taken, s)
        is2.append(s)
    ex = jnp.exp(top2 - top1)
    w1 = 1.0 / (1.0 + ex)
    w2 = ex / (1.0 + ex)
    ew = [jnp.where(is1[e], w1, jnp.where(is2[e], w2, 0.0)) for e in range(ne)]
    rows = [jnp.where(sel[g], g_w * ew[e], 0.0) for g in range(ng) for e in range(ne)]
    return jnp.concatenate(rows, axis=0), sel


MOE_TM = 1024
MOE_CH = 128
MOE_NB = 11
MOE_POS_LANE = 3 * N_EXPERTS


def _route_sort_kernel(x_ref, g_ref, sh_ref, sc_ref, wr_ref, br_ref, tri_ref,
                       hs_ref, gs_ref, meta_ref, gid_ref):
    ng, ch, nb = N_EXPERT_GROUPS, MOE_CH, MOE_NB
    tm = x_ref.shape[0]
    h32 = _rms_mod(x_ref[...], g_ref[...], sh_ref[0], sc_ref[0])
    hb = h32.astype(BF16)
    hl = (h32 - hb.astype(F32)).astype(BF16)
    wr = wr_ref[...]
    wb = wr.astype(BF16)
    wl = (wr - wb.astype(F32)).astype(BF16)
    nt = (((1,), (1,)), ((), ()))
    l2 = lax.dot_general(jnp.concatenate([wb, wl], axis=0), hb, nt, preferred_element_type=F32)
    lt = (l2[:LANES] + l2[LANES:] + lax.dot_general(wb, hl, nt, preferred_element_type=F32)
          + br_ref[...])
    gates_t, sel = _routing_gates_t(lt)
    oh = jnp.concatenate([m.astype(F32) for m in sel] + [jnp.zeros((16 - ng, tm), F32)], axis=0)
    cnt = jnp.dot(oh.astype(BF16), tri_ref[...], preferred_element_type=F32)
    rank = jnp.sum(oh * cnt, axis=0, keepdims=True)
    n = jnp.sum(oh, axis=1, keepdims=True)
    nch = jnp.floor((n + float(ch - 1)) * (1.0 / ch))
    first = [jnp.zeros((1, 1), F32)]
    for g in range(ng):
        first.append(first[-1] + nch[g:g + 1, :])
    seg = functools.reduce(lambda a, b: a + b,
                           [oh[g:g + 1, :] * (first[g] * float(ch)) for g in range(ng)])
    pos = seg + rank
    rows = lax.broadcasted_iota(jnp.int32, (nb * ch, tm), 0)
    perm = jnp.where(rows == pos.astype(jnp.int32), 1.0, 0.0).astype(BF16)
    hs_ref[0] = jnp.dot(perm, hb, preferred_element_type=F32).astype(BF16)
    hi = gates_t.astype(BF16).astype(F32)
    r1 = gates_t - hi
    mid = r1.astype(BF16).astype(F32)
    lo = r1 - mid
    meta_t = jnp.concatenate([hi, mid, lo, pos, jnp.zeros((LANES - MOE_POS_LANE - 1, tm), F32)], axis=0)
    meta = meta_t.T
    meta_ref[...] = meta
    gs_ref[0] = jnp.dot(perm, meta.astype(BF16), preferred_element_type=F32)
    lane = lax.broadcasted_iota(jnp.int32, (1, LANES), 1).astype(F32)
    gid = functools.reduce(lambda a, b: a + b,
                           [jnp.where(lane >= first[g + 1], 1.0, 0.0) for g in range(ng)])
    gid_ref[0] = gid.astype(jnp.int32)


def _route_sort(x2, g, shift, scale, wr, br, tri, tiles_per_batch):
    t, d = x2.shape
    tm, rows = MOE_TM, MOE_NB * MOE_CH
    nt = t // tm
    bvec = lambda i: (i // tiles_per_batch, 0, 0)
    return pl.pallas_call(
        _route_sort_kernel,
        grid=(nt,),
        in_specs=[
            pl.BlockSpec((tm, d), lambda i: (i, 0)),
            pl.BlockSpec((1, d), lambda i: (0, 0)),
            pl.BlockSpec((1, 1, d), bvec),
            pl.BlockSpec((1, 1, d), bvec),
            pl.BlockSpec((LANES, d), lambda i: (0, 0)),
            pl.BlockSpec((LANES, 1), lambda i: (0, 0)),
            pl.BlockSpec((tm, tm), lambda i: (0, 0)),
        ],
        out_specs=[
            pl.BlockSpec((1, rows, d), lambda i: (i, 0, 0)),
            pl.BlockSpec((1, rows, LANES), lambda i: (i, 0, 0)),
            pl.BlockSpec((tm, LANES), lambda i: (i, 0)),
            pl.BlockSpec((1, 1, LANES), lambda i: (i, 0, 0)),
        ],
        out_shape=[
            jax.ShapeDtypeStruct((nt, rows, d), BF16),
            jax.ShapeDtypeStruct((nt, rows, LANES), F32),
            jax.ShapeDtypeStruct((t, LANES), F32),
            jax.ShapeDtypeStruct((nt, 1, LANES), jnp.int32),
        ],
        compiler_params=_cparams(1),
        name="moe_route_sort",
    )(x2, g, shift, scale, wr, br, tri)


def _experts_kernel(order_ref, grp_ref, first_ref, hs_ref, gs_ref, wg_ref, wu_ref, wd_ref, ys_ref,
                    wgb_ref, wub_ref, wdb_ref):
    ne = EXPERTS_PER_GROUP
    step = pl.program_id(0)
    grp = grp_ref[step]

    @pl.when(first_ref[step] == 1)
    def _():
        wgb_ref[...] = wg_ref[0].astype(BF16)
        wub_ref[...] = wu_ref[0].astype(BF16)
        wdb_ref[...] = wd_ref[0].astype(BF16)

    @pl.when(grp < N_EXPERT_GROUPS)
    def _():
        h = hs_ref[0]
        gs = gs_ref[0]
        lane = lax.broadcasted_iota(jnp.int32, gs.shape, 1)
        hid = []
        for e in range(ne):
            idx = grp * ne + e
            pick = jnp.logical_or(jnp.logical_or(lane == idx, lane == idx + N_EXPERTS),
                                  lane == idx + 2 * N_EXPERTS)
            gate = jnp.sum(jnp.where(pick, gs, 0.0), axis=1, keepdims=True)
            a = jnp.dot(h, wgb_ref[e], preferred_element_type=F32)
            u = jnp.dot(h, wub_ref[e], preferred_element_type=F32)
            hid.append(((a / (1.0 + jnp.exp(-a))) * u * gate).astype(BF16))
        hid = jnp.concatenate(hid, axis=1)
        ys_ref[0] = jnp.dot(hid, wdb_ref[...], preferred_element_type=F32).astype(ys_ref.dtype)

    @pl.when(grp >= N_EXPERT_GROUPS)
    def _():
        ys_ref[0] = jnp.zeros(ys_ref.shape[1:], ys_ref.dtype)


def _experts(order, grp, first, hs, gs, wg4, wu4, wd4, layer):
    nchunks, ch, d = hs.shape
    ne, f = wg4.shape[1], wg4.shape[-1]
    ng = N_EXPERT_GROUPS
    wsel = lambda s, order, grp, first: (layer * ng + jnp.minimum(grp[s], ng - 1), 0, 0, 0)
    wsel3 = lambda s, order, grp, first: (layer * ng + jnp.minimum(grp[s], ng - 1), 0, 0)
    chunk = lambda s, order, grp, first: (order[s], 0, 0)
    once = pl.Buffered(1)
    grid_spec = pltpu.PrefetchScalarGridSpec(
        num_scalar_prefetch=3,
        grid=(nchunks,),
        in_specs=[
            pl.BlockSpec((1, ch, d), chunk),
            pl.BlockSpec((1, ch, LANES), chunk),
            pl.BlockSpec((1, ne, d, f), wsel, pipeline_mode=once),
            pl.BlockSpec((1, ne, d, f), wsel, pipeline_mode=once),
            pl.BlockSpec((1, ne * f, d), wsel3, pipeline_mode=once),
        ],
        out_specs=pl.BlockSpec((1, ch, d), chunk),
        scratch_shapes=[
            pltpu.VMEM((ne, d, f), BF16),
            pltpu.VMEM((ne, d, f), BF16),
            pltpu.VMEM((ne * f, d), BF16),
        ],
    )
    return pl.pallas_call(
        _experts_kernel,
        grid_spec=grid_spec,
        out_shape=jax.ShapeDtypeStruct((nchunks, ch, d), BF16),
        compiler_params=_cparams(1),
        name="moe_experts",
    )(order, grp, first, hs, gs, wg4, wu4, wd4)


def _unsort_kernel(x_ref, ys_ref, meta_ref, gate_ref, gf_ref, o_ref, *, final_norm):
    rows = ys_ref.shape[1]
    tm = x_ref.shape[0]
    pos = meta_ref[:, MOE_POS_LANE:MOE_POS_LANE + 1].astype(jnp.int32)
    cols = lax.broadcasted_iota(jnp.int32, (tm, rows), 1)
    perm_t = jnp.where(cols == pos, 1.0, 0.0).astype(BF16)
    y = jnp.dot(perm_t, ys_ref[0], preferred_element_type=F32)
    y = x_ref[...] + gate_ref[0] * y
    if final_norm:
        y = y * lax.rsqrt(jnp.mean(y * y, axis=-1, keepdims=True) + EPS) * gf_ref[...]
    o_ref[...] = y


def _unsort_residual(x2, ys, meta, gate, gfinal, final_norm, tiles_per_batch):
    t, d = x2.shape
    tm = MOE_TM
    nt, rows, _ = ys.shape
    kern = functools.partial(_unsort_kernel, final_norm=final_norm)
    return pl.pallas_call(
        kern,
        grid=(nt,),
        in_specs=[
            pl.BlockSpec((tm, d), lambda i: (i, 0)),
            pl.BlockSpec((1, rows, d), lambda i: (i, 0, 0)),
            pl.BlockSpec((tm, LANES), lambda i: (i, 0)),
            pl.BlockSpec((1, 1, d), lambda i: (i // tiles_per_batch, 0, 0)),
            pl.BlockSpec((1, d), lambda i: (0, 0)),
        ],
        out_specs=pl.BlockSpec((tm, d), lambda i: (i, 0)),
        out_shape=jax.ShapeDtypeStruct((t, d), F32),
        compiler_params=_cparams(1),
        name="moe_unsort_residual",
    )(x2, ys, meta, gate, gfinal)


def moe_layer(x, g, sh2, sc2, g2, w_rg, b_rg, w_re, b_re, w_gate, w_up, w_down, layer, gfinal, final_norm):
    b, s, d = x.shape
    ng, ne = N_EXPERT_GROUPS, EXPERTS_PER_GROUP
    f = w_gate.shape[-1]
    assert s % MOE_TM == 0
    nr = ng + N_EXPERTS
    wr = jnp.concatenate([w_rg.T, w_re.reshape(d, N_EXPERTS).T,
                          jnp.zeros((LANES - nr, d), F32)], axis=0)
    br = jnp.concatenate([b_rg, b_re.reshape(N_EXPERTS),
                          jnp.zeros((LANES - nr,), F32)]).reshape(LANES, 1)
    ti = np.arange(MOE_TM)
    tri = jnp.asarray(ti[:, None] < ti[None, :], F32).astype(BF16)
    x2 = x.reshape(b * s, d)
    tpb = s // MOE_TM
    hs, gs, meta, gid = _route_sort(x2, g.reshape(1, d), sh2, sc2, wr, br, tri, tpb)
    nt = hs.shape[0]
    gid_flat = gid[:, 0, :MOE_NB].reshape(-1)
    order = jnp.argsort(gid_flat, stable=True).astype(jnp.int32)
    grp = gid_flat[order]
    first = jnp.concatenate([jnp.ones((1,), jnp.int32), (grp[1:] != grp[:-1]).astype(jnp.int32)])
    nl = w_gate.shape[0]
    ys = _experts(order, grp, first,
                  hs.reshape(nt * MOE_NB, MOE_CH, d), gs.reshape(nt * MOE_NB, MOE_CH, LANES),
                  w_gate.reshape(nl * ng, ne, d, f), w_up.reshape(nl * ng, ne, d, f),
                  w_down.reshape(nl * ng, ne * f, d), layer)
    out = _unsort_residual(x2, ys.reshape(nt, MOE_NB * MOE_CH, d), meta, g2,
                           gfinal.reshape(1, d), final_norm, tpb)
    return out.reshape(b, s, d)


def _fft1_kernel(u_ref, w_ref, twr_ref, twi_ref, o_ref, *, n_sub, width):
    n1 = FFT_N1
    y = jnp.dot(w_ref[...], u_ref[0], preferred_element_type=F32)
    for j in range(n_sub):
        cols = slice(j * width, (j + 1) * width)
        yr, yi = y[:n1, cols], y[n1:, cols]
        twr, twi = twr_ref[j], twi_ref[j]
        o_ref[0, :n1, cols] = (yr * twr - yi * twi).astype(o_ref.dtype)
        o_ref[0, n1:, cols] = (yr * twi + yi * twr).astype(o_ref.dtype)


def _fft_stage1(u3, w1, twr, twi, width):
    b, n1, cols = u3.shape
    n_sub = 8
    tn = n_sub * width
    kern = functools.partial(_fft1_kernel, n_sub=n_sub, width=width)
    return pl.pallas_call(
        kern,
        grid=(b, cols // tn),
        in_specs=[
            pl.BlockSpec((1, n1, tn), lambda i, j: (i, 0, j)),
            pl.BlockSpec((2 * n1, n1), lambda i, j: (0, 0)),
            pl.BlockSpec((n_sub, n1, 1), lambda i, j: (j, 0, 0)),
            pl.BlockSpec((n_sub, n1, 1), lambda i, j: (j, 0, 0)),
        ],
        out_specs=pl.BlockSpec((1, 2 * n1, tn), lambda i, j: (i, 0, j)),
        out_shape=jax.ShapeDtypeStruct((b, 2 * n1, cols), BF16),
        compiler_params=_cparams(2),
        name="fft_stage1",
    )(u3, w1, twr, twi)


def _fft2_mix_kernel(y_ref, g_ref, x_ref, cc_ref, sc_ref, w_ref, gate_ref, o_ref):
    kp, n2 = FFT_K1_PER_STEP, FFT_N2
    width = y_ref.shape[-1]
    gd = cc_ref.shape[0]
    half = kp * n2
    y = y_ref[0].reshape(2 * half, width)
    r = jnp.dot(g_ref[...], y, preferred_element_type=F32)
    ar, ai = r[:half].astype(BF16), r[half:].astype(BF16)
    parts = []
    for g in range(N_FOURIER_GROUPS):
        cols = slice(g * gd, (g + 1) * gd)
        parts.append(jnp.dot(ar[:, cols], cc_ref[...], preferred_element_type=F32)
                     + jnp.dot(ai[:, cols], sc_ref[...], preferred_element_type=F32))
    f = jnp.concatenate(parts, axis=1).astype(BF16)
    yv = jnp.dot(f, w_ref[...], preferred_element_type=F32)
    out = x_ref[0].reshape(half, width) + gate_ref[0] * yv
    o_ref[0] = out.reshape(n2, kp, width)


def _fft2_mix(y5, gmat, x4, cc, sc, w_bf16, gate):
    b, _, n1, n2, width = y5.shape
    kp = FFT_K1_PER_STEP
    gd = cc.shape[0]
    return pl.pallas_call(
        _fft2_mix_kernel,
        grid=(b, n1 // kp),
        in_specs=[
            pl.BlockSpec((1, 2, kp, n2, width), lambda i, j: (i, 0, j, 0, 0)),
            pl.BlockSpec((2 * kp * n2, 2 * kp * n2), lambda i, j: (0, 0)),
            pl.BlockSpec((1, n2, kp, width), lambda i, j: (i, 0, j, 0)),
            pl.BlockSpec((gd, gd), lambda i, j: (0, 0)),
            pl.BlockSpec((gd, gd), lambda i, j: (0, 0)),
            pl.BlockSpec((width, width), lambda i, j: (0, 0)),
            pl.BlockSpec((1, 1, width), lambda i, j: (i, 0, 0)),
        ],
        out_specs=pl.BlockSpec((1, n2, kp, width), lambda i, j: (i, 0, j, 0)),
        out_shape=jax.ShapeDtypeStruct((b, n2, n1, width), F32),
        compiler_params=_cparams(2),
        name="fft_stage2_mix",
    )(y5, gmat, x4, cc, sc, w_bf16, gate)


def _dft_tables(seq, gd):
    n1, n2, kp = FFT_N1, FFT_N2, FFT_K1_PER_STEP

    def cs(num, den):
        ang = (2.0 * math.pi / den) * (num % den).astype(np.float64)
        return np.cos(ang), np.sin(ang)

    i1 = np.arange(n1)
    c1, s1 = cs(np.outer(i1, i1), n1)
    w1 = np.concatenate([c1, -s1], axis=0)
    ct, st = cs(np.outer(np.arange(n2), i1), seq)
    twr, twi = ct[:, :, None], -st[:, :, None]
    i2 = np.arange(n2)
    c2, s2 = cs(np.outer(i2, i2), n2)
    m3 = np.block([[c2, s2], [-s2, c2]])
    m3 = m3.reshape(2, n2, 2, n2)
    gmat = np.zeros((2, n2, kp, 2, kp, n2))
    for j in range(kp):
        gmat[:, :, j, :, j, :] = m3
    gmat = gmat.reshape(2 * n2 * kp, 2 * kp * n2)
    ic = np.arange(gd)
    cc, sc = cs(np.outer(ic, ic), gd)
    norm = 1.0 / math.sqrt(seq * gd)
    return tuple(jnp.asarray(t, F32) for t in (w1, twr, twi, gmat, cc * norm, sc * norm))


def kernel(x, c, ada_w, ada_b, norm_mix, norm_ffn, attn_w_in, attn_lam_q1, attn_lam_k1, attn_lam_q2, attn_lam_k2, attn_subln, attn_w_out, fourier_w_in, fourier_w_out, router_group_w, router_group_b, router_expert_w, router_expert_b, expert_w_gate, expert_w_up, expert_w_down, norm_final):
    b, s, d = x.shape
    depth = ada_w.shape[0]
    assert s == FFT_N1 * FFT_N2 and depth == 2

    c8 = jnp.zeros((8, d), F32).at[:b].set(c)
    mod = _ada(c8, ada_w, ada_b.reshape(depth, 1, N_ADA * d))[:, :b]
    mod = mod.reshape(depth, b, N_ADA, 1, d)

    slopes = (2.0 ** (-8.0 * jnp.arange(1, N_DIFF_HEADS + 1, dtype=F32) / N_DIFF_HEADS)
              ).reshape(N_DIFF_HEADS, 1, 1)
    gd = d // N_FOURIER_GROUPS
    w1, twr, twi, gmat, cc, sc = _dft_tables(s, gd)
    w1, gmat, cc, sc = (t.astype(BF16) for t in (w1, gmat, cc, sc))

    for i in range(depth):
        sh1, sc1, g1, sh2, sc2, g2 = [mod[i, :, k] for k in range(N_ADA)]
        j = i // 2
        if i % 2 == 0:
            qscale = jnp.concatenate([jnp.full((d,), LOG2E * DIFF_HEAD_DIM ** -0.5, F32),
                                      jnp.ones((2 * d,), F32)])
            w_in = (attn_w_in[j] * qscale).astype(BF16)
            qkv = _norm_mod_matmul(x, norm_mix[i].reshape(1, d), sh1, sc1, w_in, tm=512)
            lam_init = 0.8 - 0.6 * math.exp(-0.3 * i)
            o = _diff_attention(qkv, slopes,
                                attn_lam_q1[j].reshape(1, -1), attn_lam_k1[j].reshape(1, -1),
                                attn_lam_q2[j].reshape(1, -1), attn_lam_k2[j].reshape(1, -1),
                                attn_subln[j].reshape(-1, 1), lam_init)
            x = _proj_residual(x, o, attn_w_out[j].astype(BF16), g1, tm=512)
        else:
            u = _norm_mod_matmul(x, norm_mix[i].reshape(1, d), sh1, sc1,
                                 fourier_w_in[j].astype(BF16), tm=512)
            y = _fft_stage1(u.reshape(b, FFT_N1, FFT_N2 * d), w1, twr, twi, d)
            x = _fft2_mix(y.reshape(b, 2, FFT_N1, FFT_N2, d), gmat, x.reshape(b, FFT_N2, FFT_N1, d),
                          cc, sc, fourier_w_out[j].astype(BF16), g1).reshape(b, s, d)

        x = moe_layer(x, norm_ffn[i], sh2, sc2, g2, router_group_w[i], router_group_b[i],
                      router_expert_w[i], router_expert_b[i], expert_w_gate, expert_w_up,
                      expert_w_down, i, norm_final, i == depth - 1)
    return x
```

```python
import functools
import math

import numpy as np
import jax
import jax.numpy as jnp
from jax import lax
from jax.experimental import pallas as pl
from jax.experimental.pallas import tpu as pltpu

F32 = jnp.float32
BF16 = jnp.bfloat16

EPS = 1e-6
N_DIFF_HEADS = 8
DIFF_HEAD_DIM = 64
DIFF_V_DIM = 128
N_FOURIER_GROUPS = 4
N_EXPERT_GROUPS = 4
EXPERTS_PER_GROUP = 4
N_EXPERTS = 16
N_ADA = 6

LANES = 128
VMEM_LIMIT_BYTES = 56 * 1024 * 1024

FFT_N1 = 128
FFT_N2 = 64
FFT_K1_PER_STEP = 8


def _cparams(n_axes):
    return pltpu.CompilerParams(
        dimension_semantics=("arbitrary",) * n_axes,
        vmem_limit_bytes=VMEM_LIMIT_BYTES,
    )


def _rms_mod(x, g, shift, scale):
    y = x * lax.rsqrt(jnp.mean(x * x, axis=-1, keepdims=True) + EPS)
    return (y * g) * (1.0 + scale) + shift


def _ada_kernel(c_ref, w_ref, b_ref, o_ref):
    c = c_ref[...]
    cond = c / (1.0 + jnp.exp(-c))
    o_ref[0] = jnp.dot(cond, w_ref[0], precision=lax.Precision.HIGHEST,
                       preferred_element_type=F32) + b_ref[0]


def _ada(c8, ada_w, ada_b3):
    depth, d, n = ada_w.shape
    tn = 1536
    return pl.pallas_call(
        _ada_kernel,
        grid=(depth, n // tn),
        in_specs=[
            pl.BlockSpec((8, d), lambda i, j: (0, 0)),
            pl.BlockSpec((1, d, tn), lambda i, j: (i, 0, j)),
            pl.BlockSpec((1, 1, tn), lambda i, j: (i, 0, j)),
        ],
        out_specs=pl.BlockSpec((1, 8, tn), lambda i, j: (i, 0, j)),
        out_shape=jax.ShapeDtypeStruct((depth, 8, n), F32),
        compiler_params=_cparams(2),
        name="ada_mod",
    )(c8, ada_w, ada_b3)


def _nmm_kernel(x_ref, g_ref, sh_ref, sc_ref, w_ref, o_ref):
    h = _rms_mod(x_ref[0], g_ref[...], sh_ref[0], sc_ref[0])
    o_ref[0] = jnp.dot(h.astype(BF16), w_ref[...],
                       preferred_element_type=F32).astype(o_ref.dtype)


def _norm_mod_matmul(x, g, shift, scale, w_bf16, tm):
    b, s, d = x.shape
    n = w_bf16.shape[1]
    return pl.pallas_call(
        _nmm_kernel,
        grid=(b, s // tm),
        in_specs=[
            pl.BlockSpec((1, tm, d), lambda i, j: (i, j, 0)),
            pl.BlockSpec((1, d), lambda i, j: (0, 0)),
            pl.BlockSpec((1, 1, d), lambda i, j: (i, 0, 0)),
            pl.BlockSpec((1, 1, d), lambda i, j: (i, 0, 0)),
            pl.BlockSpec((d, n), lambda i, j: (0, 0)),
        ],
        out_specs=pl.BlockSpec((1, tm, n), lambda i, j: (i, j, 0)),
        out_shape=jax.ShapeDtypeStruct((b, s, n), BF16),
        compiler_params=_cparams(2),
        name="norm_mod_matmul",
    )(x, g, shift, scale, w_bf16)


ATT_TQ = 256
ATT_TK = 512
ATT_PAIR = 2 * ATT_TK
ATT_NQP = ATT_PAIR // ATT_TQ
ATT_QSUB = 4
ATT_TRIPS_FAST = (4, 2)
ATT_TRIPS_SAFE = (2,)
LOG2E = 1.4426950408889634
ATT_SKIP_BITS = 40.0
ATT_FAST_MAX_U = 36.0


def _attn_bounds_kernel(q_ref, k_ref, slope_ref, lo_ref, hi_ref, fast_ref, ub_ref, *, seq):
    tq = ATT_TQ
    nq = seq // tq
    dd = 2 * DIFF_HEAD_DIM
    r = lax.broadcasted_iota(jnp.int32, (dd, dd), 0)
    c = lax.broadcasted_iota(jnp.int32, (dd, dd), 1)
    halves = jnp.where(c == r // DIFF_HEAD_DIM, 1.0, 0.0).astype(BF16)

    def half_norm2(ref):
        x = ref[0].astype(F32)
        return jnp.dot((x * x).astype(BF16), halves, preferred_element_type=F32)

    qn2 = jnp.max(jnp.max(half_norm2(q_ref).reshape(nq, tq, dd), axis=1), axis=1, keepdims=True)
    kn2 = jnp.max(jnp.max(half_norm2(k_ref), axis=0, keepdims=True), axis=1, keepdims=True)
    ub = jnp.sqrt(qn2 * kn2) * (1.0 + 2.0 ** -6)
    slope = slope_ref[0] * LOG2E
    dist = (2.0 * ub + (ATT_SKIP_BITS + math.log2(seq))) / slope
    dist = jnp.ceil(jnp.minimum(dist, float(seq))).astype(jnp.int32)
    q0 = lax.broadcasted_iota(jnp.int32, (nq, 1), 0) * tq
    jmin = jnp.maximum(q0 - dist + 1, 0)
    jmax = jnp.minimum(q0 + (tq - 1) + dist - 1, seq - 1)
    shift = jnp.full((nq, 1), int(math.log2(ATT_TK)), jnp.int32)
    lo_ref[0, 0] = lax.shift_right_logical(jmin, shift)
    hi_ref[0, 0] = lax.shift_right_logical(jmax, shift) + 1
    fast_ref[0, 0] = (ub <= ATT_FAST_MAX_U).astype(jnp.int32)
    ub_ref[0, 0] = ub


def _attn_bounds(qkv, slopes):
    b, s, _ = qkv.shape
    h, dv = N_DIFF_HEADS, DIFF_V_DIM
    nq = s // ATT_TQ
    kern = functools.partial(_attn_bounds_kernel, seq=s)
    oblk = pl.BlockSpec((1, 1, nq, 1), lambda bi, hi: (bi, hi, 0, 0))
    ishape = jax.ShapeDtypeStruct((b, h, nq, 1), jnp.int32)
    return pl.pallas_call(
        kern,
        grid=(b, h),
        in_specs=[
            pl.BlockSpec((1, s, dv), lambda bi, hi: (bi, 0, hi)),
            pl.BlockSpec((1, s, dv), lambda bi, hi: (bi, 0, h + hi)),
            pl.BlockSpec((1, 1, 1), lambda bi, hi: (hi, 0, 0)),
        ],
        out_specs=[oblk, oblk, oblk, oblk],
        out_shape=[ishape, ishape, ishape, jax.ShapeDtypeStruct((b, h, nq, 1), F32)],
        compiler_params=_cparams(2),
        name="attn_bounds",
    )(qkv, qkv, slopes)


def _attn_kernel(lo_ref, hi_ref, fast_ref, q_ref, k_ref, v_ref, ub_ref, slope_ref,
                 lq1_ref, lk1_ref, lq2_ref, lk2_ref, sub_ref,
                 o_ref, vt_ref, bias_ref, u_ref, m_ref, l_ref, acc_ref, *, lam_init, seq):
    tq, tk, pair, nqp = ATT_TQ, ATT_TK, ATT_PAIR, ATT_NQP
    dh = DIFF_HEAD_DIM
    step = pl.program_id(2)
    slope = slope_ref[0] * LOG2E

    @pl.when(step == 0)
    def _():
        for c in range(seq // 512):
            vt_ref[:, c * 512:(c + 1) * 512] = v_ref[0, c * 512:(c + 1) * 512, :].T
        for s in range(2):
            rows = slice(s * tk, (s + 1) * tk)
            r = lax.broadcasted_iota(jnp.int32, (tk, 2 * tq), 0) + s * tk
            c = lax.broadcasted_iota(jnp.int32, (tk, 2 * tq), 1)
            c = jnp.where(c >= tq, c - tq, c)
            rc = (r - c).astype(F32)
            bias_ref[0, rows, :] = slope * rc
            bias_ref[1, rows, :] = -(slope * rc)
            for w in range(nqp):
                bias_ref[2 + w, rows, :] = -(slope * jnp.abs(rc - float(w * tq)))

    lam = (jnp.exp(jnp.sum(lq1_ref[...] * lk1_ref[...], axis=-1, keepdims=True))
           - jnp.exp(jnp.sum(lq2_ref[...] * lk2_ref[...], axis=-1, keepdims=True))
           + lam_init)

    def one_tile(t):
        qi = step * ATT_QSUB + t
        tile = (pl.program_id(0) * N_DIFF_HEADS + pl.program_id(1)) * (seq // tq) + qi
        lo, hi = lo_ref[tile], hi_ref[tile]
        lt_ref, at_ref = l_ref.at[t], acc_ref.at[t]

        qt = q_ref[0, t * tq:(t + 1) * tq, :].T
        z = jnp.zeros((dh, tq), BF16)
        qbd = jnp.concatenate([jnp.concatenate([qt[:dh], z], axis=1),
                               jnp.concatenate([z, qt[dh:]], axis=1)], axis=0)

        lt_ref[...] = jnp.zeros(lt_ref.shape, F32)
        at_ref[...] = jnp.zeros(at_ref.shape, F32)

        kd = qi // nqp
        w = qi % nqp

        def mode(n):
            kp = n // 2
            delta = (kp * pair - qi * tq).astype(F32)
            idx = jnp.where(kp < kd, 0, jnp.where(kp > kd, 1, 2 + w))
            sgn = jnp.where(kp < kd, delta, jnp.where(kp > kd, -delta, 0.0))
            return idx, pl.multiple_of((n % 2) * tk, tk), slope * sgn

        def scores(n):
            idx, row0, _ = mode(n)
            k0 = pl.multiple_of(n * tk, tk)
            kblk = k_ref[0, pl.ds(k0, tk), :]
            u = jnp.dot(kblk, qbd, preferred_element_type=F32)
            return u + bias_ref[idx, pl.ds(row0, tk), :]

        def values(n):
            return vt_ref[:, pl.ds(pl.multiple_of(n * tk, tk), tk)]

        def pipelined(qk_phase, sm_phase, carry0, trips):
            def body(n, carry_a):
                carry_b = qk_phase(n + 1, 1)
                sm_phase(n, 0, carry_a)
                carry_a_next = qk_phase(jnp.minimum(n + 2, hi - 1), 0)
                sm_phase(n + 1, 1, carry_b)
                return carry_a_next

            start, carry = lo, carry0
            for npt in trips:
                ntrip = (hi - start) // (2 * npt)

                def trip(i, c, start=start, npt=npt):
                    for j in range(npt):
                        c = body(start + 2 * (npt * i + j), c)
                    return c

                carry = lax.fori_loop(0, ntrip, trip, carry)
                start = start + 2 * npt * ntrip
            ndouble = (hi - start) // 2
            carry = lax.fori_loop(0, ndouble, lambda i, c, start=start: body(start + 2 * i, c), carry)
            last = start + 2 * ndouble

            @pl.when(last < hi)
            def _():
                sm_phase(last, 0, carry)

        def fixed_reference():
            ub = ub_ref[t]

            def qk_phase(n, slot):
                u_ref[slot] = scores(n)
                return 0

            def sm_phase(n, slot, _):
                cst = mode(n)[2]
                p = jnp.exp2(u_ref[slot] + (cst - ub))
                lt_ref[...] += jnp.sum(p, axis=0, keepdims=True)
                at_ref[...] += jnp.dot(values(n), p.astype(BF16), preferred_element_type=F32)

            pipelined(qk_phase, sm_phase, 0, ATT_TRIPS_FAST)

        def running_max():
            m_ref[...] = jnp.full(m_ref.shape, -jnp.inf, F32)

            def qk_phase(n, slot):
                v = scores(n)
                u_ref[slot] = v
                return jnp.max(v, axis=0, keepdims=True)

            def sm_phase(n, slot, mx):
                cst = mode(n)[2]
                m_old = m_ref[...]
                m_new = jnp.maximum(m_old, mx + cst)
                alpha = jnp.exp2(m_old - m_new)
                p = jnp.exp2(u_ref[slot] - (m_new - cst))
                lt_ref[...] = alpha * lt_ref[...] + jnp.sum(p, axis=0, keepdims=True)
                m_ref[...] = m_new
                at_ref[...] = alpha * at_ref[...] + jnp.dot(values(n), p.astype(BF16),
                                                            preferred_element_type=F32)

            pipelined(qk_phase, sm_phase, jnp.max(u_ref[0], axis=0, keepdims=True), ATT_TRIPS_SAFE)

        u_ref[0] = scores(lo)
        lax.cond(fast_ref[tile] == 1, fixed_reference, running_max)

        acc = at_ref[...]
        l = lt_ref[...]
        ot = acc[:, :tq] / l[:, :tq] - lam * (acc[:, tq:] / l[:, tq:])
        ms = jnp.mean(ot * ot, axis=0, keepdims=True)
        ot = ot * lax.rsqrt(ms + EPS) * sub_ref[...] * (1.0 - lam_init)
        o_ref[0, t * tq:(t + 1) * tq, :] = ot.T.astype(o_ref.dtype)

    for t in range(ATT_QSUB):
        one_tile(t)


def _diff_attention(qkv, slopes, lq1, lk1, lq2, lk2, subln_col, lam_init):
    b, s, _ = qkv.shape
    h, dv = N_DIFF_HEADS, DIFF_V_DIM
    tq, tk, qsub = ATT_TQ, ATT_TK, ATT_QSUB
    nq = s // tq
    assert s % ATT_PAIR == 0 and nq % qsub == 0
    nsteps = nq // qsub
    lo, hi, fast, ub = _attn_bounds(qkv, slopes)
    kern = functools.partial(_attn_kernel, lam_init=lam_init, seq=s)
    vec = pl.BlockSpec((1, DIFF_HEAD_DIM), lambda bi, hi, qi, *_: (0, 0))
    grid_spec = pltpu.PrefetchScalarGridSpec(
        num_scalar_prefetch=3,
        grid=(b, h, nsteps),
        in_specs=[
            pl.BlockSpec((1, qsub * tq, dv), lambda bi, hi, qi, *_: (bi, qi, hi)),
            pl.BlockSpec((1, s, dv), lambda bi, hi, qi, *_: (bi, 0, h + hi)),
            pl.BlockSpec((1, s, dv), lambda bi, hi, qi, *_: (bi, 0, 2 * h + hi)),
            pl.BlockSpec((qsub, 1, 1), lambda bi, hi, qi, *_: ((bi * h + hi) * nsteps + qi, 0, 0)),
            pl.BlockSpec((1, 1, 1), lambda bi, hi, qi, *_: (hi, 0, 0)),
            vec, vec, vec, vec,
            pl.BlockSpec((dv, 1), lambda bi, hi, qi, *_: (0, 0)),
        ],
        out_specs=pl.BlockSpec((1, qsub * tq, dv), lambda bi, hi, qi, *_: (bi, qi, hi)),
        scratch_shapes=[
            pltpu.VMEM((dv, s), BF16),
            pltpu.VMEM((2 + ATT_NQP, ATT_PAIR, 2 * tq), F32),
            pltpu.VMEM((2, tk, 2 * tq), F32),
            pltpu.VMEM((1, 2 * tq), F32),
            pltpu.VMEM((qsub, 1, 2 * tq), F32),
            pltpu.VMEM((qsub, dv, 2 * tq), F32),
        ],
    )
    return pl.pallas_call(
        kern,
        grid_spec=grid_spec,
        out_shape=jax.ShapeDtypeStruct((b, s, h * dv), BF16),
        compiler_params=_cparams(3),
        name="diff_attention",
    )(lo.reshape(-1), hi.reshape(-1), fast.reshape(-1),
      qkv, qkv, qkv, ub.reshape(-1, 1, 1), slopes, lq1, lk1, lq2, lk2, subln_col)


def _proj_res_kernel(x_ref, a_ref, w_ref, gate_ref, o_ref):
    y = jnp.dot(a_ref[0], w_ref[...], preferred_element_type=F32)
    o_ref[0] = x_ref[0] + gate_ref[0] * y


def _proj_residual(x, a, w_bf16, gate, tm):
    b, s, d = x.shape
    kdim = a.shape[-1]
    return pl.pallas_call(
        _proj_res_kernel,
        grid=(b, s // tm),
        in_specs=[
            pl.BlockSpec((1, tm, d), lambda i, j: (i, j, 0)),
            pl.BlockSpec((1, tm, kdim), lambda i, j: (i, j, 0)),
            pl.BlockSpec((kdim, d), lambda i, j: (0, 0)),
            pl.BlockSpec((1, 1, d), lambda i, j: (i, 0, 0)),
        ],
        out_specs=pl.BlockSpec((1, tm, d), lambda i, j: (i, j, 0)),
        out_shape=jax.ShapeDtypeStruct((b, s, d), F32),
        compiler_params=_cparams(2),
        name="proj_residual",
    )(x, a, w_bf16, gate)


def _routing_gates_t(lt):
    ng, ne = N_EXPERT_GROUPS, EXPERTS_PER_GROUP
    gl = [lt[g:g + 1, :] for g in range(ng)]
    gmax = functools.reduce(jnp.maximum, gl)
    den = functools.reduce(lambda a, b: a + b, [jnp.exp(v - gmax) for v in gl])
    g_w = 1.0 / den
    sel = []
    taken = jnp.zeros(gmax.shape, jnp.bool_)
    for g in range(ng):
        s = jnp.logical_and(gl[g] == gmax, jnp.logical_not(taken))
        taken = jnp.logical_or(taken, s)
        sel.append(s)
    ev = []
    for e in range(ne):
        v = jnp.zeros_like(gmax)
        for g in range(ng):
            row = ng + ne * g + e
            v = jnp.where(sel[g], lt[row:row + 1, :], v)
        ev.append(v)
    top1 = functools.reduce(jnp.maximum, ev)
    is1 = []
    taken = jnp.zeros(gmax.shape, jnp.bool_)
    for e in range(ne):
        s = jnp.logical_and(ev[e] == top1, jnp.logical_not(taken))
        taken = jnp.logical_or(taken, s)
        is1.append(s)
    rest = [jnp.where(is1[e], -jnp.inf, ev[e]) for e in range(ne)]
    top2 = functools.reduce(jnp.maximum, rest)
    is2 = []
    taken = jnp.zeros(gmax.shape, jnp.bool_)
    for e in range(ne):
        s = jnp.logical_and(jnp.logical_and(rest[e] == top2, jnp.logical_not(is1[e])),
                            jnp.logical_not(taken))
        taken = jnp.logical_or(taken, s)
        is2.append(s)
    ex = jnp.exp(top2 - top1)
    w1 = 1.0 / (1.0 + ex)
    w2 = ex / (1.0 + ex)
    ew = [jnp.where(is1[e], w1, jnp.where(is2[e], w2, 0.0)) for e in range(ne)]
    rows = [jnp.where(sel[g], g_w * ew[e], 0.0) for g in range(ng) for e in range(ne)]
    return jnp.concatenate(rows, axis=0), sel


MOE_TM = 1024
MOE_CH = 128
MOE_NB = 11
MOE_POS_LANE = 3 * N_EXPERTS


def _route_sort_kernel(x_ref, g_ref, sh_ref, sc_ref, wr_ref, br_ref, tri_ref,
                       hs_ref, gs_ref, meta_ref, gid_ref):
    ng, ch, nb = N_EXPERT_GROUPS, MOE_CH, MOE_NB
    tm = x_ref.shape[0]
    h32 = _rms_mod(x_ref[...], g_ref[...], sh_ref[0], sc_ref[0])
    hb = h32.astype(BF16)
    hl = (h32 - hb.astype(F32)).astype(BF16)
    wr = wr_ref[...]
    wb = wr.astype(BF16)
    wl = (wr - wb.astype(F32)).astype(BF16)
    nt = (((1,), (1,)), ((), ()))
    l2 = lax.dot_general(jnp.concatenate([wb, wl], axis=0), hb, nt, preferred_element_type=F32)
    lt = (l2[:LANES] + l2[LANES:] + lax.dot_general(wb, hl, nt, preferred_element_type=F32)
          + br_ref[...])
    gates_t, sel = _routing_gates_t(lt)
    oh = jnp.concatenate([m.astype(F32) for m in sel] + [jnp.zeros((16 - ng, tm), F32)], axis=0)
    cnt = jnp.dot(oh.astype(BF16), tri_ref[...], preferred_element_type=F32)
    rank = jnp.sum(oh * cnt, axis=0, keepdims=True)
    n = jnp.sum(oh, axis=1, keepdims=True)
    nch = jnp.floor((n + float(ch - 1)) * (1.0 / ch))
    first = [jnp.zeros((1, 1), F32)]
    for g in range(ng):
        first.append(first[-1] + nch[g:g + 1, :])
    seg = functools.reduce(lambda a, b: a + b,
                           [oh[g:g + 1, :] * (first[g] * float(ch)) for g in range(ng)])
    pos = seg + rank
    rows = lax.broadcasted_iota(jnp.int32, (nb * ch, tm), 0)
    perm = jnp.where(rows == pos.astype(jnp.int32), 1.0, 0.0).astype(BF16)
    hs_ref[0] = jnp.dot(perm, hb, preferred_element_type=F32).astype(BF16)
    hi = gates_t.astype(BF16).astype(F32)
    r1 = gates_t - hi
    mid = r1.astype(BF16).astype(F32)
    lo = r1 - mid
    meta_t = jnp.concatenate([hi, mid, lo, pos, jnp.zeros((LANES - MOE_POS_LANE - 1, tm), F32)], axis=0)
    meta = meta_t.T
    meta_ref[...] = meta
    gs_ref[0] = jnp.dot(perm, meta.astype(BF16), preferred_element_type=F32)
    lane = lax.broadcasted_iota(jnp.int32, (1, LANES), 1).astype(F32)
    gid = functools.reduce(lambda a, b: a + b,
                           [jnp.where(lane >= first[g + 1], 1.0, 0.0) for g in range(ng)])
    gid_ref[0] = gid.astype(jnp.int32)


def _route_sort(x2, g, shift, scale, wr, br, tri, tiles_per_batch):
    t, d = x2.shape
    tm, rows = MOE_TM, MOE_NB * MOE_CH
    nt = t // tm
    bvec = lambda i: (i // tiles_per_batch, 0, 0)
    return pl.pallas_call(
        _route_sort_kernel,
        grid=(nt,),
        in_specs=[
            pl.BlockSpec((tm, d), lambda i: (i, 0)),
            pl.BlockSpec((1, d), lambda i: (0, 0)),
            pl.BlockSpec((1, 1, d), bvec),
            pl.BlockSpec((1, 1, d), bvec),
            pl.BlockSpec((LANES, d), lambda i: (0, 0)),
            pl.BlockSpec((LANES, 1), lambda i: (0, 0)),
            pl.BlockSpec((tm, tm), lambda i: (0, 0)),
        ],
        out_specs=[
            pl.BlockSpec((1, rows, d), lambda i: (i, 0, 0)),
            pl.BlockSpec((1, rows, LANES), lambda i: (i, 0, 0)),
            pl.BlockSpec((tm, LANES), lambda i: (i, 0)),
            pl.BlockSpec((1, 1, LANES), lambda i: (i, 0, 0)),
        ],
        out_shape=[
            jax.ShapeDtypeStruct((nt, rows, d), BF16),
            jax.ShapeDtypeStruct((nt, rows, LANES), F32),
            jax.ShapeDtypeStruct((t, LANES), F32),
            jax.ShapeDtypeStruct((nt, 1, LANES), jnp.int32),
        ],
        compiler_params=_cparams(1),
        name="moe_route_sort",
    )(x2, g, shift, scale, wr, br, tri)


def _experts_kernel(order_ref, grp_ref, first_ref, hs_ref, gs_ref, wg_ref, wu_ref, wd_ref, ys_ref,
                    wgb_ref, wub_ref, wdb_ref):
    ne = EXPERTS_PER_GROUP
    step = pl.program_id(0)
    grp = grp_ref[step]

    @pl.when(first_ref[step] == 1)
    def _():
        wgb_ref[...] = wg_ref[0].astype(BF16)
        wub_ref[...] = wu_ref[0].astype(BF16)
        wdb_ref[...] = wd_ref[0].astype(BF16)

    @pl.when(grp < N_EXPERT_GROUPS)
    def _():
        h = hs_ref[0]
        gs = gs_ref[0]
        lane = lax.broadcasted_iota(jnp.int32, gs.shape, 1)
        hid = []
        for e in range(ne):
            idx = grp * ne + e
            pick = jnp.logical_or(jnp.logical_or(lane == idx, lane == idx + N_EXPERTS),
                                  lane == idx + 2 * N_EXPERTS)
            gate = jnp.sum(jnp.where(pick, gs, 0.0), axis=1, keepdims=True)
            a = jnp.dot(h, wgb_ref[e], preferred_element_type=F32)
            u = jnp.dot(h, wub_ref[e], preferred_element_type=F32)
            hid.append(((a / (1.0 + jnp.exp(-a))) * u * gate).astype(BF16))
        hid = jnp.concatenate(hid, axis=1)
        ys_ref[0] = jnp.dot(hid, wdb_ref[...], preferred_element_type=F32).astype(ys_ref.dtype)

    @pl.when(grp >= N_EXPERT_GROUPS)
    def _():
        ys_ref[0] = jnp.zeros(ys_ref.shape[1:], ys_ref.dtype)


def _experts(order, grp, first, hs, gs, wg4, wu4, wd4, layer):
    nchunks, ch, d = hs.shape
    ne, f = wg4.shape[1], wg4.shape[-1]
    ng = N_EXPERT_GROUPS
    wsel = lambda s, order, grp, first: (layer * ng + jnp.minimum(grp[s], ng - 1), 0, 0, 0)
    wsel3 = lambda s, order, grp, first: (layer * ng + jnp.minimum(grp[s], ng - 1), 0, 0)
    chunk = lambda s, order, grp, first: (order[s], 0, 0)
    once = pl.Buffered(1)
    grid_spec = pltpu.PrefetchScalarGridSpec(
        num_scalar_prefetch=3,
        grid=(nchunks,),
        in_specs=[
            pl.BlockSpec((1, ch, d), chunk),
            pl.BlockSpec((1, ch, LANES), chunk),
            pl.BlockSpec((1, ne, d, f), wsel, pipeline_mode=once),
            pl.BlockSpec((1, ne, d, f), wsel, pipeline_mode=once),
            pl.BlockSpec((1, ne * f, d), wsel3, pipeline_mode=once),
        ],
        out_specs=pl.BlockSpec((1, ch, d), chunk),
        scratch_shapes=[
            pltpu.VMEM((ne, d, f), BF16),
            pltpu.VMEM((ne, d, f), BF16),
            pltpu.VMEM((ne * f, d), BF16),
        ],
    )
    return pl.pallas_call(
        _experts_kernel,
        grid_spec=grid_spec,
        out_shape=jax.ShapeDtypeStruct((nchunks, ch, d), BF16),
        compiler_params=_cparams(1),
        name="moe_experts",
    )(order, grp, first, hs, gs, wg4, wu4, wd4)


def _unsort_kernel(x_ref, ys_ref, meta_ref, gate_ref, gf_ref, o_ref, *, final_norm):
    rows = ys_ref.shape[1]
    tm = x_ref.shape[0]
    pos = meta_ref[:, MOE_POS_LANE:MOE_POS_LANE + 1].astype(jnp.int32)
    cols = lax.broadcasted_iota(jnp.int32, (tm, rows), 1)
    perm_t = jnp.where(cols == pos, 1.0, 0.0).astype(BF16)
    y = jnp.dot(perm_t, ys_ref[0], preferred_element_type=F32)
    y = x_ref[...] + gate_ref[0] * y
    if final_norm:
        y = y * lax.rsqrt(jnp.mean(y * y, axis=-1, keepdims=True) + EPS) * gf_ref[...]
    o_ref[...] = y


def _unsort_residual(x2, ys, meta, gate, gfinal, final_norm, tiles_per_batch):
    t, d = x2.shape
    tm = MOE_TM
    nt, rows, _ = ys.shape
    kern = functools.partial(_unsort_kernel, final_norm=final_norm)
    return pl.pallas_call(
        kern,
        grid=(nt,),
        in_specs=[
            pl.BlockSpec((tm, d), lambda i: (i, 0)),
            pl.BlockSpec((1, rows, d), lambda i: (i, 0, 0)),
            pl.BlockSpec((tm, LANES), lambda i: (i, 0)),
            pl.BlockSpec((1, 1, d), lambda i: (i // tiles_per_batch, 0, 0)),
            pl.BlockSpec((1, d), lambda i: (0, 0)),
        ],
        out_specs=pl.BlockSpec((tm, d), lambda i: (i, 0)),
        out_shape=jax.ShapeDtypeStruct((t, d), F32),
        compiler_params=_cparams(1),
        name="moe_unsort_residual",
    )(x2, ys, meta, gate, gfinal)


def moe_layer(x, g, sh2, sc2, g2, w_rg, b_rg, w_re, b_re, w_gate, w_up, w_down, layer, gfinal, final_norm):
    b, s, d = x.shape
    ng, ne = N_EXPERT_GROUPS, EXPERTS_PER_GROUP
    f = w_gate.shape[-1]
    assert s % MOE_TM == 0
    nr = ng + N_EXPERTS
    wr = jnp.concatenate([w_rg.T, w_re.reshape(d, N_EXPERTS).T,
                          jnp.zeros((LANES - nr, d), F32)], axis=0)
    br = jnp.concatenate([b_rg, b_re.reshape(N_EXPERTS),
                          jnp.zeros((LANES - nr,), F32)]).reshape(LANES, 1)
    ti = np.arange(MOE_TM)
    tri = jnp.asarray(ti[:, None] < ti[None, :], F32).astype(BF16)
    x2 = x.reshape(b * s, d)
    tpb = s // MOE_TM
    hs, gs, meta, gid = _route_sort(x2, g.reshape(1, d), sh2, sc2, wr, br, tri, tpb)
    nt = hs.shape[0]
    gid_flat = gid[:, 0, :MOE_NB].reshape(-1)
    order = jnp.argsort(gid_flat, stable=True).astype(jnp.int32)
    grp = gid_flat[order]
    first = jnp.concatenate([jnp.ones((1,), jnp.int32), (grp[1:] != grp[:-1]).astype(jnp.int32)])
    nl = w_gate.shape[0]
    ys = _experts(order, grp, first,
                  hs.reshape(nt * MOE_NB, MOE_CH, d), gs.reshape(nt * MOE_NB, MOE_CH, LANES),
                  w_gate.reshape(nl * ng, ne, d, f), w_up.reshape(nl * ng, ne, d, f),
                  w_down.reshape(nl * ng, ne * f, d), layer)
    out = _unsort_residual(x2, ys.reshape(nt, MOE_NB * MOE_CH, d), meta, g2,
                           gfinal.reshape(1, d), final_norm, tpb)
    return out.reshape(b, s, d)


def _fft1_kernel(u_ref, w_ref, twr_ref, twi_ref, o_ref, *, n_sub, width):
    n1 = FFT_N1
    y = jnp.dot(w_ref[...], u_ref[0], preferred_element_type=F32)
    for j in range(n_sub):
        cols = slice(j * width, (j + 1) * width)
        yr, yi = y[:n1, cols], y[n1:, cols]
        twr, twi = twr_ref[j], twi_ref[j]
        o_ref[0, :n1, cols] = (yr * twr - yi * twi).astype(o_ref.dtype)
        o_ref[0, n1:, cols] = (yr * twi + yi * twr).astype(o_ref.dtype)


def _fft_stage1(u3, w1, twr, twi, width):
    b, n1, cols = u3.shape
    n_sub = 8
    tn = n_sub * width
    kern = functools.partial(_fft1_kernel, n_sub=n_sub, width=width)
    return pl.pallas_call(
        kern,
        grid=(b, cols // tn),
        in_specs=[
            pl.BlockSpec((1, n1, tn), lambda i, j: (i, 0, j)),
            pl.BlockSpec((2 * n1, n1), lambda i, j: (0, 0)),
            pl.BlockSpec((n_sub, n1, 1), lambda i, j: (j, 0, 0)),
            pl.BlockSpec((n_sub, n1, 1), lambda i, j: (j, 0, 0)),
        ],
        out_specs=pl.BlockSpec((1, 2 * n1, tn), lambda i, j: (i, 0, j)),
        out_shape=jax.ShapeDtypeStruct((b, 2 * n1, cols), BF16),
        compiler_params=_cparams(2),
        name="fft_stage1",
    )(u3, w1, twr, twi)


def _fft2_mix_kernel(y_ref, g_ref, x_ref, cc_ref, sc_ref, w_ref, gate_ref, o_ref):
    kp, n2 = FFT_K1_PER_STEP, FFT_N2
    width = y_ref.shape[-1]
    gd = cc_ref.shape[0]
    half = kp * n2
    y = y_ref[0].reshape(2 * half, width)
    r = jnp.dot(g_ref[...], y, preferred_element_type=F32)
    ar, ai = r[:half].astype(BF16), r[half:].astype(BF16)
    parts = []
    for g in range(N_FOURIER_GROUPS):
        cols = slice(g * gd, (g + 1) * gd)
        parts.append(jnp.dot(ar[:, cols], cc_ref[...], preferred_element_type=F32)
                     + jnp.dot(ai[:, cols], sc_ref[...], preferred_element_type=F32))
    f = jnp.concatenate(parts, axis=1).astype(BF16)
    yv = jnp.dot(f, w_ref[...], preferred_element_type=F32)
    out = x_ref[0].reshape(half, width) + gate_ref[0] * yv
    o_ref[0] = out.reshape(n2, kp, width)


def _fft2_mix(y5, gmat, x4, cc, sc, w_bf16, gate):
    b, _, n1, n2, width = y5.shape
    kp = FFT_K1_PER_STEP
    gd = cc.shape[0]
    return pl.pallas_call(
        _fft2_mix_kernel,
        grid=(b, n1 // kp),
        in_specs=[
            pl.BlockSpec((1, 2, kp, n2, width), lambda i, j: (i, 0, j, 0, 0)),
            pl.BlockSpec((2 * kp * n2, 2 * kp * n2), lambda i, j: (0, 0)),
            pl.BlockSpec((1, n2, kp, width), lambda i, j: (i, 0, j, 0)),
            pl.BlockSpec((gd, gd), lambda i, j: (0, 0)),
            pl.BlockSpec((gd, gd), lambda i, j: (0, 0)),
            pl.BlockSpec((width, width), lambda i, j: (0, 0)),
            pl.BlockSpec((1, 1, width), lambda i, j: (i, 0, 0)),
        ],
        out_specs=pl.BlockSpec((1, n2, kp, width), lambda i, j: (i, 0, j, 0)),
        out_shape=jax.ShapeDtypeStruct((b, n2, n1, width), F32),
        compiler_params=_cparams(2),
        name="fft_stage2_mix",
    )(y5, gmat, x4, cc, sc, w_bf16, gate)


def _dft_tables(seq, gd):
    n1, n2, kp = FFT_N1, FFT_N2, FFT_K1_PER_STEP

    def cs(num, den):
        ang = (2.0 * math.pi / den) * (num % den).astype(np.float64)
        return np.cos(ang), np.sin(ang)

    i1 = np.arange(n1)
    c1, s1 = cs(np.outer(i1, i1), n1)
    w1 = np.concatenate([c1, -s1], axis=0)
    ct, st = cs(np.outer(np.arange(n2), i1), seq)
    twr, twi = ct[:, :, None], -st[:, :, None]
    i2 = np.arange(n2)
    c2, s2 = cs(np.outer(i2, i2), n2)
    m3 = np.block([[c2, s2], [-s2, c2]])
    m3 = m3.reshape(2, n2, 2, n2)
    gmat = np.zeros((2, n2, kp, 2, kp, n2))
    for j in range(kp):
        gmat[:, :, j, :, j, :] = m3
    gmat = gmat.reshape(2 * n2 * kp, 2 * kp * n2)
    ic = np.arange(gd)
    cc, sc = cs(np.outer(ic, ic), gd)
    norm = 1.0 / math.sqrt(seq * gd)
    return tuple(jnp.asarray(t, F32) for t in (w1, twr, twi, gmat, cc * norm, sc * norm))


def kernel(x, c, ada_w, ada_b, norm_mix, norm_ffn, attn_w_in, attn_lam_q1, attn_lam_k1, attn_lam_q2, attn_lam_k2, attn_subln, attn_w_out, fourier_w_in, fourier_w_out, router_group_w, router_group_b, router_expert_w, router_expert_b, expert_w_gate, expert_w_up, expert_w_down, norm_final):
    b, s, d = x.shape
    depth = ada_w.shape[0]
    assert s == FFT_N1 * FFT_N2 and depth == 2

    c8 = jnp.zeros((8, d), F32).at[:b].set(c)
    mod = _ada(c8, ada_w, ada_b.reshape(depth, 1, N_ADA * d))[:, :b]
    mod = mod.reshape(depth, b, N_ADA, 1, d)

    slopes = (2.0 ** (-8.0 * jnp.arange(1, N_DIFF_HEADS + 1, dtype=F32) / N_DIFF_HEADS)
              ).reshape(N_DIFF_HEADS, 1, 1)
    gd = d // N_FOURIER_GROUPS
    w1, twr, twi, gmat, cc, sc = _dft_tables(s, gd)
    w1, gmat, cc, sc = (t.astype(BF16) for t in (w1, gmat, cc, sc))

    for i in range(depth):
        sh1, sc1, g1, sh2, sc2, g2 = [mod[i, :, k] for k in range(N_ADA)]
        j = i // 2
        if i % 2 == 0:
            qscale = jnp.concatenate([jnp.full((d,), LOG2E * DIFF_HEAD_DIM ** -0.5, F32),
                                      jnp.ones((2 * d,), F32)])
            w_in = (attn_w_in[j] * qscale).astype(BF16)
            qkv = _norm_mod_matmul(x, norm_mix[i].reshape(1, d), sh1, sc1, w_in, tm=512)
            lam_init = 0.8 - 0.6 * math.exp(-0.3 * i)
            o = _diff_attention(qkv, slopes,
                                attn_lam_q1[j].reshape(1, -1), attn_lam_k1[j].reshape(1, -1),
                                attn_lam_q2[j].reshape(1, -1), attn_lam_k2[j].reshape(1, -1),
                                attn_subln[j].reshape(-1, 1), lam_init)
            x = _proj_residual(x, o, attn_w_out[j].astype(BF16), g1, tm=512)
        else:
            u = _norm_mod_matmul(x, norm_mix[i].reshape(1, d), sh1, sc1,
                                 fourier_w_in[j].astype(BF16), tm=512)
            y = _fft_stage1(u.reshape(b, FFT_N1, FFT_N2 * d), w1, twr, twi, d)
            x = _fft2_mix(y.reshape(b, 2, FFT_N1, FFT_N2, d), gmat, x.reshape(b, FFT_N2, FFT_N1, d),
                          cc, sc, fourier_w_out[j].astype(BF16), g1).reshape(b, s, d)

        x = moe_layer(x, norm_ffn[i], sh2, sc2, g2, router_group_w[i], router_group_b[i],
                      router_expert_w[i], router_expert_b[i], expert_w_gate, expert_w_up,
                      expert_w_down, i, norm_final, i == depth - 1)
    return x
```

```python
import functools
import math

import numpy as np
import jax
import jax.numpy as jnp
from jax import lax
from jax.experimental import pallas as pl
from jax.experimental.pallas import tpu as pltpu

F32 = jnp.float32
BF16 = jnp.bfloat16

EPS = 1e-6
N_DIFF_HEADS = 8
DIFF_HEAD_DIM = 64
DIFF_V_DIM = 128
N_FOURIER_GROUPS = 4
N_EXPERT_GROUPS = 4
EXPERTS_PER_GROUP = 4
N_EXPERTS = 16
N_ADA = 6

LANES = 128
VMEM_LIMIT_BYTES = 56 * 1024 * 1024

FFT_N1 = 128
FFT_N2 = 64
FFT_K1_PER_STEP = 8


def _cparams(n_axes):
    return pltpu.CompilerParams(
        dimension_semantics=("arbitrary",) * n_axes,
        vmem_limit_bytes=VMEM_LIMIT_BYTES,
    )


def _rms_mod(x, g, shift, scale):
    y = x * lax.rsqrt(jnp.mean(x * x, axis=-1, keepdims=True) + EPS)
    return (y * g) * (1.0 + scale) + shift


def _ada_kernel(c_ref, w_ref, b_ref, o_ref):
    c = c_ref[...]
    cond = c / (1.0 + jnp.exp(-c))
    o_ref[0] = jnp.dot(cond, w_ref[0], precision=lax.Precision.HIGHEST,
                       preferred_element_type=F32) + b_ref[0]


def _ada(c8, ada_w, ada_b3):
    depth, d, n = ada_w.shape
    tn = 1536
    return pl.pallas_call(
        _ada_kernel,
        grid=(depth, n // tn),
        in_specs=[
            pl.BlockSpec((8, d), lambda i, j: (0, 0)),
            pl.BlockSpec((1, d, tn), lambda i, j: (i, 0, j)),
            pl.BlockSpec((1, 1, tn), lambda i, j: (i, 0, j)),
        ],
        out_specs=pl.BlockSpec((1, 8, tn), lambda i, j: (i, 0, j)),
        out_shape=jax.ShapeDtypeStruct((depth, 8, n), F32),
        compiler_params=_cparams(2),
        name="ada_mod",
    )(c8, ada_w, ada_b3)


def _nmm_kernel(x_ref, g_ref, sh_ref, sc_ref, w_ref, o_ref):
    h = _rms_mod(x_ref[0], g_ref[...], sh_ref[0], sc_ref[0])
    o_ref[0] = jnp.dot(h.astype(BF16), w_ref[...],
                       preferred_element_type=F32).astype(o_ref.dtype)


def _norm_mod_matmul(x, g, shift, scale, w_bf16, tm):
    b, s, d = x.shape
    n = w_bf16.shape[1]
    return pl.pallas_call(
        _nmm_kernel,
        grid=(b, s // tm),
        in_specs=[
            pl.BlockSpec((1, tm, d), lambda i, j: (i, j, 0)),
            pl.BlockSpec((1, d), lambda i, j: (0, 0)),
            pl.BlockSpec((1, 1, d), lambda i, j: (i, 0, 0)),
            pl.BlockSpec((1, 1, d), lambda i, j: (i, 0, 0)),
            pl.BlockSpec((d, n), lambda i, j: (0, 0)),
        ],
        out_specs=pl.BlockSpec((1, tm, n), lambda i, j: (i, j, 0)),
        out_shape=jax.ShapeDtypeStruct((b, s, n), BF16),
        compiler_params=_cparams(2),
        name="norm_mod_matmul",
    )(x, g, shift, scale, w_bf16)


ATT_TQ = 256
ATT_TK = 512
ATT_PAIR = 2 * ATT_TK
ATT_NQP = ATT_PAIR // ATT_TQ
ATT_QSUB = 4
ATT_TRIPS_FAST = (4, 2)
ATT_TRIPS_SAFE = (2,)
LOG2E = 1.4426950408889634
ATT_SKIP_BITS = 30.0
ATT_FAST_MAX_U = 36.0


def _attn_bounds_kernel(q_ref, k_ref, slope_ref, lo_ref, hi_ref, fast_ref, ub_ref, *, seq):
    tq = ATT_TQ
    nq = seq // tq
    dd = 2 * DIFF_HEAD_DIM
    r = lax.broadcasted_iota(jnp.int32, (dd, dd), 0)
    c = lax.broadcasted_iota(jnp.int32, (dd, dd), 1)
    halves = jnp.where(c == r // DIFF_HEAD_DIM, 1.0, 0.0).astype(BF16)

    def half_norm2(ref):
        x = ref[0].astype(F32)
        return jnp.dot((x * x).astype(BF16), halves, preferred_element_type=F32)

    qn2 = jnp.max(jnp.max(half_norm2(q_ref).reshape(nq, tq, dd), axis=1), axis=1, keepdims=True)
    kn2 = jnp.max(jnp.max(half_norm2(k_ref), axis=0, keepdims=True), axis=1, keepdims=True)
    ub = jnp.sqrt(qn2 * kn2) * (1.0 + 2.0 ** -6)
    slope = slope_ref[0] * LOG2E
    dist = (2.0 * ub + (ATT_SKIP_BITS + math.log2(seq))) / slope
    dist = jnp.ceil(jnp.minimum(dist, float(seq))).astype(jnp.int32)
    q0 = lax.broadcasted_iota(jnp.int32, (nq, 1), 0) * tq
    jmin = jnp.maximum(q0 - dist + 1, 0)
    jmax = jnp.minimum(q0 + (tq - 1) + dist - 1, seq - 1)
    shift = jnp.full((nq, 1), int(math.log2(ATT_TK)), jnp.int32)
    lo_ref[0, 0] = lax.shift_right_logical(jmin, shift)
    hi_ref[0, 0] = lax.shift_right_logical(jmax, shift) + 1
    fast_ref[0, 0] = (ub <= ATT_FAST_MAX_U).astype(jnp.int32)
    ub_ref[0, 0] = ub


def _attn_bounds(qkv, slopes):
    b, s, _ = qkv.shape
    h, dv = N_DIFF_HEADS, DIFF_V_DIM
    nq = s // ATT_TQ
    kern = functools.partial(_attn_bounds_kernel, seq=s)
    oblk = pl.BlockSpec((1, 1, nq, 1), lambda bi, hi: (bi, hi, 0, 0))
    ishape = jax.ShapeDtypeStruct((b, h, nq, 1), jnp.int32)
    return pl.pallas_call(
        kern,
        grid=(b, h),
        in_specs=[
            pl.BlockSpec((1, s, dv), lambda bi, hi: (bi, 0, hi)),
            pl.BlockSpec((1, s, dv), lambda bi, hi: (bi, 0, h + hi)),
            pl.BlockSpec((1, 1, 1), lambda bi, hi: (hi, 0, 0)),
        ],
        out_specs=[oblk, oblk, oblk, oblk],
        out_shape=[ishape, ishape, ishape, jax.ShapeDtypeStruct((b, h, nq, 1), F32)],
        compiler_params=_cparams(2),
        name="attn_bounds",
    )(qkv, qkv, slopes)


def _attn_kernel(lo_ref, hi_ref, fast_ref, q_ref, k_ref, v_ref, ub_ref, slope_ref,
                 lq1_ref, lk1_ref, lq2_ref, lk2_ref, sub_ref,
                 o_ref, vt_ref, bias_ref, u_ref, m_ref, l_ref, acc_ref, *, lam_init, seq):
    tq, tk, pair, nqp = ATT_TQ, ATT_TK, ATT_PAIR, ATT_NQP
    dh = DIFF_HEAD_DIM
    step = pl.program_id(2)
    slope = slope_ref[0] * LOG2E

    @pl.when(step == 0)
    def _():
        for c in range(seq // 512):
            vt_ref[:, c * 512:(c + 1) * 512] = v_ref[0, c * 512:(c + 1) * 512, :].T
        for s in range(2):
            rows = slice(s * tk, (s + 1) * tk)
            r = lax.broadcasted_iota(jnp.int32, (tk, 2 * tq), 0) + s * tk
            c = lax.broadcasted_iota(jnp.int32, (tk, 2 * tq), 1)
            c = jnp.where(c >= tq, c - tq, c)
            rc = (r - c).astype(F32)
            bias_ref[0, rows, :] = slope * rc
            bias_ref[1, rows, :] = -(slope * rc)
            for w in range(nqp):
                bias_ref[2 + w, rows, :] = -(slope * jnp.abs(rc - float(w * tq)))

    lam = (jnp.exp(jnp.sum(lq1_ref[...] * lk1_ref[...], axis=-1, keepdims=True))
           - jnp.exp(jnp.sum(lq2_ref[...] * lk2_ref[...], axis=-1, keepdims=True))
           + lam_init)

    def one_tile(t):
        qi = step * ATT_QSUB + t
        tile = (pl.program_id(0) * N_DIFF_HEADS + pl.program_id(1)) * (seq // tq) + qi
        lo, hi = lo_ref[tile], hi_ref[tile]
        lt_ref, at_ref = l_ref.at[t], acc_ref.at[t]

        qt = q_ref[0, t * tq:(t + 1) * tq, :].T
        z = jnp.zeros((dh, tq), BF16)
        qbd = jnp.concatenate([jnp.concatenate([qt[:dh], z], axis=1),
                               jnp.concatenate([z, qt[dh:]], axis=1)], axis=0)

        lt_ref[...] = jnp.zeros(lt_ref.shape, F32)
        at_ref[...] = jnp.zeros(at_ref.shape, F32)

        kd = qi // nqp
        w = qi % nqp

        def mode(n):
            kp = n // 2
            delta = (kp * pair - qi * tq).astype(F32)
            idx = jnp.where(kp < kd, 0, jnp.where(kp > kd, 1, 2 + w))
            sgn = jnp.where(kp < kd, delta, jnp.where(kp > kd, -delta, 0.0))
            return idx, pl.multiple_of((n % 2) * tk, tk), slope * sgn

        def scores(n):
            idx, row0, _ = mode(n)
            k0 = pl.multiple_of(n * tk, tk)
            kblk = k_ref[0, pl.ds(k0, tk), :]
            u = jnp.dot(kblk, qbd, preferred_element_type=F32)
            return u + bias_ref[idx, pl.ds(row0, tk), :]

        def values(n):
            return vt_ref[:, pl.ds(pl.multiple_of(n * tk, tk), tk)]

        def pipelined(qk_phase, sm_phase, carry0, trips):
            def body(n, carry_a):
                carry_b = qk_phase(n + 1, 1)
                sm_phase(n, 0, carry_a)
                carry_a_next = qk_phase(jnp.minimum(n + 2, hi - 1), 0)
                sm_phase(n + 1, 1, carry_b)
                return carry_a_next

            start, carry = lo, carry0
            for npt in trips:
                ntrip = (hi - start) // (2 * npt)

                def trip(i, c, start=start, npt=npt):
                    for j in range(npt):
                        c = body(start + 2 * (npt * i + j), c)
                    return c

                carry = lax.fori_loop(0, ntrip, trip, carry)
                start = start + 2 * npt * ntrip
            ndouble = (hi - start) // 2
            carry = lax.fori_loop(0, ndouble, lambda i, c, start=start: body(start + 2 * i, c), carry)
            last = start + 2 * ndouble

            @pl.when(last < hi)
            def _():
                sm_phase(last, 0, carry)

        def fixed_reference():
            ub = ub_ref[t]

            def qk_phase(n, slot):
                u_ref[slot] = scores(n)
                return 0

            def sm_phase(n, slot, _):
                cst = mode(n)[2]
                p = jnp.exp2(u_ref[slot] + (cst - ub))
                lt_ref[...] += jnp.sum(p, axis=0, keepdims=True)
                at_ref[...] += jnp.dot(values(n), p.astype(BF16), preferred_element_type=F32)

            pipelined(qk_phase, sm_phase, 0, ATT_TRIPS_FAST)

        def running_max():
            m_ref[...] = jnp.full(m_ref.shape, -jnp.inf, F32)

            def qk_phase(n, slot):
                v = scores(n)
                u_ref[slot] = v
                return jnp.max(v, axis=0, keepdims=True)

            def sm_phase(n, slot, mx):
                cst = mode(n)[2]
                m_old = m_ref[...]
                m_new = jnp.maximum(m_old, mx + cst)
                alpha = jnp.exp2(m_old - m_new)
                p = jnp.exp2(u_ref[slot] - (m_new - cst))
                lt_ref[...] = alpha * lt_ref[...] + jnp.sum(p, axis=0, keepdims=True)
                m_ref[...] = m_new
                at_ref[...] = alpha * at_ref[...] + jnp.dot(values(n), p.astype(BF16),
                                                            preferred_element_type=F32)

            pipelined(qk_phase, sm_phase, jnp.max(u_ref[0], axis=0, keepdims=True), ATT_TRIPS_SAFE)

        u_ref[0] = scores(lo)
        lax.cond(fast_ref[tile] == 1, fixed_reference, running_max)

        acc = at_ref[...]
        l = lt_ref[...]
        ot = acc[:, :tq] / l[:, :tq] - lam * (acc[:, tq:] / l[:, tq:])
        ms = jnp.mean(ot * ot, axis=0, keepdims=True)
        ot = ot * lax.rsqrt(ms + EPS) * sub_ref[...] * (1.0 - lam_init)
        o_ref[0, t * tq:(t + 1) * tq, :] = ot.T.astype(o_ref.dtype)

    for t in range(ATT_QSUB):
        one_tile(t)


def _diff_attention(qkv, slopes, lq1, lk1, lq2, lk2, subln_col, lam_init):
    b, s, _ = qkv.shape
    h, dv = N_DIFF_HEADS, DIFF_V_DIM
    tq, tk, qsub = ATT_TQ, ATT_TK, ATT_QSUB
    nq = s // tq
    assert s % ATT_PAIR == 0 and nq % qsub == 0
    nsteps = nq // qsub
    lo, hi, fast, ub = _attn_bounds(qkv, slopes)
    kern = functools.partial(_attn_kernel, lam_init=lam_init, seq=s)
    vec = pl.BlockSpec((1, DIFF_HEAD_DIM), lambda bi, hi, qi, *_: (0, 0))
    grid_spec = pltpu.PrefetchScalarGridSpec(
        num_scalar_prefetch=3,
        grid=(b, h, nsteps),
        in_specs=[
            pl.BlockSpec((1, qsub * tq, dv), lambda bi, hi, qi, *_: (bi, qi, hi)),
            pl.BlockSpec((1, s, dv), lambda bi, hi, qi, *_: (bi, 0, h + hi)),
            pl.BlockSpec((1, s, dv), lambda bi, hi, qi, *_: (bi, 0, 2 * h + hi)),
            pl.BlockSpec((qsub, 1, 1), lambda bi, hi, qi, *_: ((bi * h + hi) * nsteps + qi, 0, 0)),
            pl.BlockSpec((1, 1, 1), lambda bi, hi, qi, *_: (hi, 0, 0)),
            vec, vec, vec, vec,
            pl.BlockSpec((dv, 1), lambda bi, hi, qi, *_: (0, 0)),
        ],
        out_specs=pl.BlockSpec((1, qsub * tq, dv), lambda bi, hi, qi, *_: (bi, qi, hi)),
        scratch_shapes=[
            pltpu.VMEM((dv, s), BF16),
            pltpu.VMEM((2 + ATT_NQP, ATT_PAIR, 2 * tq), F32),
            pltpu.VMEM((2, tk, 2 * tq), F32),
            pltpu.VMEM((1, 2 * tq), F32),
            pltpu.VMEM((qsub, 1, 2 * tq), F32),
            pltpu.VMEM((qsub, dv, 2 * tq), F32),
        ],
    )
    return pl.pallas_call(
        kern,
        grid_spec=grid_spec,
        out_shape=jax.ShapeDtypeStruct((b, s, h * dv), BF16),
        compiler_params=_cparams(3),
        name="diff_attention",
    )(lo.reshape(-1), hi.reshape(-1), fast.reshape(-1),
      qkv, qkv, qkv, ub.reshape(-1, 1, 1), slopes, lq1, lk1, lq2, lk2, subln_col)


def _proj_res_kernel(x_ref, a_ref, w_ref, gate_ref, o_ref):
    y = jnp.dot(a_ref[0], w_ref[...], preferred_element_type=F32)
    o_ref[0] = x_ref[0] + gate_ref[0] * y


def _proj_residual(x, a, w_bf16, gate, tm):
    b, s, d = x.shape
    kdim = a.shape[-1]
    return pl.pallas_call(
        _proj_res_kernel,
        grid=(b, s // tm),
        in_specs=[
            pl.BlockSpec((1, tm, d), lambda i, j: (i, j, 0)),
            pl.BlockSpec((1, tm, kdim), lambda i, j: (i, j, 0)),
            pl.BlockSpec((kdim, d), lambda i, j: (0, 0)),
            pl.BlockSpec((1, 1, d), lambda i, j: (i, 0, 0)),
        ],
        out_specs=pl.BlockSpec((1, tm, d), lambda i, j: (i, j, 0)),
        out_shape=jax.ShapeDtypeStruct((b, s, d), F32),
        compiler_params=_cparams(2),
        name="proj_residual",
    )(x, a, w_bf16, gate)


def _routing_gates_t(lt):
    ng, ne = N_EXPERT_GROUPS, EXPERTS_PER_GROUP
    gl = [lt[g:g + 1, :] for g in range(ng)]
    gmax = functools.reduce(jnp.maximum, gl)
    den = functools.reduce(lambda a, b: a + b, [jnp.exp(v - gmax) for v in gl])
    g_w = 1.0 / den
    sel = []
    taken = jnp.zeros(gmax.shape, jnp.bool_)
    for g in range(ng):
        s = jnp.logical_and(gl[g] == gmax, jnp.logical_not(taken))
        taken = jnp.logical_or(taken, s)
        sel.append(s)
    ev = []
    for e in range(ne):
        v = jnp.zeros_like(gmax)
        for g in range(ng):
            row = ng + ne * g + e
            v = jnp.where(sel[g], lt[row:row + 1, :], v)
        ev.append(v)
    top1 = functools.reduce(jnp.maximum, ev)
    is1 = []
    taken = jnp.zeros(gmax.shape, jnp.bool_)
    for e in range(ne):
        s = jnp.logical_and(ev[e] == top1, jnp.logical_not(taken))
        taken = jnp.logical_or(taken, s)
        is1.append(s)
    rest = [jnp.where(is1[e], -jnp.inf, ev[e]) for e in range(ne)]
    top2 = functools.reduce(jnp.maximum, rest)
    is2 = []
    taken = jnp.zeros(gmax.shape, jnp.bool_)
    for e in range(ne):
        s = jnp.logical_and(jnp.logical_and(rest[e] == top2, jnp.logical_not(is1[e])),
                            jnp.logical_not(taken))
        taken = jnp.logical_or(taken, s)
        is2.append(s)
    ex = jnp.exp(top2 - top1)
    w1 = 1.0 / (1.0 + ex)
    w2 = ex / (1.0 + ex)
    ew = [jnp.where(is1[e], w1, jnp.where(is2[e], w2, 0.0)) for e in range(ne)]
    rows = [jnp.where(sel[g], g_w * ew[e], 0.0) for g in range(ng) for e in range(ne)]
    return jnp.concatenate(rows, axis=0), sel


MOE_TM = 1024
MOE_CH = 128
MOE_NB = 11
MOE_POS_LANE = 3 * N_EXPERTS


def _route_sort_kernel(x_ref, g_ref, sh_ref, sc_ref, wr_ref, br_ref, tri_ref,
                       hs_ref, gs_ref, meta_ref, gid_ref):
    ng, ch, nb = N_EXPERT_GROUPS, MOE_CH, MOE_NB
    tm = x_ref.shape[0]
    h32 = _rms_mod(x_ref[...], g_ref[...], sh_ref[0], sc_ref[0])
    hb = h32.astype(BF16)
    hl = (h32 - hb.astype(F32)).astype(BF16)
    wr = wr_ref[...]
    wb = wr.astype(BF16)
    wl = (wr - wb.astype(F32)).astype(BF16)
    nt = (((1,), (1,)), ((), ()))
    l2 = lax.dot_general(jnp.concatenate([wb, wl], axis=0), hb, nt, preferred_element_type=F32)
    lt = (l2[:LANES] + l2[LANES:] + lax.dot_general(wb, hl, nt, preferred_element_type=F32)
          + br_ref[...])
    gates_t, sel = _routing_gates_t(lt)
    oh = jnp.concatenate([m.astype(F32) for m in sel] + [jnp.zeros((16 - ng, tm), F32)], axis=0)
    cnt = jnp.dot(oh.astype(BF16), tri_ref[...], preferred_element_type=F32)
    rank = jnp.sum(oh * cnt, axis=0, keepdims=True)
    n = jnp.sum(oh, axis=1, keepdims=True)
    nch = jnp.floor((n + float(ch - 1)) * (1.0 / ch))
    first = [jnp.zeros((1, 1), F32)]
    for g in range(ng):
        first.append(first[-1] + nch[g:g + 1, :])
    seg = functools.reduce(lambda a, b: a + b,
                           [oh[g:g + 1, :] * (first[g] * float(ch)) for g in range(ng)])
    pos = seg + rank
    rows = lax.broadcasted_iota(jnp.int32, (nb * ch, tm), 0)
    perm = jnp.where(rows == pos.astype(jnp.int32), 1.0, 0.0).astype(BF16)
    hs_ref[0] = jnp.dot(perm, hb, preferred_element_type=F32).astype(BF16)
    hi = gates_t.astype(BF16).astype(F32)
    r1 = gates_t - hi
    mid = r1.astype(BF16).astype(F32)
    lo = r1 - mid
    meta_t = jnp.concatenate([hi, mid, lo, pos, jnp.zeros((LANES - MOE_POS_LANE - 1, tm), F32)], axis=0)
    meta = meta_t.T
    meta_ref[...] = meta
    gs_ref[0] = jnp.dot(perm, meta.astype(BF16), preferred_element_type=F32)
    lane = lax.broadcasted_iota(jnp.int32, (1, LANES), 1).astype(F32)
    gid = functools.reduce(lambda a, b: a + b,
                           [jnp.where(lane >= first[g + 1], 1.0, 0.0) for g in range(ng)])
    gid_ref[0] = gid.astype(jnp.int32)


def _route_sort(x2, g, shift, scale, wr, br, tri, tiles_per_batch):
    t, d = x2.shape
    tm, rows = MOE_TM, MOE_NB * MOE_CH
    nt = t // tm
    bvec = lambda i: (i // tiles_per_batch, 0, 0)
    return pl.pallas_call(
        _route_sort_kernel,
        grid=(nt,),
        in_specs=[
            pl.BlockSpec((tm, d), lambda i: (i, 0)),
            pl.BlockSpec((1, d), lambda i: (0, 0)),
            pl.BlockSpec((1, 1, d), bvec),
            pl.BlockSpec((1, 1, d), bvec),
            pl.BlockSpec((LANES, d), lambda i: (0, 0)),
            pl.BlockSpec((LANES, 1), lambda i: (0, 0)),
            pl.BlockSpec((tm, tm), lambda i: (0, 0)),
        ],
        out_specs=[
            pl.BlockSpec((1, rows, d), lambda i: (i, 0, 0)),
            pl.BlockSpec((1, rows, LANES), lambda i: (i, 0, 0)),
            pl.BlockSpec((tm, LANES), lambda i: (i, 0)),
            pl.BlockSpec((1, 1, LANES), lambda i: (i, 0, 0)),
        ],
        out_shape=[
            jax.ShapeDtypeStruct((nt, rows, d), BF16),
            jax.ShapeDtypeStruct((nt, rows, LANES), F32),
            jax.ShapeDtypeStruct((t, LANES), F32),
            jax.ShapeDtypeStruct((nt, 1, LANES), jnp.int32),
        ],
        compiler_params=_cparams(1),
        name="moe_route_sort",
    )(x2, g, shift, scale, wr, br, tri)


def _experts_kernel(order_ref, grp_ref, first_ref, hs_ref, gs_ref, wg_ref, wu_ref, wd_ref, ys_ref,
                    wgb_ref, wub_ref, wdb_ref):
    ne = EXPERTS_PER_GROUP
    step = pl.program_id(0)
    grp = grp_ref[step]

    @pl.when(first_ref[step] == 1)
    def _():
        wgb_ref[...] = wg_ref[0].astype(BF16)
        wub_ref[...] = wu_ref[0].astype(BF16)
        wdb_ref[...] = wd_ref[0].astype(BF16)

    @pl.when(grp < N_EXPERT_GROUPS)
    def _():
        h = hs_ref[0]
        gs = gs_ref[0]
        lane = lax.broadcasted_iota(jnp.int32, gs.shape, 1)
        hid = []
        for e in range(ne):
            idx = grp * ne + e
            pick = jnp.logical_or(jnp.logical_or(lane == idx, lane == idx + N_EXPERTS),
                                  lane == idx + 2 * N_EXPERTS)
            gate = jnp.sum(jnp.where(pick, gs, 0.0), axis=1, keepdims=True)
            a = jnp.dot(h, wgb_ref[e], preferred_element_type=F32)
            u = jnp.dot(h, wub_ref[e], preferred_element_type=F32)
            hid.append(((a / (1.0 + jnp.exp(-a))) * u * gate).astype(BF16))
        hid = jnp.concatenate(hid, axis=1)
        ys_ref[0] = jnp.dot(hid, wdb_ref[...], preferred_element_type=F32).astype(ys_ref.dtype)

    @pl.when(grp >= N_EXPERT_GROUPS)
    def _():
        ys_ref[0] = jnp.zeros(ys_ref.shape[1:], ys_ref.dtype)


def _experts(order, grp, first, hs, gs, wg4, wu4, wd4, layer):
    nchunks, ch, d = hs.shape
    ne, f = wg4.shape[1], wg4.shape[-1]
    ng = N_EXPERT_GROUPS
    wsel = lambda s, order, grp, first: (layer * ng + jnp.minimum(grp[s], ng - 1), 0, 0, 0)
    wsel3 = lambda s, order, grp, first: (layer * ng + jnp.minimum(grp[s], ng - 1), 0, 0)
    chunk = lambda s, order, grp, first: (order[s], 0, 0)
    once = pl.Buffered(1)
    grid_spec = pltpu.PrefetchScalarGridSpec(
        num_scalar_prefetch=3,
        grid=(nchunks,),
        in_specs=[
            pl.BlockSpec((1, ch, d), chunk),
            pl.BlockSpec((1, ch, LANES), chunk),
            pl.BlockSpec((1, ne, d, f), wsel),
            pl.BlockSpec((1, ne, d, f), wsel),
            pl.BlockSpec((1, ne * f, d), wsel3, pipeline_mode=once),
        ],
        out_specs=pl.BlockSpec((1, ch, d), chunk),
        scratch_shapes=[
            pltpu.VMEM((ne, d, f), BF16),
            pltpu.VMEM((ne, d, f), BF16),
            pltpu.VMEM((ne * f, d), BF16),
        ],
    )
    return pl.pallas_call(
        _experts_kernel,
        grid_spec=grid_spec,
        out_shape=jax.ShapeDtypeStruct((nchunks, ch, d), BF16),
        compiler_params=_cparams(1),
        name="moe_experts",
    )(order, grp, first, hs, gs, wg4, wu4, wd4)


def _unsort_kernel(x_ref, ys_ref, meta_ref, gate_ref, gf_ref, o_ref, *, final_norm):
    rows = ys_ref.shape[1]
    tm = x_ref.shape[0]
    pos = meta_ref[:, MOE_POS_LANE:MOE_POS_LANE + 1].astype(jnp.int32)
    cols = lax.broadcasted_iota(jnp.int32, (tm, rows), 1)
    perm_t = jnp.where(cols == pos, 1.0, 0.0).astype(BF16)
    y = jnp.dot(perm_t, ys_ref[0], preferred_element_type=F32)
    y = x_ref[...] + gate_ref[0] * y
    if final_norm:
        y = y * lax.rsqrt(jnp.mean(y * y, axis=-1, keepdims=True) + EPS) * gf_ref[...]
    o_ref[...] = y


def _unsort_residual(x2, ys, meta, gate, gfinal, final_norm, tiles_per_batch):
    t, d = x2.shape
    tm = MOE_TM
    nt, rows, _ = ys.shape
    kern = functools.partial(_unsort_kernel, final_norm=final_norm)
    return pl.pallas_call(
        kern,
        grid=(nt,),
        in_specs=[
            pl.BlockSpec((tm, d), lambda i: (i, 0)),
            pl.BlockSpec((1, rows, d), lambda i: (i, 0, 0)),
            pl.BlockSpec((tm, LANES), lambda i: (i, 0)),
            pl.BlockSpec((1, 1, d), lambda i: (i // tiles_per_batch, 0, 0)),
            pl.BlockSpec((1, d), lambda i: (0, 0)),
        ],
        out_specs=pl.BlockSpec((tm, d), lambda i: (i, 0)),
        out_shape=jax.ShapeDtypeStruct((t, d), F32),
        compiler_params=_cparams(1),
        name="moe_unsort_residual",
    )(x2, ys, meta, gate, gfinal)


def moe_layer(x, g, sh2, sc2, g2, w_rg, b_rg, w_re, b_re, w_gate, w_up, w_down, layer, gfinal, final_norm):
    b, s, d = x.shape
    ng, ne = N_EXPERT_GROUPS, EXPERTS_PER_GROUP
    f = w_gate.shape[-1]
    assert s % MOE_TM == 0
    nr = ng + N_EXPERTS
    wr = jnp.concatenate([w_rg.T, w_re.reshape(d, N_EXPERTS).T,
                          jnp.zeros((LANES - nr, d), F32)], axis=0)
    br = jnp.concatenate([b_rg, b_re.reshape(N_EXPERTS),
                          jnp.zeros((LANES - nr,), F32)]).reshape(LANES, 1)
    ti = np.arange(MOE_TM)
    tri = jnp.asarray(ti[:, None] < ti[None, :], F32).astype(BF16)
    x2 = x.reshape(b * s, d)
    tpb = s // MOE_TM
    hs, gs, meta, gid = _route_sort(x2, g.reshape(1, d), sh2, sc2, wr, br, tri, tpb)
    nt = hs.shape[0]
    gid_flat = gid[:, 0, :MOE_NB].reshape(-1)
    order = jnp.argsort(gid_flat, stable=True).astype(jnp.int32)
    grp = gid_flat[order]
    first = jnp.concatenate([jnp.ones((1,), jnp.int32), (grp[1:] != grp[:-1]).astype(jnp.int32)])
    nl = w_gate.shape[0]
    ys = _experts(order, grp, first,
                  hs.reshape(nt * MOE_NB, MOE_CH, d), gs.reshape(nt * MOE_NB, MOE_CH, LANES),
                  w_gate.reshape(nl * ng, ne, d, f), w_up.reshape(nl * ng, ne, d, f),
                  w_down.reshape(nl * ng, ne * f, d), layer)
    out = _unsort_residual(x2, ys.reshape(nt, MOE_NB * MOE_CH, d), meta, g2,
                           gfinal.reshape(1, d), final_norm, tpb)
    return out.reshape(b, s, d)


def _fft1_kernel(u_ref, w_ref, twr_ref, twi_ref, o_ref, *, n_sub, width):
    n1 = FFT_N1
    y = jnp.dot(w_ref[...], u_ref[0], preferred_element_type=F32)
    for j in range(n_sub):
        cols = slice(j * width, (j + 1) * width)
        yr, yi = y[:n1, cols], y[n1:, cols]
        twr, twi = twr_ref[j], twi_ref[j]
        o_ref[0, :n1, cols] = (yr * twr - yi * twi).astype(o_ref.dtype)
        o_ref[0, n1:, cols] = (yr * twi + yi * twr).astype(o_ref.dtype)


def _fft_stage1(u3, w1, twr, twi, width):
    b, n1, cols = u3.shape
    n_sub = 8
    tn = n_sub * width
    kern = functools.partial(_fft1_kernel, n_sub=n_sub, width=width)
    return pl.pallas_call(
        kern,
        grid=(b, cols // tn),
        in_specs=[
            pl.BlockSpec((1, n1, tn), lambda i, j: (i, 0, j)),
            pl.BlockSpec((2 * n1, n1), lambda i, j: (0, 0)),
            pl.BlockSpec((n_sub, n1, 1), lambda i, j: (j, 0, 0)),
            pl.BlockSpec((n_sub, n1, 1), lambda i, j: (j, 0, 0)),
        ],
        out_specs=pl.BlockSpec((1, 2 * n1, tn), lambda i, j: (i, 0, j)),
        out_shape=jax.ShapeDtypeStruct((b, 2 * n1, cols), BF16),
        compiler_params=_cparams(2),
        name="fft_stage1",
    )(u3, w1, twr, twi)


def _fft2_mix_kernel(y_ref, g_ref, x_ref, cc_ref, sc_ref, w_ref, gate_ref, o_ref):
    kp, n2 = FFT_K1_PER_STEP, FFT_N2
    width = y_ref.shape[-1]
    gd = cc_ref.shape[0]
    half = kp * n2
    y = y_ref[0].reshape(2 * half, width)
    r = jnp.dot(g_ref[...], y, preferred_element_type=F32)
    ar, ai = r[:half].astype(BF16), r[half:].astype(BF16)
    parts = []
    for g in range(N_FOURIER_GROUPS):
        cols = slice(g * gd, (g + 1) * gd)
        parts.append(jnp.dot(ar[:, cols], cc_ref[...], preferred_element_type=F32)
                     + jnp.dot(ai[:, cols], sc_ref[...], preferred_element_type=F32))
    f = jnp.concatenate(parts, axis=1).astype(BF16)
    yv = jnp.dot(f, w_ref[...], preferred_element_type=F32)
    out = x_ref[0].reshape(half, width) + gate_ref[0] * yv
    o_ref[0] = out.reshape(n2, kp, width)


def _fft2_mix(y5, gmat, x4, cc, sc, w_bf16, gate):
    b, _, n1, n2, width = y5.shape
    kp = FFT_K1_PER_STEP
    gd = cc.shape[0]
    return pl.pallas_call(
        _fft2_mix_kernel,
        grid=(b, n1 // kp),
        in_specs=[
            pl.BlockSpec((1, 2, kp, n2, width), lambda i, j: (i, 0, j, 0, 0)),
            pl.BlockSpec((2 * kp * n2, 2 * kp * n2), lambda i, j: (0, 0)),
            pl.BlockSpec((1, n2, kp, width), lambda i, j: (i, 0, j, 0)),
            pl.BlockSpec((gd, gd), lambda i, j: (0, 0)),
            pl.BlockSpec((gd, gd), lambda i, j: (0, 0)),
            pl.BlockSpec((width, width), lambda i, j: (0, 0)),
            pl.BlockSpec((1, 1, width), lambda i, j: (i, 0, 0)),
        ],
        out_specs=pl.BlockSpec((1, n2, kp, width), lambda i, j: (i, 0, j, 0)),
        out_shape=jax.ShapeDtypeStruct((b, n2, n1, width), F32),
        compiler_params=_cparams(2),
        name="fft_stage2_mix",
    )(y5, gmat, x4, cc, sc, w_bf16, gate)


def _dft_tables(seq, gd):
    n1, n2, kp = FFT_N1, FFT_N2, FFT_K1_PER_STEP

    def cs(num, den):
        ang = (2.0 * math.pi / den) * (num % den).astype(np.float64)
        return np.cos(ang), np.sin(ang)

    i1 = np.arange(n1)
    c1, s1 = cs(np.outer(i1, i1), n1)
    w1 = np.concatenate([c1, -s1], axis=0)
    ct, st = cs(np.outer(np.arange(n2), i1), seq)
    twr, twi = ct[:, :, None], -st[:, :, None]
    i2 = np.arange(n2)
    c2, s2 = cs(np.outer(i2, i2), n2)
    m3 = np.block([[c2, s2], [-s2, c2]])
    m3 = m3.reshape(2, n2, 2, n2)
    gmat = np.zeros((2, n2, kp, 2, kp, n2))
    for j in range(kp):
        gmat[:, :, j, :, j, :] = m3
    gmat = gmat.reshape(2 * n2 * kp, 2 * kp * n2)
    ic = np.arange(gd)
    cc, sc = cs(np.outer(ic, ic), gd)
    norm = 1.0 / math.sqrt(seq * gd)
    return tuple(jnp.asarray(t, F32) for t in (w1, twr, twi, gmat, cc * norm, sc * norm))


def kernel(x, c, ada_w, ada_b, norm_mix, norm_ffn, attn_w_in, attn_lam_q1, attn_lam_k1, attn_lam_q2, attn_lam_k2, attn_subln, attn_w_out, fourier_w_in, fourier_w_out, router_group_w, router_group_b, router_expert_w, router_expert_b, expert_w_gate, expert_w_up, expert_w_down, norm_final):
    b, s, d = x.shape
    depth = ada_w.shape[0]
    assert s == FFT_N1 * FFT_N2 and depth == 2

    c8 = jnp.zeros((8, d), F32).at[:b].set(c)
    mod = _ada(c8, ada_w, ada_b.reshape(depth, 1, N_ADA * d))[:, :b]
    mod = mod.reshape(depth, b, N_ADA, 1, d)

    slopes = (2.0 ** (-8.0 * jnp.arange(1, N_DIFF_HEADS + 1, dtype=F32) / N_DIFF_HEADS)
              ).reshape(N_DIFF_HEADS, 1, 1)
    gd = d // N_FOURIER_GROUPS
    w1, twr, twi, gmat, cc, sc = _dft_tables(s, gd)
    w1, gmat, cc, sc = (t.astype(BF16) for t in (w1, gmat, cc, sc))

    for i in range(depth):
        sh1, sc1, g1, sh2, sc2, g2 = [mod[i, :, k] for k in range(N_ADA)]
        j = i // 2
        if i % 2 == 0:
            qscale = jnp.concatenate([jnp.full((d,), LOG2E * DIFF_HEAD_DIM ** -0.5, F32),
                                      jnp.ones((2 * d,), F32)])
            w_in = (attn_w_in[j] * qscale).astype(BF16)
            qkv = _norm_mod_matmul(x, norm_mix[i].reshape(1, d), sh1, sc1, w_in, tm=512)
            lam_init = 0.8 - 0.6 * math.exp(-0.3 * i)
            o = _diff_attention(qkv, slopes,
                                attn_lam_q1[j].reshape(1, -1), attn_lam_k1[j].reshape(1, -1),
                                attn_lam_q2[j].reshape(1, -1), attn_lam_k2[j].reshape(1, -1),
                                attn_subln[j].reshape(-1, 1), lam_init)
            x = _proj_residual(x, o, attn_w_out[j].astype(BF16), g1, tm=512)
        else:
            u = _norm_mod_matmul(x, norm_mix[i].reshape(1, d), sh1, sc1,
                                 fourier_w_in[j].astype(BF16), tm=512)
            y = _fft_stage1(u.reshape(b, FFT_N1, FFT_N2 * d), w1, twr, twi, d)
            x = _fft2_mix(y.reshape(b, 2, FFT_N1, FFT_N2, d), gmat, x.reshape(b, FFT_N2, FFT_N1, d),
                          cc, sc, fourier_w_out[j].astype(BF16), g1).reshape(b, s, d)

        x = moe_layer(x, norm_ffn[i], sh2, sc2, g2, router_group_w[i], router_group_b[i],
                      router_expert_w[i], router_expert_b[i], expert_w_gate, expert_w_up,
                      expert_w_down, i, norm_final, i == depth - 1)
    return x
```

```python
import functools
import math

import numpy as np
import jax
import jax.numpy as jnp
from jax import lax
from jax.experimental import pallas as pl
from jax.experimental.pallas import tpu as pltpu

F32 = jnp.float32
BF16 = jnp.bfloat16

EPS = 1e-6
N_DIFF_HEADS = 8
DIFF_HEAD_DIM = 64
DIFF_V_DIM = 128
N_FOURIER_GROUPS = 4
N_EXPERT_GROUPS = 4
EXPERTS_PER_GROUP = 4
N_EXPERTS = 16
N_ADA = 6

LANES = 128
VMEM_LIMIT_BYTES = 56 * 1024 * 1024

FFT_N1 = 128
FFT_N2 = 64
FFT_K1_PER_STEP = 8


def _cparams(n_axes):
    return pltpu.CompilerParams(
        dimension_semantics=("arbitrary",) * n_axes,
        vmem_limit_bytes=VMEM_LIMIT_BYTES,
    )


def _rms_mod(x, g, shift, scale):
    y = x * lax.rsqrt(jnp.mean(x * x, axis=-1, keepdims=True) + EPS)
    return (y * g) * (1.0 + scale) + shift


def _ada_kernel(c_ref, w_ref, b_ref, o_ref):
    c = c_ref[...]
    cond = c / (1.0 + jnp.exp(-c))
    o_ref[0] = jnp.dot(cond, w_ref[0], precision=lax.Precision.HIGHEST,
                       preferred_element_type=F32) + b_ref[0]


def _ada(c8, ada_w, ada_b3):
    depth, d, n = ada_w.shape
    tn = 1536
    return pl.pallas_call(
        _ada_kernel,
        grid=(depth, n // tn),
        in_specs=[
            pl.BlockSpec((8, d), lambda i, j: (0, 0)),
            pl.BlockSpec((1, d, tn), lambda i, j: (i, 0, j)),
            pl.BlockSpec((1, 1, tn), lambda i, j: (i, 0, j)),
        ],
        out_specs=pl.BlockSpec((1, 8, tn), lambda i, j: (i, 0, j)),
        out_shape=jax.ShapeDtypeStruct((depth, 8, n), F32),
        compiler_params=_cparams(2),
        name="ada_mod",
    )(c8, ada_w, ada_b3)


def _nmm_kernel(x_ref, g_ref, sh_ref, sc_ref, w_ref, o_ref):
    h = _rms_mod(x_ref[0], g_ref[...], sh_ref[0], sc_ref[0])
    o_ref[0] = jnp.dot(h.astype(BF16), w_ref[...],
                       preferred_element_type=F32).astype(o_ref.dtype)


def _norm_mod_matmul(x, g, shift, scale, w_bf16, tm):
    b, s, d = x.shape
    n = w_bf16.shape[1]
    return pl.pallas_call(
        _nmm_kernel,
        grid=(b, s // tm),
        in_specs=[
            pl.BlockSpec((1, tm, d), lambda i, j: (i, j, 0)),
            pl.BlockSpec((1, d), lambda i, j: (0, 0)),
            pl.BlockSpec((1, 1, d), lambda i, j: (i, 0, 0)),
            pl.BlockSpec((1, 1, d), lambda i, j: (i, 0, 0)),
            pl.BlockSpec((d, n), lambda i, j: (0, 0)),
        ],
        out_specs=pl.BlockSpec((1, tm, n), lambda i, j: (i, j, 0)),
        out_shape=jax.ShapeDtypeStruct((b, s, n), BF16),
        compiler_params=_cparams(2),
        name="norm_mod_matmul",
    )(x, g, shift, scale, w_bf16)


ATT_TQ = 256
ATT_TK = 512
ATT_PAIR = 2 * ATT_TK
ATT_NQP = ATT_PAIR // ATT_TQ
ATT_QSUB = 4
ATT_TRIPS_FAST = (4, 2)
ATT_TRIPS_SAFE = (2,)
LOG2E = 1.4426950408889634
ATT_SKIP_BITS = 30.0
ATT_FAST_MAX_U = 36.0


def _attn_bounds_kernel(q_ref, k_ref, slope_ref, lo_ref, hi_ref, fast_ref, ub_ref, *, seq):
    tq = ATT_TQ
    nq = seq // tq
    dd = 2 * DIFF_HEAD_DIM
    r = lax.broadcasted_iota(jnp.int32, (dd, dd), 0)
    c = lax.broadcasted_iota(jnp.int32, (dd, dd), 1)
    halves = jnp.where(c == r // DIFF_HEAD_DIM, 1.0, 0.0).astype(BF16)

    def half_norm2(ref):
        x = ref[0].astype(F32)
        return jnp.dot((x * x).astype(BF16), halves, preferred_element_type=F32)

    qn2 = jnp.max(jnp.max(half_norm2(q_ref).reshape(nq, tq, dd), axis=1), axis=1, keepdims=True)
    kn2 = jnp.max(jnp.max(half_norm2(k_ref), axis=0, keepdims=True), axis=1, keepdims=True)
    ub = jnp.sqrt(qn2 * kn2) * (1.0 + 2.0 ** -6)
    slope = slope_ref[0] * LOG2E
    tail = jnp.minimum(jnp.log2(2.0 / (1.0 - jnp.exp2(-slope))), math.log2(seq))
    dist = (2.0 * ub + (ATT_SKIP_BITS + tail)) / slope
    dist = jnp.ceil(jnp.minimum(dist, float(seq))).astype(jnp.int32)
    q0 = lax.broadcasted_iota(jnp.int32, (nq, 1), 0) * tq
    jmin = jnp.maximum(q0 - dist + 1, 0)
    jmax = jnp.minimum(q0 + (tq - 1) + dist - 1, seq - 1)
    shift = jnp.full((nq, 1), int(math.log2(ATT_TK)), jnp.int32)
    lo_ref[0, 0] = lax.shift_right_logical(jmin, shift)
    hi_ref[0, 0] = lax.shift_right_logical(jmax, shift) + 1
    fast_ref[0, 0] = (ub <= ATT_FAST_MAX_U).astype(jnp.int32)
    ub_ref[0, 0] = ub


def _attn_bounds(qkv, slopes):
    b, s, _ = qkv.shape
    h, dv = N_DIFF_HEADS, DIFF_V_DIM
    nq = s // ATT_TQ
    kern = functools.partial(_attn_bounds_kernel, seq=s)
    oblk = pl.BlockSpec((1, 1, nq, 1), lambda bi, hi: (bi, hi, 0, 0))
    ishape = jax.ShapeDtypeStruct((b, h, nq, 1), jnp.int32)
    return pl.pallas_call(
        kern,
        grid=(b, h),
        in_specs=[
            pl.BlockSpec((1, s, dv), lambda bi, hi: (bi, 0, hi)),
            pl.BlockSpec((1, s, dv), lambda bi, hi: (bi, 0, h + hi)),
            pl.BlockSpec((1, 1, 1), lambda bi, hi: (hi, 0, 0)),
        ],
        out_specs=[oblk, oblk, oblk, oblk],
        out_shape=[ishape, ishape, ishape, jax.ShapeDtypeStruct((b, h, nq, 1), F32)],
        compiler_params=_cparams(2),
        name="attn_bounds",
    )(qkv, qkv, slopes)


def _attn_kernel(lo_ref, hi_ref, fast_ref, q_ref, k_ref, v_ref, ub_ref, slope_ref,
                 lq1_ref, lk1_ref, lq2_ref, lk2_ref, sub_ref,
                 o_ref, vt_ref, bias_ref, u_ref, m_ref, l_ref, acc_ref, *, lam_init, seq):
    tq, tk, pair, nqp = ATT_TQ, ATT_TK, ATT_PAIR, ATT_NQP
    dh = DIFF_HEAD_DIM
    step = pl.program_id(2)
    slope = slope_ref[0] * LOG2E

    @pl.when(step == 0)
    def _():
        for c in range(seq // 512):
            vt_ref[:, c * 512:(c + 1) * 512] = v_ref[0, c * 512:(c + 1) * 512, :].T
        for s in range(2):
            rows = slice(s * tk, (s + 1) * tk)
            r = lax.broadcasted_iota(jnp.int32, (tk, 2 * tq), 0) + s * tk
            c = lax.broadcasted_iota(jnp.int32, (tk, 2 * tq), 1)
            c = jnp.where(c >= tq, c - tq, c)
            rc = (r - c).astype(F32)
            bias_ref[0, rows, :] = slope * rc
            bias_ref[1, rows, :] = -(slope * rc)
            for w in range(nqp):
                bias_ref[2 + w, rows, :] = -(slope * jnp.abs(rc - float(w * tq)))

    lam = (jnp.exp(jnp.sum(lq1_ref[...] * lk1_ref[...], axis=-1, keepdims=True))
           - jnp.exp(jnp.sum(lq2_ref[...] * lk2_ref[...], axis=-1, keepdims=True))
           + lam_init)

    def one_tile(t):
        qi = step * ATT_QSUB + t
        tile = (pl.program_id(0) * N_DIFF_HEADS + pl.program_id(1)) * (seq // tq) + qi
        lo, hi = lo_ref[tile], hi_ref[tile]
        lt_ref, at_ref = l_ref.at[t], acc_ref.at[t]

        qt = q_ref[0, t * tq:(t + 1) * tq, :].T
        z = jnp.zeros((dh, tq), BF16)
        qbd = jnp.concatenate([jnp.concatenate([qt[:dh], z], axis=1),
                               jnp.concatenate([z, qt[dh:]], axis=1)], axis=0)

        lt_ref[...] = jnp.zeros(lt_ref.shape, F32)
        at_ref[...] = jnp.zeros(at_ref.shape, F32)

        kd = qi // nqp
        w = qi % nqp

        def mode(n):
            kp = n // 2
            delta = (kp * pair - qi * tq).astype(F32)
            idx = jnp.where(kp < kd, 0, jnp.where(kp > kd, 1, 2 + w))
            sgn = jnp.where(kp < kd, delta, jnp.where(kp > kd, -delta, 0.0))
            return idx, pl.multiple_of((n % 2) * tk, tk), slope * sgn

        def scores(n):
            idx, row0, _ = mode(n)
            k0 = pl.multiple_of(n * tk, tk)
            kblk = k_ref[0, pl.ds(k0, tk), :]
            u = jnp.dot(kblk, qbd, preferred_element_type=F32)
            return u + bias_ref[idx, pl.ds(row0, tk), :]

        def values(n):
            return vt_ref[:, pl.ds(pl.multiple_of(n * tk, tk), tk)]

        def pipelined(qk_phase, sm_phase, carry0, trips):
            def body(n, carry_a):
                carry_b = qk_phase(n + 1, 1)
                sm_phase(n, 0, carry_a)
                carry_a_next = qk_phase(jnp.minimum(n + 2, hi - 1), 0)
                sm_phase(n + 1, 1, carry_b)
                return carry_a_next

            start, carry = lo, carry0
            for npt in trips:
                ntrip = (hi - start) // (2 * npt)

                def trip(i, c, start=start, npt=npt):
                    for j in range(npt):
                        c = body(start + 2 * (npt * i + j), c)
                    return c

                carry = lax.fori_loop(0, ntrip, trip, carry)
                start = start + 2 * npt * ntrip
            ndouble = (hi - start) // 2
            carry = lax.fori_loop(0, ndouble, lambda i, c, start=start: body(start + 2 * i, c), carry)
            last = start + 2 * ndouble

            @pl.when(last < hi)
            def _():
                sm_phase(last, 0, carry)

        def fixed_reference():
            ub = ub_ref[t]

            def qk_phase(n, slot):
                u_ref[slot] = scores(n)
                return 0

            def sm_phase(n, slot, _):
                cst = mode(n)[2]
                p = jnp.exp2(u_ref[slot] + (cst - ub))
                lt_ref[...] += jnp.sum(p, axis=0, keepdims=True)
                at_ref[...] += jnp.dot(values(n), p.astype(BF16), preferred_element_type=F32)

            pipelined(qk_phase, sm_phase, 0, ATT_TRIPS_FAST)

        def running_max():
            m_ref[...] = jnp.full(m_ref.shape, -jnp.inf, F32)

            def qk_phase(n, slot):
                v = scores(n)
                u_ref[slot] = v
                return jnp.max(v, axis=0, keepdims=True)

            def sm_phase(n, slot, mx):
                cst = mode(n)[2]
                m_old = m_ref[...]
                m_new = jnp.maximum(m_old, mx + cst)
                alpha = jnp.exp2(m_old - m_new)
                p = jnp.exp2(u_ref[slot] - (m_new - cst))
                lt_ref[...] = alpha * lt_ref[...] + jnp.sum(p, axis=0, keepdims=True)
                m_ref[...] = m_new
                at_ref[...] = alpha * at_ref[...] + jnp.dot(values(n), p.astype(BF16),
                                                            preferred_element_type=F32)

            pipelined(qk_phase, sm_phase, jnp.max(u_ref[0], axis=0, keepdims=True), ATT_TRIPS_SAFE)

        u_ref[0] = scores(lo)
        lax.cond(fast_ref[tile] == 1, fixed_reference, running_max)

        acc = at_ref[...]
        l = lt_ref[...]
        ot = acc[:, :tq] / l[:, :tq] - lam * (acc[:, tq:] / l[:, tq:])
        ms = jnp.mean(ot * ot, axis=0, keepdims=True)
        ot = ot * lax.rsqrt(ms + EPS) * sub_ref[...] * (1.0 - lam_init)
        o_ref[0, t * tq:(t + 1) * tq, :] = ot.T.astype(o_ref.dtype)

    for t in range(ATT_QSUB):
        one_tile(t)


def _diff_attention(qkv, slopes, lq1, lk1, lq2, lk2, subln_col, lam_init):
    b, s, _ = qkv.shape
    h, dv = N_DIFF_HEADS, DIFF_V_DIM
    tq, tk, qsub = ATT_TQ, ATT_TK, ATT_QSUB
    nq = s // tq
    assert s % ATT_PAIR == 0 and nq % qsub == 0
    nsteps = nq // qsub
    lo, hi, fast, ub = _attn_bounds(qkv, slopes)
    kern = functools.partial(_attn_kernel, lam_init=lam_init, seq=s)
    vec = pl.BlockSpec((1, DIFF_HEAD_DIM), lambda bi, hi, qi, *_: (0, 0))
    grid_spec = pltpu.PrefetchScalarGridSpec(
        num_scalar_prefetch=3,
        grid=(b, h, nsteps),
        in_specs=[
            pl.BlockSpec((1, qsub * tq, dv), lambda bi, hi, qi, *_: (bi, qi, hi)),
            pl.BlockSpec((1, s, dv), lambda bi, hi, qi, *_: (bi, 0, h + hi)),
            pl.BlockSpec((1, s, dv), lambda bi, hi, qi, *_: (bi, 0, 2 * h + hi)),
            pl.BlockSpec((qsub, 1, 1), lambda bi, hi, qi, *_: ((bi * h + hi) * nsteps + qi, 0, 0)),
            pl.BlockSpec((1, 1, 1), lambda bi, hi, qi, *_: (hi, 0, 0)),
            vec, vec, vec, vec,
            pl.BlockSpec((dv, 1), lambda bi, hi, qi, *_: (0, 0)),
        ],
        out_specs=pl.BlockSpec((1, qsub * tq, dv), lambda bi, hi, qi, *_: (bi, qi, hi)),
        scratch_shapes=[
            pltpu.VMEM((dv, s), BF16),
            pltpu.VMEM((2 + ATT_NQP, ATT_PAIR, 2 * tq), F32),
            pltpu.VMEM((2, tk, 2 * tq), F32),
            pltpu.VMEM((1, 2 * tq), F32),
            pltpu.VMEM((qsub, 1, 2 * tq), F32),
            pltpu.VMEM((qsub, dv, 2 * tq), F32),
        ],
    )
    return pl.pallas_call(
        kern,
        grid_spec=grid_spec,
        out_shape=jax.ShapeDtypeStruct((b, s, h * dv), BF16),
        compiler_params=_cparams(3),
        name="diff_attention",
    )(lo.reshape(-1), hi.reshape(-1), fast.reshape(-1),
      qkv, qkv, qkv, ub.reshape(-1, 1, 1), slopes, lq1, lk1, lq2, lk2, subln_col)


def _proj_res_kernel(x_ref, a_ref, w_ref, gate_ref, o_ref):
    y = jnp.dot(a_ref[0], w_ref[...], preferred_element_type=F32)
    o_ref[0] = x_ref[0] + gate_ref[0] * y


def _proj_residual(x, a, w_bf16, gate, tm):
    b, s, d = x.shape
    kdim = a.shape[-1]
    return pl.pallas_call(
        _proj_res_kernel,
        grid=(b, s // tm),
        in_specs=[
            pl.BlockSpec((1, tm, d), lambda i, j: (i, j, 0)),
            pl.BlockSpec((1, tm, kdim), lambda i, j: (i, j, 0)),
            pl.BlockSpec((kdim, d), lambda i, j: (0, 0)),
            pl.BlockSpec((1, 1, d), lambda i, j: (i, 0, 0)),
        ],
        out_specs=pl.BlockSpec((1, tm, d), lambda i, j: (i, j, 0)),
        out_shape=jax.ShapeDtypeStruct((b, s, d), F32),
        compiler_params=_cparams(2),
        name="proj_residual",
    )(x, a, w_bf16, gate)


def _routing_gates_t(lt):
    ng, ne = N_EXPERT_GROUPS, EXPERTS_PER_GROUP
    gl = [lt[g:g + 1, :] for g in range(ng)]
    gmax = functools.reduce(jnp.maximum, gl)
    den = functools.reduce(lambda a, b: a + b, [jnp.exp(v - gmax) for v in gl])
    g_w = 1.0 / den
    sel = []
    taken = jnp.zeros(gmax.shape, jnp.bool_)
    for g in range(ng):
        s = jnp.logical_and(gl[g] == gmax, jnp.logical_not(taken))
        taken = jnp.logical_or(taken, s)
        sel.append(s)
    ev = []
    for e in range(ne):
        v = jnp.zeros_like(gmax)
        for g in range(ng):
            row = ng + ne * g + e
            v = jnp.where(sel[g], lt[row:row + 1, :], v)
        ev.append(v)
    top1 = functools.reduce(jnp.maximum, ev)
    is1 = []
    taken = jnp.zeros(gmax.shape, jnp.bool_)
    for e in range(ne):
        s = jnp.logical_and(ev[e] == top1, jnp.logical_not(taken))
        taken = jnp.logical_or(taken, s)
        is1.append(s)
    rest = [jnp.where(is1[e], -jnp.inf, ev[e]) for e in range(ne)]
    top2 = functools.reduce(jnp.maximum, rest)
    is2 = []
    taken = jnp.zeros(gmax.shape, jnp.bool_)
    for e in range(ne):
        s = jnp.logical_and(jnp.logical_and(rest[e] == top2, jnp.logical_not(is1[e])),
                            jnp.logical_not(taken))
        taken = jnp.logical_or(taken, s)
        is2.append(s)
    ex = jnp.exp(top2 - top1)
    w1 = 1.0 / (1.0 + ex)
    w2 = ex / (1.0 + ex)
    ew = [jnp.where(is1[e], w1, jnp.where(is2[e], w2, 0.0)) for e in range(ne)]
    rows = [jnp.where(sel[g], g_w * ew[e], 0.0) for g in range(ng) for e in range(ne)]
    return jnp.concatenate(rows, axis=0), sel


MOE_TM = 1024
MOE_CH = 128
MOE_NB = 11
MOE_POS_LANE = 3 * N_EXPERTS


def _route_sort_kernel(x_ref, g_ref, sh_ref, sc_ref, wr_ref, br_ref, tri_ref,
                       hs_ref, gs_ref, meta_ref, gid_ref):
    ng, ch, nb = N_EXPERT_GROUPS, MOE_CH, MOE_NB
    tm = x_ref.shape[0]
    h32 = _rms_mod(x_ref[...], g_ref[...], sh_ref[0], sc_ref[0])
    hb = h32.astype(BF16)
    hl = (h32 - hb.astype(F32)).astype(BF16)
    wr = wr_ref[...]
    wb = wr.astype(BF16)
    wl = (wr - wb.astype(F32)).astype(BF16)
    nt = (((1,), (1,)), ((), ()))
    l2 = lax.dot_general(jnp.concatenate([wb, wl], axis=0), hb, nt, preferred_element_type=F32)
    lt = (l2[:LANES] + l2[LANES:] + lax.dot_general(wb, hl, nt, preferred_element_type=F32)
          + br_ref[...])
    gates_t, sel = _routing_gates_t(lt)
    oh = jnp.concatenate([m.astype(F32) for m in sel] + [jnp.zeros((16 - ng, tm), F32)], axis=0)
    cnt = jnp.dot(oh.astype(BF16), tri_ref[...], preferred_element_type=F32)
    rank = jnp.sum(oh * cnt, axis=0, keepdims=True)
    n = jnp.sum(oh, axis=1, keepdims=True)
    nch = jnp.floor((n + float(ch - 1)) * (1.0 / ch))
    first = [jnp.zeros((1, 1), F32)]
    for g in range(ng):
        first.append(first[-1] + nch[g:g + 1, :])
    seg = functools.reduce(lambda a, b: a + b,
                           [oh[g:g + 1, :] * (first[g] * float(ch)) for g in range(ng)])
    pos = seg + rank
    rows = lax.broadcasted_iota(jnp.int32, (nb * ch, tm), 0)
    perm = jnp.where(rows == pos.astype(jnp.int32), 1.0, 0.0).astype(BF16)
    hs_ref[0] = jnp.dot(perm, hb, preferred_element_type=F32).astype(BF16)
    hi = gates_t.astype(BF16).astype(F32)
    r1 = gates_t - hi
    mid = r1.astype(BF16).astype(F32)
    lo = r1 - mid
    meta_t = jnp.concatenate([hi, mid, lo, pos, jnp.zeros((LANES - MOE_POS_LANE - 1, tm), F32)], axis=0)
    meta = meta_t.T
    meta_ref[...] = meta
    gs_ref[0] = jnp.dot(perm, meta.astype(BF16), preferred_element_type=F32)
    lane = lax.broadcasted_iota(jnp.int32, (1, LANES), 1).astype(F32)
    gid = functools.reduce(lambda a, b: a + b,
                           [jnp.where(lane >= first[g + 1], 1.0, 0.0) for g in range(ng)])
    gid_ref[0] = gid.astype(jnp.int32)


def _route_sort(x2, g, shift, scale, wr, br, tri, tiles_per_batch):
    t, d = x2.shape
    tm, rows = MOE_TM, MOE_NB * MOE_CH
    nt = t // tm
    bvec = lambda i: (i // tiles_per_batch, 0, 0)
    return pl.pallas_call(
        _route_sort_kernel,
        grid=(nt,),
        in_specs=[
            pl.BlockSpec((tm, d), lambda i: (i, 0)),
            pl.BlockSpec((1, d), lambda i: (0, 0)),
            pl.BlockSpec((1, 1, d), bvec),
            pl.BlockSpec((1, 1, d), bvec),
            pl.BlockSpec((LANES, d), lambda i: (0, 0)),
            pl.BlockSpec((LANES, 1), lambda i: (0, 0)),
            pl.BlockSpec((tm, tm), lambda i: (0, 0)),
        ],
        out_specs=[
            pl.BlockSpec((1, rows, d), lambda i: (i, 0, 0)),
            pl.BlockSpec((1, rows, LANES), lambda i: (i, 0, 0)),
            pl.BlockSpec((tm, LANES), lambda i: (i, 0)),
            pl.BlockSpec((1, 1, LANES), lambda i: (i, 0, 0)),
        ],
        out_shape=[
            jax.ShapeDtypeStruct((nt, rows, d), BF16),
            jax.ShapeDtypeStruct((nt, rows, LANES), F32),
            jax.ShapeDtypeStruct((t, LANES), F32),
            jax.ShapeDtypeStruct((nt, 1, LANES), jnp.int32),
        ],
        compiler_params=_cparams(1),
        name="moe_route_sort",
    )(x2, g, shift, scale, wr, br, tri)


def _experts_kernel(order_ref, grp_ref, first_ref, hs_ref, gs_ref, wg_ref, wu_ref, wd_ref, ys_ref,
                    wgb_ref, wub_ref, wdb_ref):
    ne = EXPERTS_PER_GROUP
    step = pl.program_id(0)
    grp = grp_ref[step]

    @pl.when(first_ref[step] == 1)
    def _():
        wgb_ref[...] = wg_ref[0].astype(BF16)
        wub_ref[...] = wu_ref[0].astype(BF16)
        wdb_ref[...] = wd_ref[0].astype(BF16)

    @pl.when(grp < N_EXPERT_GROUPS)
    def _():
        h = hs_ref[0]
        gs = gs_ref[0]
        lane = lax.broadcasted_iota(jnp.int32, gs.shape, 1)
        hid = []
        for e in range(ne):
            idx = grp * ne + e
            pick = jnp.logical_or(jnp.logical_or(lane == idx, lane == idx + N_EXPERTS),
                                  lane == idx + 2 * N_EXPERTS)
            gate = jnp.sum(jnp.where(pick, gs, 0.0), axis=1, keepdims=True)
            a = jnp.dot(h, wgb_ref[e], preferred_element_type=F32)
            u = jnp.dot(h, wub_ref[e], preferred_element_type=F32)
            hid.append(((a / (1.0 + jnp.exp(-a))) * u * gate).astype(BF16))
        hid = jnp.concatenate(hid, axis=1)
        ys_ref[0] = jnp.dot(hid, wdb_ref[...], preferred_element_type=F32).astype(ys_ref.dtype)

    @pl.when(grp >= N_EXPERT_GROUPS)
    def _():
        ys_ref[0] = jnp.zeros(ys_ref.shape[1:], ys_ref.dtype)


def _experts(order, grp, first, hs, gs, wg4, wu4, wd4, layer):
    nchunks, ch, d = hs.shape
    ne, f = wg4.shape[1], wg4.shape[-1]
    ng = N_EXPERT_GROUPS
    wsel = lambda s, order, grp, first: (layer * ng + jnp.minimum(grp[s], ng - 1), 0, 0, 0)
    wsel3 = lambda s, order, grp, first: (layer * ng + jnp.minimum(grp[s], ng - 1), 0, 0)
    chunk = lambda s, order, grp, first: (order[s], 0, 0)
    once = pl.Buffered(1)
    grid_spec = pltpu.PrefetchScalarGridSpec(
        num_scalar_prefetch=3,
        grid=(nchunks,),
        in_specs=[
            pl.BlockSpec((1, ch, d), chunk),
            pl.BlockSpec((1, ch, LANES), chunk),
            pl.BlockSpec((1, ne, d, f), wsel),
            pl.BlockSpec((1, ne, d, f), wsel),
            pl.BlockSpec((1, ne * f, d), wsel3, pipeline_mode=once),
        ],
        out_specs=pl.BlockSpec((1, ch, d), chunk),
        scratch_shapes=[
            pltpu.VMEM((ne, d, f), BF16),
            pltpu.VMEM((ne, d, f), BF16),
            pltpu.VMEM((ne * f, d), BF16),
        ],
    )
    return pl.pallas_call(
        _experts_kernel,
        grid_spec=grid_spec,
        out_shape=jax.ShapeDtypeStruct((nchunks, ch, d), BF16),
        compiler_params=_cparams(1),
        name="moe_experts",
    )(order, grp, first, hs, gs, wg4, wu4, wd4)


def _unsort_kernel(x_ref, ys_ref, meta_ref, gate_ref, gf_ref, o_ref, *, final_norm):
    rows = ys_ref.shape[1]
    tm = x_ref.shape[0]
    pos = meta_ref[:, MOE_POS_LANE:MOE_POS_LANE + 1].astype(jnp.int32)
    cols = lax.broadcasted_iota(jnp.int32, (tm, rows), 1)
    perm_t = jnp.where(cols == pos, 1.0, 0.0).astype(BF16)
    y = jnp.dot(perm_t, ys_ref[0], preferred_element_type=F32)
    y = x_ref[...] + gate_ref[0] * y
    if final_norm:
        y = y * lax.rsqrt(jnp.mean(y * y, axis=-1, keepdims=True) + EPS) * gf_ref[...]
    o_ref[...] = y


def _unsort_residual(x2, ys, meta, gate, gfinal, final_norm, tiles_per_batch):
    t, d = x2.shape
    tm = MOE_TM
    nt, rows, _ = ys.shape
    kern = functools.partial(_unsort_kernel, final_norm=final_norm)
    return pl.pallas_call(
        kern,
        grid=(nt,),
        in_specs=[
            pl.BlockSpec((tm, d), lambda i: (i, 0)),
            pl.BlockSpec((1, rows, d), lambda i: (i, 0, 0)),
            pl.BlockSpec((tm, LANES), lambda i: (i, 0)),
            pl.BlockSpec((1, 1, d), lambda i: (i // tiles_per_batch, 0, 0)),
            pl.BlockSpec((1, d), lambda i: (0, 0)),
        ],
        out_specs=pl.BlockSpec((tm, d), lambda i: (i, 0)),
        out_shape=jax.ShapeDtypeStruct((t, d), F32),
        compiler_params=_cparams(1),
        name="moe_unsort_residual",
    )(x2, ys, meta, gate, gfinal)


def moe_layer(x, g, sh2, sc2, g2, w_rg, b_rg, w_re, b_re, w_gate, w_up, w_down, layer, gfinal, final_norm):
    b, s, d = x.shape
    ng, ne = N_EXPERT_GROUPS, EXPERTS_PER_GROUP
    f = w_gate.shape[-1]
    assert s % MOE_TM == 0
    nr = ng + N_EXPERTS
    wr = jnp.concatenate([w_rg.T, w_re.reshape(d, N_EXPERTS).T,
                          jnp.zeros((LANES - nr, d), F32)], axis=0)
    br = jnp.concatenate([b_rg, b_re.reshape(N_EXPERTS),
                          jnp.zeros((LANES - nr,), F32)]).reshape(LANES, 1)
    ti = np.arange(MOE_TM)
    tri = jnp.asarray(ti[:, None] < ti[None, :], F32).astype(BF16)
    x2 = x.reshape(b * s, d)
    tpb = s // MOE_TM
    hs, gs, meta, gid = _route_sort(x2, g.reshape(1, d), sh2, sc2, wr, br, tri, tpb)
    nt = hs.shape[0]
    gid_flat = gid[:, 0, :MOE_NB].reshape(-1)
    order = jnp.argsort(gid_flat, stable=True).astype(jnp.int32)
    grp = gid_flat[order]
    first = jnp.concatenate([jnp.ones((1,), jnp.int32), (grp[1:] != grp[:-1]).astype(jnp.int32)])
    nl = w_gate.shape[0]
    ys = _experts(order, grp, first,
                  hs.reshape(nt * MOE_NB, MOE_CH, d), gs.reshape(nt * MOE_NB, MOE_CH, LANES),
                  w_gate.reshape(nl * ng, ne, d, f), w_up.reshape(nl * ng, ne, d, f),
                  w_down.reshape(nl * ng, ne * f, d), layer)
    out = _unsort_residual(x2, ys.reshape(nt, MOE_NB * MOE_CH, d), meta, g2,
                           gfinal.reshape(1, d), final_norm, tpb)
    return out.reshape(b, s, d)


def _fft1_kernel(u_ref, w_ref, twr_ref, twi_ref, o_ref, *, n_sub, width):
    n1 = FFT_N1
    y = jnp.dot(w_ref[...], u_ref[0], preferred_element_type=F32)
    for j in range(n_sub):
        cols = slice(j * width, (j + 1) * width)
        yr, yi = y[:n1, cols], y[n1:, cols]
        twr, twi = twr_ref[j], twi_ref[j]
        o_ref[0, :n1, cols] = (yr * twr - yi * twi).astype(o_ref.dtype)
        o_ref[0, n1:, cols] = (yr * twi + yi * twr).astype(o_ref.dtype)


def _fft_stage1(u3, w1, twr, twi, width):
    b, n1, cols = u3.shape
    n_sub = 8
    tn = n_sub * width
    kern = functools.partial(_fft1_kernel, n_sub=n_sub, width=width)
    return pl.pallas_call(
        kern,
        grid=(b, cols // tn),
        in_specs=[
            pl.BlockSpec((1, n1, tn), lambda i, j: (i, 0, j)),
            pl.BlockSpec((2 * n1, n1), lambda i, j: (0, 0)),
            pl.BlockSpec((n_sub, n1, 1), lambda i, j: (j, 0, 0)),
            pl.BlockSpec((n_sub, n1, 1), lambda i, j: (j, 0, 0)),
        ],
        out_specs=pl.BlockSpec((1, 2 * n1, tn), lambda i, j: (i, 0, j)),
        out_shape=jax.ShapeDtypeStruct((b, 2 * n1, cols), BF16),
        compiler_params=_cparams(2),
        name="fft_stage1",
    )(u3, w1, twr, twi)


def _fft2_mix_kernel(y_ref, g_ref, x_ref, cc_ref, sc_ref, w_ref, gate_ref, o_ref):
    kp, n2 = FFT_K1_PER_STEP, FFT_N2
    width = y_ref.shape[-1]
    gd = cc_ref.shape[0]
    half = kp * n2
    y = y_ref[0].reshape(2 * half, width)
    r = jnp.dot(g_ref[...], y, preferred_element_type=F32)
    ar, ai = r[:half].astype(BF16), r[half:].astype(BF16)
    parts = []
    for g in range(N_FOURIER_GROUPS):
        cols = slice(g * gd, (g + 1) * gd)
        parts.append(jnp.dot(ar[:, cols], cc_ref[...], preferred_element_type=F32)
                     + jnp.dot(ai[:, cols], sc_ref[...], preferred_element_type=F32))
    f = jnp.concatenate(parts, axis=1).astype(BF16)
    yv = jnp.dot(f, w_ref[...], preferred_element_type=F32)
    out = x_ref[0].reshape(half, width) + gate_ref[0] * yv
    o_ref[0] = out.reshape(n2, kp, width)


def _fft2_mix(y5, gmat, x4, cc, sc, w_bf16, gate):
    b, _, n1, n2, width = y5.shape
    kp = FFT_K1_PER_STEP
    gd = cc.shape[0]
    return pl.pallas_call(
        _fft2_mix_kernel,
        grid=(b, n1 // kp),
        in_specs=[
            pl.BlockSpec((1, 2, kp, n2, width), lambda i, j: (i, 0, j, 0, 0)),
            pl.BlockSpec((2 * kp * n2, 2 * kp * n2), lambda i, j: (0, 0)),
            pl.BlockSpec((1, n2, kp, width), lambda i, j: (i, 0, j, 0)),
            pl.BlockSpec((gd, gd), lambda i, j: (0, 0)),
            pl.BlockSpec((gd, gd), lambda i, j: (0, 0)),
            pl.BlockSpec((width, width), lambda i, j: (0, 0)),
            pl.BlockSpec((1, 1, width), lambda i, j: (i, 0, 0)),
        ],
        out_specs=pl.BlockSpec((1, n2, kp, width), lambda i, j: (i, 0, j, 0)),
        out_shape=jax.ShapeDtypeStruct((b, n2, n1, width), F32),
        compiler_params=_cparams(2),
        name="fft_stage2_mix",
    )(y5, gmat, x4, cc, sc, w_bf16, gate)


def _dft_tables(seq, gd):
    n1, n2, kp = FFT_N1, FFT_N2, FFT_K1_PER_STEP

    def cs(num, den):
        ang = (2.0 * math.pi / den) * (num % den).astype(np.float64)
        return np.cos(ang), np.sin(ang)

    i1 = np.arange(n1)
    c1, s1 = cs(np.outer(i1, i1), n1)
    w1 = np.concatenate([c1, -s1], axis=0)
    ct, st = cs(np.outer(np.arange(n2), i1), seq)
    twr, twi = ct[:, :, None], -st[:, :, None]
    i2 = np.arange(n2)
    c2, s2 = cs(np.outer(i2, i2), n2)
    m3 = np.block([[c2, s2], [-s2, c2]])
    m3 = m3.reshape(2, n2, 2, n2)
    gmat = np.zeros((2, n2, kp, 2, kp, n2))
    for j in range(kp):
        gmat[:, :, j, :, j, :] = m3
    gmat = gmat.reshape(2 * n2 * kp, 2 * kp * n2)
    ic = np.arange(gd)
    cc, sc = cs(np.outer(ic, ic), gd)
    norm = 1.0 / math.sqrt(seq * gd)
    return tuple(jnp.asarray(t, F32) for t in (w1, twr, twi, gmat, cc * norm, sc * norm))


def kernel(x, c, ada_w, ada_b, norm_mix, norm_ffn, attn_w_in, attn_lam_q1, attn_lam_k1, attn_lam_q2, attn_lam_k2, attn_subln, attn_w_out, fourier_w_in, fourier_w_out, router_group_w, router_group_b, router_expert_w, router_expert_b, expert_w_gate, expert_w_up, expert_w_down, norm_final):
    b, s, d = x.shape
    depth = ada_w.shape[0]
    assert s == FFT_N1 * FFT_N2 and depth == 2

    c8 = jnp.zeros((8, d), F32).at[:b].set(c)
    mod = _ada(c8, ada_w, ada_b.reshape(depth, 1, N_ADA * d))[:, :b]
    mod = mod.reshape(depth, b, N_ADA, 1, d)

    slopes = (2.0 ** (-8.0 * jnp.arange(1, N_DIFF_HEADS + 1, dtype=F32) / N_DIFF_HEADS)
              ).reshape(N_DIFF_HEADS, 1, 1)
    gd = d // N_FOURIER_GROUPS
    w1, twr, twi, gmat, cc, sc = _dft_tables(s, gd)
    w1, gmat, cc, sc = (t.astype(BF16) for t in (w1, gmat, cc, sc))

    for i in range(depth):
        sh1, sc1, g1, sh2, sc2, g2 = [mod[i, :, k] for k in range(N_ADA)]
        j = i // 2
        if i % 2 == 0:
            qscale = jnp.concatenate([jnp.full((d,), LOG2E * DIFF_HEAD_DIM ** -0.5, F32),
                                      jnp.ones((2 * d,), F32)])
            w_in = (attn_w_in[j] * qscale).astype(BF16)
            qkv = _norm_mod_matmul(x, norm_mix[i].reshape(1, d), sh1, sc1, w_in, tm=512)
            lam_init = 0.8 - 0.6 * math.exp(-0.3 * i)
            o = _diff_attention(qkv, slopes,
                                attn_lam_q1[j].reshape(1, -1), attn_lam_k1[j].reshape(1, -1),
                                attn_lam_q2[j].reshape(1, -1), attn_lam_k2[j].reshape(1, -1),
                                attn_subln[j].reshape(-1, 1), lam_init)
            x = _proj_residual(x, o, attn_w_out[j].astype(BF16), g1, tm=512)
        else:
            u = _norm_mod_matmul(x, norm_mix[i].reshape(1, d), sh1, sc1,
                                 fourier_w_in[j].astype(BF16), tm=512)
            y = _fft_stage1(u.reshape(b, FFT_N1, FFT_N2 * d), w1, twr, twi, d)
            x = _fft2_mix(y.reshape(b, 2, FFT_N1, FFT_N2, d), gmat, x.reshape(b, FFT_N2, FFT_N1, d),
                          cc, sc, fourier_w_out[j].astype(BF16), g1).reshape(b, s, d)

        x = moe_layer(x, norm_ffn[i], sh2, sc2, g2, router_group_w[i], router_group_b[i],
                      router_expert_w[i], router_expert_b[i], expert_w_gate, expert_w_up,
                      expert_w_down, i, norm_final, i == depth - 1)
    return x
```

```python
import functools
import math

import numpy as np
import jax
import jax.numpy as jnp
from jax import lax
from jax.experimental import pallas as pl
from jax.experimental.pallas import tpu as pltpu

F32 = jnp.float32
BF16 = jnp.bfloat16

EPS = 1e-6
N_DIFF_HEADS = 8
DIFF_HEAD_DIM = 64
DIFF_V_DIM = 128
N_FOURIER_GROUPS = 4
N_EXPERT_GROUPS = 4
EXPERTS_PER_GROUP = 4
N_EXPERTS = 16
N_ADA = 6

LANES = 128
VMEM_LIMIT_BYTES = 56 * 1024 * 1024

FFT_N1 = 128
FFT_N2 = 64
FFT_K1_PER_STEP = 8


def _cparams(n_axes):
    return pltpu.CompilerParams(
        dimension_semantics=("arbitrary",) * n_axes,
        vmem_limit_bytes=VMEM_LIMIT_BYTES,
    )


def _rms_mod(x, g, shift, scale):
    y = x * lax.rsqrt(jnp.mean(x * x, axis=-1, keepdims=True) + EPS)
    return (y * g) * (1.0 + scale) + shift


def _ada_kernel(c_ref, w_ref, b_ref, o_ref):
    c = c_ref[...]
    cond = c / (1.0 + jnp.exp(-c))
    o_ref[0] = jnp.dot(cond, w_ref[0], precision=lax.Precision.HIGHEST,
                       preferred_element_type=F32) + b_ref[0]


def _ada(c8, ada_w, ada_b3):
    depth, d, n = ada_w.shape
    tn = 1536
    return pl.pallas_call(
        _ada_kernel,
        grid=(depth, n // tn),
        in_specs=[
            pl.BlockSpec((8, d), lambda i, j: (0, 0)),
            pl.BlockSpec((1, d, tn), lambda i, j: (i, 0, j)),
            pl.BlockSpec((1, 1, tn), lambda i, j: (i, 0, j)),
        ],
        out_specs=pl.BlockSpec((1, 8, tn), lambda i, j: (i, 0, j)),
        out_shape=jax.ShapeDtypeStruct((depth, 8, n), F32),
        compiler_params=_cparams(2),
        name="ada_mod",
    )(c8, ada_w, ada_b3)


def _nmm_kernel(x_ref, g_ref, sh_ref, sc_ref, w_ref, o_ref):
    h = _rms_mod(x_ref[0], g_ref[...], sh_ref[0], sc_ref[0])
    o_ref[0] = jnp.dot(h.astype(BF16), w_ref[...],
                       preferred_element_type=F32).astype(o_ref.dtype)


def _norm_mod_matmul(x, g, shift, scale, w_bf16, tm):
    b, s, d = x.shape
    n = w_bf16.shape[1]
    return pl.pallas_call(
        _nmm_kernel,
        grid=(b, s // tm),
        in_specs=[
            pl.BlockSpec((1, tm, d), lambda i, j: (i, j, 0)),
            pl.BlockSpec((1, d), lambda i, j: (0, 0)),
            pl.BlockSpec((1, 1, d), lambda i, j: (i, 0, 0)),
            pl.BlockSpec((1, 1, d), lambda i, j: (i, 0, 0)),
            pl.BlockSpec((d, n), lambda i, j: (0, 0)),
        ],
        out_specs=pl.BlockSpec((1, tm, n), lambda i, j: (i, j, 0)),
        out_shape=jax.ShapeDtypeStruct((b, s, n), BF16),
        compiler_params=_cparams(2),
        name="norm_mod_matmul",
    )(x, g, shift, scale, w_bf16)


ATT_TQ = 256
ATT_TK = 512
ATT_PAIR = 2 * ATT_TK
ATT_NQP = ATT_PAIR // ATT_TQ
ATT_QSUB = 8
ATT_TRIPS_FAST = (4, 2)
ATT_TRIPS_SAFE = (2,)
LOG2E = 1.4426950408889634
ATT_SKIP_BITS = 30.0
ATT_FAST_MAX_U = 36.0


def _attn_bounds_kernel(q_ref, k_ref, slope_ref, lo_ref, hi_ref, fast_ref, ub_ref, *, seq):
    tq = ATT_TQ
    nq = seq // tq
    dd = 2 * DIFF_HEAD_DIM
    r = lax.broadcasted_iota(jnp.int32, (dd, dd), 0)
    c = lax.broadcasted_iota(jnp.int32, (dd, dd), 1)
    halves = jnp.where(c == r // DIFF_HEAD_DIM, 1.0, 0.0).astype(BF16)

    def half_norm2(ref):
        x = ref[0].astype(F32)
        return jnp.dot((x * x).astype(BF16), halves, preferred_element_type=F32)

    qn2 = jnp.max(jnp.max(half_norm2(q_ref).reshape(nq, tq, dd), axis=1), axis=1, keepdims=True)
    kn2 = jnp.max(jnp.max(half_norm2(k_ref), axis=0, keepdims=True), axis=1, keepdims=True)
    ub = jnp.sqrt(qn2 * kn2) * (1.0 + 2.0 ** -6)
    slope = slope_ref[0] * LOG2E
    tail = jnp.minimum(jnp.log2(2.0 / (1.0 - jnp.exp2(-slope))), math.log2(seq))
    dist = (2.0 * ub + (ATT_SKIP_BITS + tail)) / slope
    dist = jnp.ceil(jnp.minimum(dist, float(seq))).astype(jnp.int32)
    q0 = lax.broadcasted_iota(jnp.int32, (nq, 1), 0) * tq
    jmin = jnp.maximum(q0 - dist + 1, 0)
    jmax = jnp.minimum(q0 + (tq - 1) + dist - 1, seq - 1)
    shift = jnp.full((nq, 1), int(math.log2(ATT_TK)), jnp.int32)
    lo_ref[0, 0] = lax.shift_right_logical(jmin, shift)
    hi_ref[0, 0] = lax.shift_right_logical(jmax, shift) + 1
    fast_ref[0, 0] = (ub <= ATT_FAST_MAX_U).astype(jnp.int32)
    ub_ref[0, 0] = ub


def _attn_bounds(qkv, slopes):
    b, s, _ = qkv.shape
    h, dv = N_DIFF_HEADS, DIFF_V_DIM
    nq = s // ATT_TQ
    kern = functools.partial(_attn_bounds_kernel, seq=s)
    oblk = pl.BlockSpec((1, 1, nq, 1), lambda bi, hi: (bi, hi, 0, 0))
    ishape = jax.ShapeDtypeStruct((b, h, nq, 1), jnp.int32)
    return pl.pallas_call(
        kern,
        grid=(b, h),
        in_specs=[
            pl.BlockSpec((1, s, dv), lambda bi, hi: (bi, 0, hi)),
            pl.BlockSpec((1, s, dv), lambda bi, hi: (bi, 0, h + hi)),
            pl.BlockSpec((1, 1, 1), lambda bi, hi: (hi, 0, 0)),
        ],
        out_specs=[oblk, oblk, oblk, oblk],
        out_shape=[ishape, ishape, ishape, jax.ShapeDtypeStruct((b, h, nq, 1), F32)],
        compiler_params=_cparams(2),
        name="attn_bounds",
    )(qkv, qkv, slopes)


def _attn_kernel(lo_ref, hi_ref, fast_ref, q_ref, k_ref, v_ref, ub_ref, slope_ref,
                 lq1_ref, lk1_ref, lq2_ref, lk2_ref, sub_ref,
                 o_ref, vt_ref, bias_ref, u_ref, m_ref, l_ref, acc_ref, *, lam_init, seq):
    tq, tk, pair, nqp = ATT_TQ, ATT_TK, ATT_PAIR, ATT_NQP
    dh = DIFF_HEAD_DIM
    step = pl.program_id(2)
    slope = slope_ref[0] * LOG2E

    @pl.when(step == 0)
    def _():
        for c in range(seq // 512):
            vt_ref[:, c * 512:(c + 1) * 512] = v_ref[0, c * 512:(c + 1) * 512, :].T
        for s in range(2):
            rows = slice(s * tk, (s + 1) * tk)
            r = lax.broadcasted_iota(jnp.int32, (tk, 2 * tq), 0) + s * tk
            c = lax.broadcasted_iota(jnp.int32, (tk, 2 * tq), 1)
            c = jnp.where(c >= tq, c - tq, c)
            rc = (r - c).astype(F32)
            bias_ref[0, rows, :] = slope * rc
            bias_ref[1, rows, :] = -(slope * rc)
            for w in range(nqp):
                bias_ref[2 + w, rows, :] = -(slope * jnp.abs(rc - float(w * tq)))

    lam = (jnp.exp(jnp.sum(lq1_ref[...] * lk1_ref[...], axis=-1, keepdims=True))
           - jnp.exp(jnp.sum(lq2_ref[...] * lk2_ref[...], axis=-1, keepdims=True))
           + lam_init)

    def one_tile(t):
        qi = step * ATT_QSUB + t
        tile = (pl.program_id(0) * N_DIFF_HEADS + pl.program_id(1)) * (seq // tq) + qi
        lo, hi = lo_ref[tile], hi_ref[tile]
        lt_ref, at_ref = l_ref.at[t], acc_ref.at[t]

        qt = q_ref[0, t * tq:(t + 1) * tq, :].T
        z = jnp.zeros((dh, tq), BF16)
        qbd = jnp.concatenate([jnp.concatenate([qt[:dh], z], axis=1),
                               jnp.concatenate([z, qt[dh:]], axis=1)], axis=0)

        lt_ref[...] = jnp.zeros(lt_ref.shape, F32)
        at_ref[...] = jnp.zeros(at_ref.shape, F32)

        kd = qi // nqp
        w = qi % nqp

        def mode(n):
            kp = n // 2
            delta = (kp * pair - qi * tq).astype(F32)
            idx = jnp.where(kp < kd, 0, jnp.where(kp > kd, 1, 2 + w))
            sgn = jnp.where(kp < kd, delta, jnp.where(kp > kd, -delta, 0.0))
            return idx, pl.multiple_of((n % 2) * tk, tk), slope * sgn

        def scores(n):
            idx, row0, _ = mode(n)
            k0 = pl.multiple_of(n * tk, tk)
            kblk = k_ref[0, pl.ds(k0, tk), :]
            u = jnp.dot(kblk, qbd, preferred_element_type=F32)
            return u + bias_ref[idx, pl.ds(row0, tk), :]

        def values(n):
            return vt_ref[:, pl.ds(pl.multiple_of(n * tk, tk), tk)]

        def pipelined(qk_phase, sm_phase, carry0, trips):
            def body(n, carry_a):
                carry_b = qk_phase(n + 1, 1)
                sm_phase(n, 0, carry_a)
                carry_a_next = qk_phase(jnp.minimum(n + 2, hi - 1), 0)
                sm_phase(n + 1, 1, carry_b)
                return carry_a_next

            start, carry = lo, carry0
            for npt in trips:
                ntrip = (hi - start) // (2 * npt)

                def trip(i, c, start=start, npt=npt):
                    for j in range(npt):
                        c = body(start + 2 * (npt * i + j), c)
                    return c

                carry = lax.fori_loop(0, ntrip, trip, carry)
                start = start + 2 * npt * ntrip
            ndouble = (hi - start) // 2
            carry = lax.fori_loop(0, ndouble, lambda i, c, start=start: body(start + 2 * i, c), carry)
            last = start + 2 * ndouble

            @pl.when(last < hi)
            def _():
                sm_phase(last, 0, carry)

        def fixed_reference():
            ub = ub_ref[t]

            def qk_phase(n, slot):
                u_ref[slot] = scores(n)
                return 0

            def sm_phase(n, slot, _):
                cst = mode(n)[2]
                p = jnp.exp2(u_ref[slot] + (cst - ub))
                lt_ref[...] += jnp.sum(p, axis=0, keepdims=True)
                at_ref[...] += jnp.dot(values(n), p.astype(BF16), preferred_element_type=F32)

            pipelined(qk_phase, sm_phase, 0, ATT_TRIPS_FAST)

        def running_max():
            m_ref[...] = jnp.full(m_ref.shape, -jnp.inf, F32)

            def qk_phase(n, slot):
                v = scores(n)
                u_ref[slot] = v
                return jnp.max(v, axis=0, keepdims=True)

            def sm_phase(n, slot, mx):
                cst = mode(n)[2]
                m_old = m_ref[...]
                m_new = jnp.maximum(m_old, mx + cst)
                alpha = jnp.exp2(m_old - m_new)
                p = jnp.exp2(u_ref[slot] - (m_new - cst))
                lt_ref[...] = alpha * lt_ref[...] + jnp.sum(p, axis=0, keepdims=True)
                m_ref[...] = m_new
                at_ref[...] = alpha * at_ref[...] + jnp.dot(values(n), p.astype(BF16),
                                                            preferred_element_type=F32)

            pipelined(qk_phase, sm_phase, jnp.max(u_ref[0], axis=0, keepdims=True), ATT_TRIPS_SAFE)

        u_ref[0] = scores(lo)
        lax.cond(fast_ref[tile] == 1, fixed_reference, running_max)

        acc = at_ref[...]
        l = lt_ref[...]
        ot = acc[:, :tq] / l[:, :tq] - lam * (acc[:, tq:] / l[:, tq:])
        ms = jnp.mean(ot * ot, axis=0, keepdims=True)
        ot = ot * lax.rsqrt(ms + EPS) * sub_ref[...] * (1.0 - lam_init)
        o_ref[0, t * tq:(t + 1) * tq, :] = ot.T.astype(o_ref.dtype)

    for t in range(ATT_QSUB):
        one_tile(t)


def _diff_attention(qkv, slopes, lq1, lk1, lq2, lk2, subln_col, lam_init):
    b, s, _ = qkv.shape
    h, dv = N_DIFF_HEADS, DIFF_V_DIM
    tq, tk, qsub = ATT_TQ, ATT_TK, ATT_QSUB
    nq = s // tq
    assert s % ATT_PAIR == 0 and nq % qsub == 0
    nsteps = nq // qsub
    lo, hi, fast, ub = _attn_bounds(qkv, slopes)
    kern = functools.partial(_attn_kernel, lam_init=lam_init, seq=s)
    vec = pl.BlockSpec((1, DIFF_HEAD_DIM), lambda bi, hi, qi, *_: (0, 0))
    grid_spec = pltpu.PrefetchScalarGridSpec(
        num_scalar_prefetch=3,
        grid=(b, h, nsteps),
        in_specs=[
            pl.BlockSpec((1, qsub * tq, dv), lambda bi, hi, qi, *_: (bi, qi, hi)),
            pl.BlockSpec((1, s, dv), lambda bi, hi, qi, *_: (bi, 0, h + hi)),
            pl.BlockSpec((1, s, dv), lambda bi, hi, qi, *_: (bi, 0, 2 * h + hi)),
            pl.BlockSpec((qsub, 1, 1), lambda bi, hi, qi, *_: ((bi * h + hi) * nsteps + qi, 0, 0)),
            pl.BlockSpec((1, 1, 1), lambda bi, hi, qi, *_: (hi, 0, 0)),
            vec, vec, vec, vec,
            pl.BlockSpec((dv, 1), lambda bi, hi, qi, *_: (0, 0)),
        ],
        out_specs=pl.BlockSpec((1, qsub * tq, dv), lambda bi, hi, qi, *_: (bi, qi, hi)),
        scratch_shapes=[
            pltpu.VMEM((dv, s), BF16),
            pltpu.VMEM((2 + ATT_NQP, ATT_PAIR, 2 * tq), F32),
            pltpu.VMEM((2, tk, 2 * tq), F32),
            pltpu.VMEM((1, 2 * tq), F32),
            pltpu.VMEM((qsub, 1, 2 * tq), F32),
            pltpu.VMEM((qsub, dv, 2 * tq), F32),
        ],
    )
    return pl.pallas_call(
        kern,
        grid_spec=grid_spec,
        out_shape=jax.ShapeDtypeStruct((b, s, h * dv), BF16),
        compiler_params=_cparams(3),
        name="diff_attention",
    )(lo.reshape(-1), hi.reshape(-1), fast.reshape(-1),
      qkv, qkv, qkv, ub.reshape(-1, 1, 1), slopes, lq1, lk1, lq2, lk2, subln_col)


def _proj_res_kernel(x_ref, a_ref, w_ref, gate_ref, o_ref):
    y = jnp.dot(a_ref[0], w_ref[...], preferred_element_type=F32)
    o_ref[0] = x_ref[0] + gate_ref[0] * y


def _proj_residual(x, a, w_bf16, gate, tm):
    b, s, d = x.shape
    kdim = a.shape[-1]
    return pl.pallas_call(
        _proj_res_kernel,
        grid=(b, s // tm),
        in_specs=[
            pl.BlockSpec((1, tm, d), lambda i, j: (i, j, 0)),
            pl.BlockSpec((1, tm, kdim), lambda i, j: (i, j, 0)),
            pl.BlockSpec((kdim, d), lambda i, j: (0, 0)),
            pl.BlockSpec((1, 1, d), lambda i, j: (i, 0, 0)),
        ],
        out_specs=pl.BlockSpec((1, tm, d), lambda i, j: (i, j, 0)),
        out_shape=jax.ShapeDtypeStruct((b, s, d), F32),
        compiler_params=_cparams(2),
        name="proj_residual",
    )(x, a, w_bf16, gate)


def _routing_gates_t(lt):
    ng, ne = N_EXPERT_GROUPS, EXPERTS_PER_GROUP
    gl = [lt[g:g + 1, :] for g in range(ng)]
    gmax = functools.reduce(jnp.maximum, gl)
    den = functools.reduce(lambda a, b: a + b, [jnp.exp(v - gmax) for v in gl])
    g_w = 1.0 / den
    sel = []
    taken = jnp.zeros(gmax.shape, jnp.bool_)
    for g in range(ng):
        s = jnp.logical_and(gl[g] == gmax, jnp.logical_not(taken))
        taken = jnp.logical_or(taken, s)
        sel.append(s)
    ev = []
    for e in range(ne):
        v = jnp.zeros_like(gmax)
        for g in range(ng):
            row = ng + ne * g + e
            v = jnp.where(sel[g], lt[row:row + 1, :], v)
        ev.append(v)
    top1 = functools.reduce(jnp.maximum, ev)
    is1 = []
    taken = jnp.zeros(gmax.shape, jnp.bool_)
    for e in range(ne):
        s = jnp.logical_and(ev[e] == top1, jnp.logical_not(taken))
        taken = jnp.logical_or(taken, s)
        is1.append(s)
    rest = [jnp.where(is1[e], -jnp.inf, ev[e]) for e in range(ne)]
    top2 = functools.reduce(jnp.maximum, rest)
    is2 = []
    taken = jnp.zeros(gmax.shape, jnp.bool_)
    for e in range(ne):
        s = jnp.logical_and(jnp.logical_and(rest[e] == top2, jnp.logical_not(is1[e])),
                            jnp.logical_not(taken))
        taken = jnp.logical_or(taken, s)
        is2.append(s)
    ex = jnp.exp(top2 - top1)
    w1 = 1.0 / (1.0 + ex)
    w2 = ex / (1.0 + ex)
    ew = [jnp.where(is1[e], w1, jnp.where(is2[e], w2, 0.0)) for e in range(ne)]
    rows = [jnp.where(sel[g], g_w * ew[e], 0.0) for g in range(ng) for e in range(ne)]
    return jnp.concatenate(rows, axis=0), sel


MOE_TM = 1024
MOE_CH = 128
MOE_NB = 11
MOE_POS_LANE = 3 * N_EXPERTS


def _route_sort_kernel(x_ref, g_ref, sh_ref, sc_ref, wr_ref, br_ref, tri_ref,
                       hs_ref, gs_ref, meta_ref, gid_ref):
    ng, ch, nb = N_EXPERT_GROUPS, MOE_CH, MOE_NB
    tm = x_ref.shape[0]
    h32 = _rms_mod(x_ref[...], g_ref[...], sh_ref[0], sc_ref[0])
    hb = h32.astype(BF16)
    hl = (h32 - hb.astype(F32)).astype(BF16)
    wr = wr_ref[...]
    wb = wr.astype(BF16)
    wl = (wr - wb.astype(F32)).astype(BF16)
    nt = (((1,), (1,)), ((), ()))
    l2 = lax.dot_general(jnp.concatenate([wb, wl], axis=0), hb, nt, preferred_element_type=F32)
    lt = (l2[:LANES] + l2[LANES:] + lax.dot_general(wb, hl, nt, preferred_element_type=F32)
          + br_ref[...])
    gates_t, sel = _routing_gates_t(lt)
    oh = jnp.concatenate([m.astype(F32) for m in sel] + [jnp.zeros((16 - ng, tm), F32)], axis=0)
    cnt = jnp.dot(oh.astype(BF16), tri_ref[...], preferred_element_type=F32)
    rank = jnp.sum(oh * cnt, axis=0, keepdims=True)
    n = jnp.sum(oh, axis=1, keepdims=True)
    nch = jnp.floor((n + float(ch - 1)) * (1.0 / ch))
    first = [jnp.zeros((1, 1), F32)]
    for g in range(ng):
        first.append(first[-1] + nch[g:g + 1, :])
    seg = functools.reduce(lambda a, b: a + b,
                           [oh[g:g + 1, :] * (first[g] * float(ch)) for g in range(ng)])
    pos = seg + rank
    rows = lax.broadcasted_iota(jnp.int32, (nb * ch, tm), 0)
    perm = jnp.where(rows == pos.astype(jnp.int32), 1.0, 0.0).astype(BF16)
    hs_ref[0] = jnp.dot(perm, hb, preferred_element_type=F32).astype(BF16)
    hi = gates_t.astype(BF16).astype(F32)
    r1 = gates_t - hi
    mid = r1.astype(BF16).astype(F32)
    lo = r1 - mid
    meta_t = jnp.concatenate([hi, mid, lo, pos, jnp.zeros((LANES - MOE_POS_LANE - 1, tm), F32)], axis=0)
    meta = meta_t.T
    meta_ref[...] = meta
    gs_ref[0] = jnp.dot(perm, meta.astype(BF16), preferred_element_type=F32)
    lane = lax.broadcasted_iota(jnp.int32, (1, LANES), 1).astype(F32)
    gid = functools.reduce(lambda a, b: a + b,
                           [jnp.where(lane >= first[g + 1], 1.0, 0.0) for g in range(ng)])
    gid_ref[0] = gid.astype(jnp.int32)


def _route_sort(x2, g, shift, scale, wr, br, tri, tiles_per_batch):
    t, d = x2.shape
    tm, rows = MOE_TM, MOE_NB * MOE_CH
    nt = t // tm
    bvec = lambda i: (i // tiles_per_batch, 0, 0)
    return pl.pallas_call(
        _route_sort_kernel,
        grid=(nt,),
        in_specs=[
            pl.BlockSpec((tm, d), lambda i: (i, 0)),
            pl.BlockSpec((1, d), lambda i: (0, 0)),
            pl.BlockSpec((1, 1, d), bvec),
            pl.BlockSpec((1, 1, d), bvec),
            pl.BlockSpec((LANES, d), lambda i: (0, 0)),
            pl.BlockSpec((LANES, 1), lambda i: (0, 0)),
            pl.BlockSpec((tm, tm), lambda i: (0, 0)),
        ],
        out_specs=[
            pl.BlockSpec((1, rows, d), lambda i: (i, 0, 0)),
            pl.BlockSpec((1, rows, LANES), lambda i: (i, 0, 0)),
            pl.BlockSpec((tm, LANES), lambda i: (i, 0)),
            pl.BlockSpec((1, 1, LANES), lambda i: (i, 0, 0)),
        ],
        out_shape=[
            jax.ShapeDtypeStruct((nt, rows, d), BF16),
            jax.ShapeDtypeStruct((nt, rows, LANES), F32),
            jax.ShapeDtypeStruct((t, LANES), F32),
            jax.ShapeDtypeStruct((nt, 1, LANES), jnp.int32),
        ],
        compiler_params=_cparams(1),
        name="moe_route_sort",
    )(x2, g, shift, scale, wr, br, tri)


def _experts_kernel(order_ref, grp_ref, first_ref, hs_ref, gs_ref, wg_ref, wu_ref, wd_ref, ys_ref,
                    wgb_ref, wub_ref, wdb_ref):
    ne = EXPERTS_PER_GROUP
    step = pl.program_id(0)
    grp = grp_ref[step]

    @pl.when(first_ref[step] == 1)
    def _():
        wgb_ref[...] = wg_ref[0].astype(BF16)
        wub_ref[...] = wu_ref[0].astype(BF16)
        wdb_ref[...] = wd_ref[0].astype(BF16)

    @pl.when(grp < N_EXPERT_GROUPS)
    def _():
        h = hs_ref[0]
        gs = gs_ref[0]
        lane = lax.broadcasted_iota(jnp.int32, gs.shape, 1)
        hid = []
        for e in range(ne):
            idx = grp * ne + e
            pick = jnp.logical_or(jnp.logical_or(lane == idx, lane == idx + N_EXPERTS),
                                  lane == idx + 2 * N_EXPERTS)
            gate = jnp.sum(jnp.where(pick, gs, 0.0), axis=1, keepdims=True)
            a = jnp.dot(h, wgb_ref[e], preferred_element_type=F32)
            u = jnp.dot(h, wub_ref[e], preferred_element_type=F32)
            hid.append(((a / (1.0 + jnp.exp(-a))) * u * gate).astype(BF16))
        hid = jnp.concatenate(hid, axis=1)
        ys_ref[0] = jnp.dot(hid, wdb_ref[...], preferred_element_type=F32).astype(ys_ref.dtype)

    @pl.when(grp >= N_EXPERT_GROUPS)
    def _():
        ys_ref[0] = jnp.zeros(ys_ref.shape[1:], ys_ref.dtype)


def _experts(order, grp, first, hs, gs, wg4, wu4, wd4, layer):
    nchunks, ch, d = hs.shape
    ne, f = wg4.shape[1], wg4.shape[-1]
    ng = N_EXPERT_GROUPS
    wsel = lambda s, order, grp, first: (layer * ng + jnp.minimum(grp[s], ng - 1), 0, 0, 0)
    wsel3 = lambda s, order, grp, first: (layer * ng + jnp.minimum(grp[s], ng - 1), 0, 0)
    chunk = lambda s, order, grp, first: (order[s], 0, 0)
    once = pl.Buffered(1)
    grid_spec = pltpu.PrefetchScalarGridSpec(
        num_scalar_prefetch=3,
        grid=(nchunks,),
        in_specs=[
            pl.BlockSpec((1, ch, d), chunk),
            pl.BlockSpec((1, ch, LANES), chunk),
            pl.BlockSpec((1, ne, d, f), wsel),
            pl.BlockSpec((1, ne, d, f), wsel),
            pl.BlockSpec((1, ne * f, d), wsel3, pipeline_mode=once),
        ],
        out_specs=pl.BlockSpec((1, ch, d), chunk),
        scratch_shapes=[
            pltpu.VMEM((ne, d, f), BF16),
            pltpu.VMEM((ne, d, f), BF16),
            pltpu.VMEM((ne * f, d), BF16),
        ],
    )
    return pl.pallas_call(
        _experts_kernel,
        grid_spec=grid_spec,
        out_shape=jax.ShapeDtypeStruct((nchunks, ch, d), BF16),
        compiler_params=_cparams(1),
        name="moe_experts",
    )(order, grp, first, hs, gs, wg4, wu4, wd4)


def _unsort_kernel(x_ref, ys_ref, meta_ref, gate_ref, gf_ref, o_ref, *, final_norm):
    rows = ys_ref.shape[1]
    tm = x_ref.shape[0]
    pos = meta_ref[:, MOE_POS_LANE:MOE_POS_LANE + 1].astype(jnp.int32)
    cols = lax.broadcasted_iota(jnp.int32, (tm, rows), 1)
    perm_t = jnp.where(cols == pos, 1.0, 0.0).astype(BF16)
    y = jnp.dot(perm_t, ys_ref[0], preferred_element_type=F32)
    y = x_ref[...] + gate_ref[0] * y
    if final_norm:
        y = y * lax.rsqrt(jnp.mean(y * y, axis=-1, keepdims=True) + EPS) * gf_ref[...]
    o_ref[...] = y


def _unsort_residual(x2, ys, meta, gate, gfinal, final_norm, tiles_per_batch):
    t, d = x2.shape
    tm = MOE_TM
    nt, rows, _ = ys.shape
    kern = functools.partial(_unsort_kernel, final_norm=final_norm)
    return pl.pallas_call(
        kern,
        grid=(nt,),
        in_specs=[
            pl.BlockSpec((tm, d), lambda i: (i, 0)),
            pl.BlockSpec((1, rows, d), lambda i: (i, 0, 0)),
            pl.BlockSpec((tm, LANES), lambda i: (i, 0)),
            pl.BlockSpec((1, 1, d), lambda i: (i // tiles_per_batch, 0, 0)),
            pl.BlockSpec((1, d), lambda i: (0, 0)),
        ],
        out_specs=pl.BlockSpec((tm, d), lambda i: (i, 0)),
        out_shape=jax.ShapeDtypeStruct((t, d), F32),
        compiler_params=_cparams(1),
        name="moe_unsort_residual",
    )(x2, ys, meta, gate, gfinal)


def moe_layer(x, g, sh2, sc2, g2, w_rg, b_rg, w_re, b_re, w_gate, w_up, w_down, layer, gfinal, final_norm):
    b, s, d = x.shape
    ng, ne = N_EXPERT_GROUPS, EXPERTS_PER_GROUP
    f = w_gate.shape[-1]
    assert s % MOE_TM == 0
    nr = ng + N_EXPERTS
    wr = jnp.concatenate([w_rg.T, w_re.reshape(d, N_EXPERTS).T,
                          jnp.zeros((LANES - nr, d), F32)], axis=0)
    br = jnp.concatenate([b_rg, b_re.reshape(N_EXPERTS),
                          jnp.zeros((LANES - nr,), F32)]).reshape(LANES, 1)
    ti = np.arange(MOE_TM)
    tri = jnp.asarray(ti[:, None] < ti[None, :], F32).astype(BF16)
    x2 = x.reshape(b * s, d)
    tpb = s // MOE_TM
    hs, gs, meta, gid = _route_sort(x2, g.reshape(1, d), sh2, sc2, wr, br, tri, tpb)
    nt = hs.shape[0]
    gid_flat = gid[:, 0, :MOE_NB].reshape(-1)
    order = jnp.argsort(gid_flat, stable=True).astype(jnp.int32)
    grp = gid_flat[order]
    first = jnp.concatenate([jnp.ones((1,), jnp.int32), (grp[1:] != grp[:-1]).astype(jnp.int32)])
    nl = w_gate.shape[0]
    ys = _experts(order, grp, first,
                  hs.reshape(nt * MOE_NB, MOE_CH, d), gs.reshape(nt * MOE_NB, MOE_CH, LANES),
                  w_gate.reshape(nl * ng, ne, d, f), w_up.reshape(nl * ng, ne, d, f),
                  w_down.reshape(nl * ng, ne * f, d), layer)
    out = _unsort_residual(x2, ys.reshape(nt, MOE_NB * MOE_CH, d), meta, g2,
                           gfinal.reshape(1, d), final_norm, tpb)
    return out.reshape(b, s, d)


def _fft1_kernel(u_ref, w_ref, twr_ref, twi_ref, o_ref, *, n_sub, width):
    n1 = FFT_N1
    y = jnp.dot(w_ref[...], u_ref[0], preferred_element_type=F32)
    for j in range(n_sub):
        cols = slice(j * width, (j + 1) * width)
        yr, yi = y[:n1, cols], y[n1:, cols]
        twr, twi = twr_ref[j], twi_ref[j]
        o_ref[0, :n1, cols] = (yr * twr - yi * twi).astype(o_ref.dtype)
        o_ref[0, n1:, cols] = (yr * twi + yi * twr).astype(o_ref.dtype)


def _fft_stage1(u3, w1, twr, twi, width):
    b, n1, cols = u3.shape
    n_sub = 8
    tn = n_sub * width
    kern = functools.partial(_fft1_kernel, n_sub=n_sub, width=width)
    return pl.pallas_call(
        kern,
        grid=(b, cols // tn),
        in_specs=[
            pl.BlockSpec((1, n1, tn), lambda i, j: (i, 0, j)),
            pl.BlockSpec((2 * n1, n1), lambda i, j: (0, 0)),
            pl.BlockSpec((n_sub, n1, 1), lambda i, j: (j, 0, 0)),
            pl.BlockSpec((n_sub, n1, 1), lambda i, j: (j, 0, 0)),
        ],
        out_specs=pl.BlockSpec((1, 2 * n1, tn), lambda i, j: (i, 0, j)),
        out_shape=jax.ShapeDtypeStruct((b, 2 * n1, cols), BF16),
        compiler_params=_cparams(2),
        name="fft_stage1",
    )(u3, w1, twr, twi)


def _fft2_mix_kernel(y_ref, g_ref, x_ref, cc_ref, sc_ref, w_ref, gate_ref, o_ref):
    kp, n2 = FFT_K1_PER_STEP, FFT_N2
    width = y_ref.shape[-1]
    gd = cc_ref.shape[0]
    half = kp * n2
    y = y_ref[0].reshape(2 * half, width)
    r = jnp.dot(g_ref[...], y, preferred_element_type=F32)
    ar, ai = r[:half].astype(BF16), r[half:].astype(BF16)
    parts = []
    for g in range(N_FOURIER_GROUPS):
        cols = slice(g * gd, (g + 1) * gd)
        parts.append(jnp.dot(ar[:, cols], cc_ref[...], preferred_element_type=F32)
                     + jnp.dot(ai[:, cols], sc_ref[...], preferred_element_type=F32))
    f = jnp.concatenate(parts, axis=1).astype(BF16)
    yv = jnp.dot(f, w_ref[...], preferred_element_type=F32)
    out = x_ref[0].reshape(half, width) + gate_ref[0] * yv
    o_ref[0] = out.reshape(n2, kp, width)


def _fft2_mix(y5, gmat, x4, cc, sc, w_bf16, gate):
    b, _, n1, n2, width = y5.shape
    kp = FFT_K1_PER_STEP
    gd = cc.shape[0]
    return pl.pallas_call(
        _fft2_mix_kernel,
        grid=(b, n1 // kp),
        in_specs=[
            pl.BlockSpec((1, 2, kp, n2, width), lambda i, j: (i, 0, j, 0, 0)),
            pl.BlockSpec((2 * kp * n2, 2 * kp * n2), lambda i, j: (0, 0)),
            pl.BlockSpec((1, n2, kp, width), lambda i, j: (i, 0, j, 0)),
            pl.BlockSpec((gd, gd), lambda i, j: (0, 0)),
            pl.BlockSpec((gd, gd), lambda i, j: (0, 0)),
            pl.BlockSpec((width, width), lambda i, j: (0, 0)),
            pl.BlockSpec((1, 1, width), lambda i, j: (i, 0, 0)),
        ],
        out_specs=pl.BlockSpec((1, n2, kp, width), lambda i, j: (i, 0, j, 0)),
        out_shape=jax.ShapeDtypeStruct((b, n2, n1, width), F32),
        compiler_params=_cparams(2),
        name="fft_stage2_mix",
    )(y5, gmat, x4, cc, sc, w_bf16, gate)


def _dft_tables(seq, gd):
    n1, n2, kp = FFT_N1, FFT_N2, FFT_K1_PER_STEP

    def cs(num, den):
        ang = (2.0 * math.pi / den) * (num % den).astype(np.float64)
        return np.cos(ang), np.sin(ang)

    i1 = np.arange(n1)
    c1, s1 = cs(np.outer(i1, i1), n1)
    w1 = np.concatenate([c1, -s1], axis=0)
    ct, st = cs(np.outer(np.arange(n2), i1), seq)
    twr, twi = ct[:, :, None], -st[:, :, None]
    i2 = np.arange(n2)
    c2, s2 = cs(np.outer(i2, i2), n2)
    m3 = np.block([[c2, s2], [-s2, c2]])
    m3 = m3.reshape(2, n2, 2, n2)
    gmat = np.zeros((2, n2, kp, 2, kp, n2))
    for j in range(kp):
        gmat[:, :, j, :, j, :] = m3
    gmat = gmat.reshape(2 * n2 * kp, 2 * kp * n2)
    ic = np.arange(gd)
    cc, sc = cs(np.outer(ic, ic), gd)
    norm = 1.0 / math.sqrt(seq * gd)
    return tuple(jnp.asarray(t, F32) for t in (w1, twr, twi, gmat, cc * norm, sc * norm))


def kernel(x, c, ada_w, ada_b, norm_mix, norm_ffn, attn_w_in, attn_lam_q1, attn_lam_k1, attn_lam_q2, attn_lam_k2, attn_subln, attn_w_out, fourier_w_in, fourier_w_out, router_group_w, router_group_b, router_expert_w, router_expert_b, expert_w_gate, expert_w_up, expert_w_down, norm_final):
    b, s, d = x.shape
    depth = ada_w.shape[0]
    assert s == FFT_N1 * FFT_N2 and depth == 2

    c8 = jnp.zeros((8, d), F32).at[:b].set(c)
    mod = _ada(c8, ada_w, ada_b.reshape(depth, 1, N_ADA * d))[:, :b]
    mod = mod.reshape(depth, b, N_ADA, 1, d)

    slopes = (2.0 ** (-8.0 * jnp.arange(1, N_DIFF_HEADS + 1, dtype=F32) / N_DIFF_HEADS)
              ).reshape(N_DIFF_HEADS, 1, 1)
    gd = d // N_FOURIER_GROUPS
    w1, twr, twi, gmat, cc, sc = _dft_tables(s, gd)
    w1, gmat, cc, sc = (t.astype(BF16) for t in (w1, gmat, cc, sc))

    for i in range(depth):
        sh1, sc1, g1, sh2, sc2, g2 = [mod[i, :, k] for k in range(N_ADA)]
        j = i // 2
        if i % 2 == 0:
            qscale = jnp.concatenate([jnp.full((d,), LOG2E * DIFF_HEAD_DIM ** -0.5, F32),
                                      jnp.ones((2 * d,), F32)])
            w_in = (attn_w_in[j] * qscale).astype(BF16)
            qkv = _norm_mod_matmul(x, norm_mix[i].reshape(1, d), sh1, sc1, w_in, tm=512)
            lam_init = 0.8 - 0.6 * math.exp(-0.3 * i)
            o = _diff_attention(qkv, slopes,
                                attn_lam_q1[j].reshape(1, -1), attn_lam_k1[j].reshape(1, -1),
                                attn_lam_q2[j].reshape(1, -1), attn_lam_k2[j].reshape(1, -1),
                                attn_subln[j].reshape(-1, 1), lam_init)
            x = _proj_residual(x, o, attn_w_out[j].astype(BF16), g1, tm=512)
        else:
            u = _norm_mod_matmul(x, norm_mix[i].reshape(1, d), sh1, sc1,
                                 fourier_w_in[j].astype(BF16), tm=512)
            y = _fft_stage1(u.reshape(b, FFT_N1, FFT_N2 * d), w1, twr, twi, d)
            x = _fft2_mix(y.reshape(b, 2, FFT_N1, FFT_N2, d), gmat, x.reshape(b, FFT_N2, FFT_N1, d),
                          cc, sc, fourier_w_out[j].astype(BF16), g1).reshape(b, s, d)

        x = moe_layer(x, norm_ffn[i], sh2, sc2, g2, router_group_w[i], router_group_b[i],
                      router_expert_w[i], router_expert_b[i], expert_w_gate, expert_w_up,
                      expert_w_down, i, norm_final, i == depth - 1)
    return x
```

```python
import functools
import math

import numpy as np
import jax
import jax.numpy as jnp
from jax import lax
from jax.experimental import pallas as pl
from jax.experimental.pallas import tpu as pltpu

F32 = jnp.float32
BF16 = jnp.bfloat16

EPS = 1e-6
N_DIFF_HEADS = 8
DIFF_HEAD_DIM = 64
DIFF_V_DIM = 128
N_FOURIER_GROUPS = 4
N_EXPERT_GROUPS = 4
EXPERTS_PER_GROUP = 4
N_EXPERTS = 16
N_ADA = 6

LANES = 128
VMEM_LIMIT_BYTES = 56 * 1024 * 1024

FFT_N1 = 128
FFT_N2 = 64
FFT_K1_PER_STEP = 8


def _cparams(n_axes):
    return pltpu.CompilerParams(
        dimension_semantics=("arbitrary",) * n_axes,
        vmem_limit_bytes=VMEM_LIMIT_BYTES,
    )


def _rms_mod(x, g, shift, scale):
    y = x * lax.rsqrt(jnp.mean(x * x, axis=-1, keepdims=True) + EPS)
    return (y * g) * (1.0 + scale) + shift


def _ada_kernel(c_ref, w_ref, b_ref, o_ref):
    c = c_ref[...]
    cond = c / (1.0 + jnp.exp(-c))
    o_ref[0] = jnp.dot(cond, w_ref[0], precision=lax.Precision.HIGHEST,
                       preferred_element_type=F32) + b_ref[0]


def _ada(c8, ada_w, ada_b3):
    depth, d, n = ada_w.shape
    tn = 1536
    return pl.pallas_call(
        _ada_kernel,
        grid=(depth, n // tn),
        in_specs=[
            pl.BlockSpec((8, d), lambda i, j: (0, 0)),
            pl.BlockSpec((1, d, tn), lambda i, j: (i, 0, j)),
            pl.BlockSpec((1, 1, tn), lambda i, j: (i, 0, j)),
        ],
        out_specs=pl.BlockSpec((1, 8, tn), lambda i, j: (i, 0, j)),
        out_shape=jax.ShapeDtypeStruct((depth, 8, n), F32),
        compiler_params=_cparams(2),
        name="ada_mod",
    )(c8, ada_w, ada_b3)


def _nmm_kernel(x_ref, g_ref, sh_ref, sc_ref, w_ref, o_ref):
    h = _rms_mod(x_ref[0], g_ref[...], sh_ref[0], sc_ref[0])
    o_ref[0] = jnp.dot(h.astype(BF16), w_ref[...],
                       preferred_element_type=F32).astype(o_ref.dtype)


def _norm_mod_matmul(x, g, shift, scale, w_bf16, tm):
    b, s, d = x.shape
    n = w_bf16.shape[1]
    return pl.pallas_call(
        _nmm_kernel,
        grid=(b, s // tm),
        in_specs=[
            pl.BlockSpec((1, tm, d), lambda i, j: (i, j, 0)),
            pl.BlockSpec((1, d), lambda i, j: (0, 0)),
            pl.BlockSpec((1, 1, d), lambda i, j: (i, 0, 0)),
            pl.BlockSpec((1, 1, d), lambda i, j: (i, 0, 0)),
            pl.BlockSpec((d, n), lambda i, j: (0, 0)),
        ],
        out_specs=pl.BlockSpec((1, tm, n), lambda i, j: (i, j, 0)),
        out_shape=jax.ShapeDtypeStruct((b, s, n), BF16),
        compiler_params=_cparams(2),
        name="norm_mod_matmul",
    )(x, g, shift, scale, w_bf16)


ATT_TQ = 256
ATT_TK = 512
ATT_PAIR = 2 * ATT_TK
ATT_NQP = ATT_PAIR // ATT_TQ
ATT_QSUB = 2
ATT_TRIPS_FAST = (4, 2)
ATT_TRIPS_SAFE = (2,)
LOG2E = 1.4426950408889634
ATT_SKIP_BITS = 30.0
ATT_FAST_MAX_U = 36.0


def _attn_bounds_kernel(q_ref, k_ref, slope_ref, lo_ref, hi_ref, fast_ref, ub_ref, *, seq):
    tq = ATT_TQ
    nq = seq // tq
    dd = 2 * DIFF_HEAD_DIM
    r = lax.broadcasted_iota(jnp.int32, (dd, dd), 0)
    c = lax.broadcasted_iota(jnp.int32, (dd, dd), 1)
    halves = jnp.where(c == r // DIFF_HEAD_DIM, 1.0, 0.0).astype(BF16)

    def half_norm2(ref):
        x = ref[0].astype(F32)
        return jnp.dot((x * x).astype(BF16), halves, preferred_element_type=F32)

    qn2 = jnp.max(jnp.max(half_norm2(q_ref).reshape(nq, tq, dd), axis=1), axis=1, keepdims=True)
    kn2 = jnp.max(jnp.max(half_norm2(k_ref), axis=0, keepdims=True), axis=1, keepdims=True)
    ub = jnp.sqrt(qn2 * kn2) * (1.0 + 2.0 ** -6)
    slope = slope_ref[0] * LOG2E
    tail = jnp.minimum(jnp.log2(2.0 / (1.0 - jnp.exp2(-slope))), math.log2(seq))
    dist = (2.0 * ub + (ATT_SKIP_BITS + tail)) / slope
    dist = jnp.ceil(jnp.minimum(dist, float(seq))).astype(jnp.int32)
    q0 = lax.broadcasted_iota(jnp.int32, (nq, 1), 0) * tq
    jmin = jnp.maximum(q0 - dist + 1, 0)
    jmax = jnp.minimum(q0 + (tq - 1) + dist - 1, seq - 1)
    shift = jnp.full((nq, 1), int(math.log2(ATT_TK)), jnp.int32)
    lo_ref[0, 0] = lax.shift_right_logical(jmin, shift)
    hi_ref[0, 0] = lax.shift_right_logical(jmax, shift) + 1
    fast_ref[0, 0] = (ub <= ATT_FAST_MAX_U).astype(jnp.int32)
    ub_ref[0, 0] = ub


def _attn_bounds(qkv, slopes):
    b, s, _ = qkv.shape
    h, dv = N_DIFF_HEADS, DIFF_V_DIM
    nq = s // ATT_TQ
    kern = functools.partial(_attn_bounds_kernel, seq=s)
    oblk = pl.BlockSpec((1, 1, nq, 1), lambda bi, hi: (bi, hi, 0, 0))
    ishape = jax.ShapeDtypeStruct((b, h, nq, 1), jnp.int32)
    return pl.pallas_call(
        kern,
        grid=(b, h),
        in_specs=[
            pl.BlockSpec((1, s, dv), lambda bi, hi: (bi, 0, hi)),
            pl.BlockSpec((1, s, dv), lambda bi, hi: (bi, 0, h + hi)),
            pl.BlockSpec((1, 1, 1), lambda bi, hi: (hi, 0, 0)),
        ],
        out_specs=[oblk, oblk, oblk, oblk],
        out_shape=[ishape, ishape, ishape, jax.ShapeDtypeStruct((b, h, nq, 1), F32)],
        compiler_params=_cparams(2),
        name="attn_bounds",
    )(qkv, qkv, slopes)


def _attn_kernel(lo_ref, hi_ref, fast_ref, q_ref, k_ref, v_ref, ub_ref, slope_ref,
                 lq1_ref, lk1_ref, lq2_ref, lk2_ref, sub_ref,
                 o_ref, vt_ref, bias_ref, u_ref, m_ref, l_ref, acc_ref, *, lam_init, seq):
    tq, tk, pair, nqp = ATT_TQ, ATT_TK, ATT_PAIR, ATT_NQP
    dh = DIFF_HEAD_DIM
    step = pl.program_id(2)
    slope = slope_ref[0] * LOG2E

    @pl.when(step == 0)
    def _():
        for c in range(seq // 512):
            vt_ref[:, c * 512:(c + 1) * 512] = v_ref[0, c * 512:(c + 1) * 512, :].T
        for s in range(2):
            rows = slice(s * tk, (s + 1) * tk)
            r = lax.broadcasted_iota(jnp.int32, (tk, 2 * tq), 0) + s * tk
            c = lax.broadcasted_iota(jnp.int32, (tk, 2 * tq), 1)
            c = jnp.where(c >= tq, c - tq, c)
            rc = (r - c).astype(F32)
            bias_ref[0, rows, :] = slope * rc
            bias_ref[1, rows, :] = -(slope * rc)
            for w in range(nqp):
                bias_ref[2 + w, rows, :] = -(slope * jnp.abs(rc - float(w * tq)))

    lam = (jnp.exp(jnp.sum(lq1_ref[...] * lk1_ref[...], axis=-1, keepdims=True))
           - jnp.exp(jnp.sum(lq2_ref[...] * lk2_ref[...], axis=-1, keepdims=True))
           + lam_init)

    def one_tile(t):
        qi = step * ATT_QSUB + t
        tile = (pl.program_id(0) * N_DIFF_HEADS + pl.program_id(1)) * (seq // tq) + qi
        lo, hi = lo_ref[tile], hi_ref[tile]
        lt_ref, at_ref = l_ref.at[t], acc_ref.at[t]

        qt = q_ref[0, t * tq:(t + 1) * tq, :].T
        z = jnp.zeros((dh, tq), BF16)
        qbd = jnp.concatenate([jnp.concatenate([qt[:dh], z], axis=1),
                               jnp.concatenate([z, qt[dh:]], axis=1)], axis=0)

        lt_ref[...] = jnp.zeros(lt_ref.shape, F32)
        at_ref[...] = jnp.zeros(at_ref.shape, F32)

        kd = qi // nqp
        w = qi % nqp

        def mode(n):
            kp = n // 2
            delta = (kp * pair - qi * tq).astype(F32)
            idx = jnp.where(kp < kd, 0, jnp.where(kp > kd, 1, 2 + w))
            sgn = jnp.where(kp < kd, delta, jnp.where(kp > kd, -delta, 0.0))
            return idx, pl.multiple_of((n % 2) * tk, tk), slope * sgn

        def scores(n):
            idx, row0, _ = mode(n)
            k0 = pl.multiple_of(n * tk, tk)
            kblk = k_ref[0, pl.ds(k0, tk), :]
            u = jnp.dot(kblk, qbd, preferred_element_type=F32)
            return u + bias_ref[idx, pl.ds(row0, tk), :]

        def values(n):
            return vt_ref[:, pl.ds(pl.multiple_of(n * tk, tk), tk)]

        def pipelined(qk_phase, sm_phase, carry0, trips):
            def body(n, carry_a):
                carry_b = qk_phase(n + 1, 1)
                sm_phase(n, 0, carry_a)
                carry_a_next = qk_phase(jnp.minimum(n + 2, hi - 1), 0)
                sm_phase(n + 1, 1, carry_b)
                return carry_a_next

            start, carry = lo, carry0
            for npt in trips:
                ntrip = (hi - start) // (2 * npt)

                def trip(i, c, start=start, npt=npt):
                    for j in range(npt):
                        c = body(start + 2 * (npt * i + j), c)
                    return c

                carry = lax.fori_loop(0, ntrip, trip, carry)
                start = start + 2 * npt * ntrip
            ndouble = (hi - start) // 2
            carry = lax.fori_loop(0, ndouble, lambda i, c, start=start: body(start + 2 * i, c), carry)
            last = start + 2 * ndouble

            @pl.when(last < hi)
            def _():
                sm_phase(last, 0, carry)

        def fixed_reference():
            ub = ub_ref[t]

            def qk_phase(n, slot):
                u_ref[slot] = scores(n)
                return 0

            def sm_phase(n, slot, _):
                cst = mode(n)[2]
                p = jnp.exp2(u_ref[slot] + (cst - ub))
                lt_ref[...] += jnp.sum(p, axis=0, keepdims=True)
                at_ref[...] += jnp.dot(values(n), p.astype(BF16), preferred_element_type=F32)

            pipelined(qk_phase, sm_phase, 0, ATT_TRIPS_FAST)

        def running_max():
            m_ref[...] = jnp.full(m_ref.shape, -jnp.inf, F32)

            def qk_phase(n, slot):
                v = scores(n)
                u_ref[slot] = v
                return jnp.max(v, axis=0, keepdims=True)

            def sm_phase(n, slot, mx):
                cst = mode(n)[2]
                m_old = m_ref[...]
                m_new = jnp.maximum(m_old, mx + cst)
                alpha = jnp.exp2(m_old - m_new)
                p = jnp.exp2(u_ref[slot] - (m_new - cst))
                lt_ref[...] = alpha * lt_ref[...] + jnp.sum(p, axis=0, keepdims=True)
                m_ref[...] = m_new
                at_ref[...] = alpha * at_ref[...] + jnp.dot(values(n), p.astype(BF16),
                                                            preferred_element_type=F32)

            pipelined(qk_phase, sm_phase, jnp.max(u_ref[0], axis=0, keepdims=True), ATT_TRIPS_SAFE)

        u_ref[0] = scores(lo)
        lax.cond(fast_ref[tile] == 1, fixed_reference, running_max)

        acc = at_ref[...]
        l = lt_ref[...]
        ot = acc[:, :tq] / l[:, :tq] - lam * (acc[:, tq:] / l[:, tq:])
        ms = jnp.mean(ot * ot, axis=0, keepdims=True)
        ot = ot * lax.rsqrt(ms + EPS) * sub_ref[...] * (1.0 - lam_init)
        o_ref[0, t * tq:(t + 1) * tq, :] = ot.T.astype(o_ref.dtype)

    for t in range(ATT_QSUB):
        one_tile(t)


def _diff_attention(qkv, slopes, lq1, lk1, lq2, lk2, subln_col, lam_init):
    b, s, _ = qkv.shape
    h, dv = N_DIFF_HEADS, DIFF_V_DIM
    tq, tk, qsub = ATT_TQ, ATT_TK, ATT_QSUB
    nq = s // tq
    assert s % ATT_PAIR == 0 and nq % qsub == 0
    nsteps = nq // qsub
    lo, hi, fast, ub = _attn_bounds(qkv, slopes)
    kern = functools.partial(_attn_kernel, lam_init=lam_init, seq=s)
    vec = pl.BlockSpec((1, DIFF_HEAD_DIM), lambda bi, hi, qi, *_: (0, 0))
    grid_spec = pltpu.PrefetchScalarGridSpec(
        num_scalar_prefetch=3,
        grid=(b, h, nsteps),
        in_specs=[
            pl.BlockSpec((1, qsub * tq, dv), lambda bi, hi, qi, *_: (bi, qi, hi)),
            pl.BlockSpec((1, s, dv), lambda bi, hi, qi, *_: (bi, 0, h + hi)),
            pl.BlockSpec((1, s, dv), lambda bi, hi, qi, *_: (bi, 0, 2 * h + hi)),
            pl.BlockSpec((qsub, 1, 1), lambda bi, hi, qi, *_: ((bi * h + hi) * nsteps + qi, 0, 0)),
            pl.BlockSpec((1, 1, 1), lambda bi, hi, qi, *_: (hi, 0, 0)),
            vec, vec, vec, vec,
            pl.BlockSpec((dv, 1), lambda bi, hi, qi, *_: (0, 0)),
        ],
        out_specs=pl.BlockSpec((1, qsub * tq, dv), lambda bi, hi, qi, *_: (bi, qi, hi)),
        scratch_shapes=[
            pltpu.VMEM((dv, s), BF16),
            pltpu.VMEM((2 + ATT_NQP, ATT_PAIR, 2 * tq), F32),
            pltpu.VMEM((2, tk, 2 * tq), F32),
            pltpu.VMEM((1, 2 * tq), F32),
            pltpu.VMEM((qsub, 1, 2 * tq), F32),
            pltpu.VMEM((qsub, dv, 2 * tq), F32),
        ],
    )
    return pl.pallas_call(
        kern,
        grid_spec=grid_spec,
        out_shape=jax.ShapeDtypeStruct((b, s, h * dv), BF16),
        compiler_params=_cparams(3),
        name="diff_attention",
    )(lo.reshape(-1), hi.reshape(-1), fast.reshape(-1),
      qkv, qkv, qkv, ub.reshape(-1, 1, 1), slopes, lq1, lk1, lq2, lk2, subln_col)


def _proj_res_kernel(x_ref, a_ref, w_ref, gate_ref, o_ref):
    y = jnp.dot(a_ref[0], w_ref[...], preferred_element_type=F32)
    o_ref[0] = x_ref[0] + gate_ref[0] * y


def _proj_residual(x, a, w_bf16, gate, tm):
    b, s, d = x.shape
    kdim = a.shape[-1]
    return pl.pallas_call(
        _proj_res_kernel,
        grid=(b, s // tm),
        in_specs=[
            pl.BlockSpec((1, tm, d), lambda i, j: (i, j, 0)),
            pl.BlockSpec((1, tm, kdim), lambda i, j: (i, j, 0)),
            pl.BlockSpec((kdim, d), lambda i, j: (0, 0)),
            pl.BlockSpec((1, 1, d), lambda i, j: (i, 0, 0)),
        ],
        out_specs=pl.BlockSpec((1, tm, d), lambda i, j: (i, j, 0)),
        out_shape=jax.ShapeDtypeStruct((b, s, d), F32),
        compiler_params=_cparams(2),
        name="proj_residual",
    )(x, a, w_bf16, gate)


def _routing_gates_t(lt):
    ng, ne = N_EXPERT_GROUPS, EXPERTS_PER_GROUP
    gl = [lt[g:g + 1, :] for g in range(ng)]
    gmax = functools.reduce(jnp.maximum, gl)
    den = functools.reduce(lambda a, b: a + b, [jnp.exp(v - gmax) for v in gl])
    g_w = 1.0 / den
    sel = []
    taken = jnp.zeros(gmax.shape, jnp.bool_)
    for g in range(ng):
        s = jnp.logical_and(gl[g] == gmax, jnp.logical_not(taken))
        taken = jnp.logical_or(taken, s)
        sel.append(s)
    ev = []
    for e in range(ne):
        v = jnp.zeros_like(gmax)
        for g in range(ng):
            row = ng + ne * g + e
            v = jnp.where(sel[g], lt[row:row + 1, :], v)
        ev.append(v)
    top1 = functools.reduce(jnp.maximum, ev)
    is1 = []
    taken = jnp.zeros(gmax.shape, jnp.bool_)
    for e in range(ne):
        s = jnp.logical_and(ev[e] == top1, jnp.logical_not(taken))
        taken = jnp.logical_or(taken, s)
        is1.append(s)
    rest = [jnp.where(is1[e], -jnp.inf, ev[e]) for e in range(ne)]
    top2 = functools.reduce(jnp.maximum, rest)
    is2 = []
    taken = jnp.zeros(gmax.shape, jnp.bool_)
    for e in range(ne):
        s = jnp.logical_and(jnp.logical_and(rest[e] == top2, jnp.logical_not(is1[e])),
                            jnp.logical_not(taken))
        taken = jnp.logical_or(taken, s)
        is2.append(s)
    ex = jnp.exp(top2 - top1)
    w1 = 1.0 / (1.0 + ex)
    w2 = ex / (1.0 + ex)
    ew = [jnp.where(is1[e], w1, jnp.where(is2[e], w2, 0.0)) for e in range(ne)]
    rows = [jnp.where(sel[g], g_w * ew[e], 0.0) for g in range(ng) for e in range(ne)]
    return jnp.concatenate(rows, axis=0), sel


MOE_TM = 1024
MOE_CH = 128
MOE_NB = 11
MOE_POS_LANE = 3 * N_EXPERTS


def _route_sort_kernel(x_ref, g_ref, sh_ref, sc_ref, wr_ref, br_ref, tri_ref,
                       hs_ref, gs_ref, meta_ref, gid_ref):
    ng, ch, nb = N_EXPERT_GROUPS, MOE_CH, MOE_NB
    tm = x_ref.shape[0]
    h32 = _rms_mod(x_ref[...], g_ref[...], sh_ref[0], sc_ref[0])
    hb = h32.astype(BF16)
    hl = (h32 - hb.astype(F32)).astype(BF16)
    wr = wr_ref[...]
    wb = wr.astype(BF16)
    wl = (wr - wb.astype(F32)).astype(BF16)
    nt = (((1,), (1,)), ((), ()))
    l2 = lax.dot_general(jnp.concatenate([wb, wl], axis=0), hb, nt, preferred_element_type=F32)
    lt = (l2[:LANES] + l2[LANES:] + lax.dot_general(wb, hl, nt, preferred_element_type=F32)
          + br_ref[...])
    gates_t, sel = _routing_gates_t(lt)
    oh = jnp.concatenate([m.astype(F32) for m in sel] + [jnp.zeros((16 - ng, tm), F32)], axis=0)
    cnt = jnp.dot(oh.astype(BF16), tri_ref[...], preferred_element_type=F32)
    rank = jnp.sum(oh * cnt, axis=0, keepdims=True)
    n = jnp.sum(oh, axis=1, keepdims=True)
    nch = jnp.floor((n + float(ch - 1)) * (1.0 / ch))
    first = [jnp.zeros((1, 1), F32)]
    for g in range(ng):
        first.append(first[-1] + nch[g:g + 1, :])
    seg = functools.reduce(lambda a, b: a + b,
                           [oh[g:g + 1, :] * (first[g] * float(ch)) for g in range(ng)])
    pos = seg + rank
    rows = lax.broadcasted_iota(jnp.int32, (nb * ch, tm), 0)
    perm = jnp.where(rows == pos.astype(jnp.int32), 1.0, 0.0).astype(BF16)
    hs_ref[0] = jnp.dot(perm, hb, preferred_element_type=F32).astype(BF16)
    hi = gates_t.astype(BF16).astype(F32)
    r1 = gates_t - hi
    mid = r1.astype(BF16).astype(F32)
    lo = r1 - mid
    meta_t = jnp.concatenate([hi, mid, lo, pos, jnp.zeros((LANES - MOE_POS_LANE - 1, tm), F32)], axis=0)
    meta = meta_t.T
    meta_ref[...] = meta
    gs_ref[0] = jnp.dot(perm, meta.astype(BF16), preferred_element_type=F32)
    lane = lax.broadcasted_iota(jnp.int32, (1, LANES), 1).astype(F32)
    gid = functools.reduce(lambda a, b: a + b,
                           [jnp.where(lane >= first[g + 1], 1.0, 0.0) for g in range(ng)])
    gid_ref[0] = gid.astype(jnp.int32)


def _route_sort(x2, g, shift, scale, wr, br, tri, tiles_per_batch):
    t, d = x2.shape
    tm, rows = MOE_TM, MOE_NB * MOE_CH
    nt = t // tm
    bvec = lambda i: (i // tiles_per_batch, 0, 0)
    return pl.pallas_call(
        _route_sort_kernel,
        grid=(nt,),
        in_specs=[
            pl.BlockSpec((tm, d), lambda i: (i, 0)),
            pl.BlockSpec((1, d), lambda i: (0, 0)),
            pl.BlockSpec((1, 1, d), bvec),
            pl.BlockSpec((1, 1, d), bvec),
            pl.BlockSpec((LANES, d), lambda i: (0, 0)),
            pl.BlockSpec((LANES, 1), lambda i: (0, 0)),
            pl.BlockSpec((tm, tm), lambda i: (0, 0)),
        ],
        out_specs=[
            pl.BlockSpec((1, rows, d), lambda i: (i, 0, 0)),
            pl.BlockSpec((1, rows, LANES), lambda i: (i, 0, 0)),
            pl.BlockSpec((tm, LANES), lambda i: (i, 0)),
            pl.BlockSpec((1, 1, LANES), lambda i: (i, 0, 0)),
        ],
        out_shape=[
            jax.ShapeDtypeStruct((nt, rows, d), BF16),
            jax.ShapeDtypeStruct((nt, rows, LANES), F32),
            jax.ShapeDtypeStruct((t, LANES), F32),
            jax.ShapeDtypeStruct((nt, 1, LANES), jnp.int32),
        ],
        compiler_params=_cparams(1),
        name="moe_route_sort",
    )(x2, g, shift, scale, wr, br, tri)


def _experts_kernel(order_ref, grp_ref, first_ref, hs_ref, gs_ref, wg_ref, wu_ref, wd_ref, ys_ref,
                    wgb_ref, wub_ref, wdb_ref):
    ne = EXPERTS_PER_GROUP
    step = pl.program_id(0)
    grp = grp_ref[step]

    @pl.when(first_ref[step] == 1)
    def _():
        wgb_ref[...] = wg_ref[0].astype(BF16)
        wub_ref[...] = wu_ref[0].astype(BF16)
        wdb_ref[...] = wd_ref[0].astype(BF16)

    @pl.when(grp < N_EXPERT_GROUPS)
    def _():
        h = hs_ref[0]
        gs = gs_ref[0]
        lane = lax.broadcasted_iota(jnp.int32, gs.shape, 1)
        hid = []
        for e in range(ne):
            idx = grp * ne + e
            pick = jnp.logical_or(jnp.logical_or(lane == idx, lane == idx + N_EXPERTS),
                                  lane == idx + 2 * N_EXPERTS)
            gate = jnp.sum(jnp.where(pick, gs, 0.0), axis=1, keepdims=True)
            a = jnp.dot(h, wgb_ref[e], preferred_element_type=F32)
            u = jnp.dot(h, wub_ref[e], preferred_element_type=F32)
            hid.append(((a / (1.0 + jnp.exp(-a))) * u * gate).astype(BF16))
        hid = jnp.concatenate(hid, axis=1)
        ys_ref[0] = jnp.dot(hid, wdb_ref[...], preferred_element_type=F32).astype(ys_ref.dtype)

    @pl.when(grp >= N_EXPERT_GROUPS)
    def _():
        ys_ref[0] = jnp.zeros(ys_ref.shape[1:], ys_ref.dtype)


def _experts(order, grp, first, hs, gs, wg4, wu4, wd4, layer):
    nchunks, ch, d = hs.shape
    ne, f = wg4.shape[1], wg4.shape[-1]
    ng = N_EXPERT_GROUPS
    wsel = lambda s, order, grp, first: (layer * ng + jnp.minimum(grp[s], ng - 1), 0, 0, 0)
    wsel3 = lambda s, order, grp, first: (layer * ng + jnp.minimum(grp[s], ng - 1), 0, 0)
    chunk = lambda s, order, grp, first: (order[s], 0, 0)
    once = pl.Buffered(1)
    grid_spec = pltpu.PrefetchScalarGridSpec(
        num_scalar_prefetch=3,
        grid=(nchunks,),
        in_specs=[
            pl.BlockSpec((1, ch, d), chunk),
            pl.BlockSpec((1, ch, LANES), chunk),
            pl.BlockSpec((1, ne, d, f), wsel),
            pl.BlockSpec((1, ne, d, f), wsel),
            pl.BlockSpec((1, ne * f, d), wsel3, pipeline_mode=once),
        ],
        out_specs=pl.BlockSpec((1, ch, d), chunk),
        scratch_shapes=[
            pltpu.VMEM((ne, d, f), BF16),
            pltpu.VMEM((ne, d, f), BF16),
            pltpu.VMEM((ne * f, d), BF16),
        ],
    )
    return pl.pallas_call(
        _experts_kernel,
        grid_spec=grid_spec,
        out_shape=jax.ShapeDtypeStruct((nchunks, ch, d), BF16),
        compiler_params=_cparams(1),
        name="moe_experts",
    )(order, grp, first, hs, gs, wg4, wu4, wd4)


def _unsort_kernel(x_ref, ys_ref, meta_ref, gate_ref, gf_ref, o_ref, *, final_norm):
    rows = ys_ref.shape[1]
    tm = x_ref.shape[0]
    pos = meta_ref[:, MOE_POS_LANE:MOE_POS_LANE + 1].astype(jnp.int32)
    cols = lax.broadcasted_iota(jnp.int32, (tm, rows), 1)
    perm_t = jnp.where(cols == pos, 1.0, 0.0).astype(BF16)
    y = jnp.dot(perm_t, ys_ref[0], preferred_element_type=F32)
    y = x_ref[...] + gate_ref[0] * y
    if final_norm:
        y = y * lax.rsqrt(jnp.mean(y * y, axis=-1, keepdims=True) + EPS) * gf_ref[...]
    o_ref[...] = y


def _unsort_residual(x2, ys, meta, gate, gfinal, final_norm, tiles_per_batch):
    t, d = x2.shape
    tm = MOE_TM
    nt, rows, _ = ys.shape
    kern = functools.partial(_unsort_kernel, final_norm=final_norm)
    return pl.pallas_call(
        kern,
        grid=(nt,),
        in_specs=[
            pl.BlockSpec((tm, d), lambda i: (i, 0)),
            pl.BlockSpec((1, rows, d), lambda i: (i, 0, 0)),
            pl.BlockSpec((tm, LANES), lambda i: (i, 0)),
            pl.BlockSpec((1, 1, d), lambda i: (i // tiles_per_batch, 0, 0)),
            pl.BlockSpec((1, d), lambda i: (0, 0)),
        ],
        out_specs=pl.BlockSpec((tm, d), lambda i: (i, 0)),
        out_shape=jax.ShapeDtypeStruct((t, d), F32),
        compiler_params=_cparams(1),
        name="moe_unsort_residual",
    )(x2, ys, meta, gate, gfinal)


def moe_layer(x, g, sh2, sc2, g2, w_rg, b_rg, w_re, b_re, w_gate, w_up, w_down, layer, gfinal, final_norm):
    b, s, d = x.shape
    ng, ne = N_EXPERT_GROUPS, EXPERTS_PER_GROUP
    f = w_gate.shape[-1]
    assert s % MOE_TM == 0
    nr = ng + N_EXPERTS
    wr = jnp.concatenate([w_rg.T, w_re.reshape(d, N_EXPERTS).T,
                          jnp.zeros((LANES - nr, d), F32)], axis=0)
    br = jnp.concatenate([b_rg, b_re.reshape(N_EXPERTS),
                          jnp.zeros((LANES - nr,), F32)]).reshape(LANES, 1)
    ti = np.arange(MOE_TM)
    tri = jnp.asarray(ti[:, None] < ti[None, :], F32).astype(BF16)
    x2 = x.reshape(b * s, d)
    tpb = s // MOE_TM
    hs, gs, meta, gid = _route_sort(x2, g.reshape(1, d), sh2, sc2, wr, br, tri, tpb)
    nt = hs.shape[0]
    gid_flat = gid[:, 0, :MOE_NB].reshape(-1)
    order = jnp.argsort(gid_flat, stable=True).astype(jnp.int32)
    grp = gid_flat[order]
    first = jnp.concatenate([jnp.ones((1,), jnp.int32), (grp[1:] != grp[:-1]).astype(jnp.int32)])
    nl = w_gate.shape[0]
    ys = _experts(order, grp, first,
                  hs.reshape(nt * MOE_NB, MOE_CH, d), gs.reshape(nt * MOE_NB, MOE_CH, LANES),
                  w_gate.reshape(nl * ng, ne, d, f), w_up.reshape(nl * ng, ne, d, f),
                  w_down.reshape(nl * ng, ne * f, d), layer)
    out = _unsort_residual(x2, ys.reshape(nt, MOE_NB * MOE_CH, d), meta, g2,
                           gfinal.reshape(1, d), final_norm, tpb)
    return out.reshape(b, s, d)


def _fft1_kernel(u_ref, w_ref, twr_ref, twi_ref, o_ref, *, n_sub, width):
    n1 = FFT_N1
    y = jnp.dot(w_ref[...], u_ref[0], preferred_element_type=F32)
    for j in range(n_sub):
        cols = slice(j * width, (j + 1) * width)
        yr, yi = y[:n1, cols], y[n1:, cols]
        twr, twi = twr_ref[j], twi_ref[j]
        o_ref[0, :n1, cols] = (yr * twr - yi * twi).astype(o_ref.dtype)
        o_ref[0, n1:, cols] = (yr * twi + yi * twr).astype(o_ref.dtype)


def _fft_stage1(u3, w1, twr, twi, width):
    b, n1, cols = u3.shape
    n_sub = 8
    tn = n_sub * width
    kern = functools.partial(_fft1_kernel, n_sub=n_sub, width=width)
    return pl.pallas_call(
        kern,
        grid=(b, cols // tn),
        in_specs=[
            pl.BlockSpec((1, n1, tn), lambda i, j: (i, 0, j)),
            pl.BlockSpec((2 * n1, n1), lambda i, j: (0, 0)),
            pl.BlockSpec((n_sub, n1, 1), lambda i, j: (j, 0, 0)),
            pl.BlockSpec((n_sub, n1, 1), lambda i, j: (j, 0, 0)),
        ],
        out_specs=pl.BlockSpec((1, 2 * n1, tn), lambda i, j: (i, 0, j)),
        out_shape=jax.ShapeDtypeStruct((b, 2 * n1, cols), BF16),
        compiler_params=_cparams(2),
        name="fft_stage1",
    )(u3, w1, twr, twi)


def _fft2_mix_kernel(y_ref, g_ref, x_ref, cc_ref, sc_ref, w_ref, gate_ref, o_ref):
    kp, n2 = FFT_K1_PER_STEP, FFT_N2
    width = y_ref.shape[-1]
    gd = cc_ref.shape[0]
    half = kp * n2
    y = y_ref[0].reshape(2 * half, width)
    r = jnp.dot(g_ref[...], y, preferred_element_type=F32)
    ar, ai = r[:half].astype(BF16), r[half:].astype(BF16)
    parts = []
    for g in range(N_FOURIER_GROUPS):
        cols = slice(g * gd, (g + 1) * gd)
        parts.append(jnp.dot(ar[:, cols], cc_ref[...], preferred_element_type=F32)
                     + jnp.dot(ai[:, cols], sc_ref[...], preferred_element_type=F32))
    f = jnp.concatenate(parts, axis=1).astype(BF16)
    yv = jnp.dot(f, w_ref[...], preferred_element_type=F32)
    out = x_ref[0].reshape(half, width) + gate_ref[0] * yv
    o_ref[0] = out.reshape(n2, kp, width)


def _fft2_mix(y5, gmat, x4, cc, sc, w_bf16, gate):
    b, _, n1, n2, width = y5.shape
    kp = FFT_K1_PER_STEP
    gd = cc.shape[0]
    return pl.pallas_call(
        _fft2_mix_kernel,
        grid=(b, n1 // kp),
        in_specs=[
            pl.BlockSpec((1, 2, kp, n2, width), lambda i, j: (i, 0, j, 0, 0)),
            pl.BlockSpec((2 * kp * n2, 2 * kp * n2), lambda i, j: (0, 0)),
            pl.BlockSpec((1, n2, kp, width), lambda i, j: (i, 0, j, 0)),
            pl.BlockSpec((gd, gd), lambda i, j: (0, 0)),
            pl.BlockSpec((gd, gd), lambda i, j: (0, 0)),
            pl.BlockSpec((width, width), lambda i, j: (0, 0)),
            pl.BlockSpec((1, 1, width), lambda i, j: (i, 0, 0)),
        ],
        out_specs=pl.BlockSpec((1, n2, kp, width), lambda i, j: (i, 0, j, 0)),
        out_shape=jax.ShapeDtypeStruct((b, n2, n1, width), F32),
        compiler_params=_cparams(2),
        name="fft_stage2_mix",
    )(y5, gmat, x4, cc, sc, w_bf16, gate)


def _dft_tables(seq, gd):
    n1, n2, kp = FFT_N1, FFT_N2, FFT_K1_PER_STEP

    def cs(num, den):
        ang = (2.0 * math.pi / den) * (num % den).astype(np.float64)
        return np.cos(ang), np.sin(ang)

    i1 = np.arange(n1)
    c1, s1 = cs(np.outer(i1, i1), n1)
    w1 = np.concatenate([c1, -s1], axis=0)
    ct, st = cs(np.outer(np.arange(n2), i1), seq)
    twr, twi = ct[:, :, None], -st[:, :, None]
    i2 = np.arange(n2)
    c2, s2 = cs(np.outer(i2, i2), n2)
    m3 = np.block([[c2, s2], [-s2, c2]])
    m3 = m3.reshape(2, n2, 2, n2)
    gmat = np.zeros((2, n2, kp, 2, kp, n2))
    for j in range(kp):
        gmat[:, :, j, :, j, :] = m3
    gmat = gmat.reshape(2 * n2 * kp, 2 * kp * n2)
    ic = np.arange(gd)
    cc, sc = cs(np.outer(ic, ic), gd)
    norm = 1.0 / math.sqrt(seq * gd)
    return tuple(jnp.asarray(t, F32) for t in (w1, twr, twi, gmat, cc * norm, sc * norm))


def kernel(x, c, ada_w, ada_b, norm_mix, norm_ffn, attn_w_in, attn_lam_q1, attn_lam_k1, attn_lam_q2, attn_lam_k2, attn_subln, attn_w_out, fourier_w_in, fourier_w_out, router_group_w, router_group_b, router_expert_w, router_expert_b, expert_w_gate, expert_w_up, expert_w_down, norm_final):
    b, s, d = x.shape
    depth = ada_w.shape[0]
    assert s == FFT_N1 * FFT_N2 and depth == 2

    c8 = jnp.zeros((8, d), F32).at[:b].set(c)
    mod = _ada(c8, ada_w, ada_b.reshape(depth, 1, N_ADA * d))[:, :b]
    mod = mod.reshape(depth, b, N_ADA, 1, d)

    slopes = (2.0 ** (-8.0 * jnp.arange(1, N_DIFF_HEADS + 1, dtype=F32) / N_DIFF_HEADS)
              ).reshape(N_DIFF_HEADS, 1, 1)
    gd = d // N_FOURIER_GROUPS
    w1, twr, twi, gmat, cc, sc = _dft_tables(s, gd)
    w1, gmat, cc, sc = (t.astype(BF16) for t in (w1, gmat, cc, sc))

    for i in range(depth):
        sh1, sc1, g1, sh2, sc2, g2 = [mod[i, :, k] for k in range(N_ADA)]
        j = i // 2
        if i % 2 == 0:
            qscale = jnp.concatenate([jnp.full((d,), LOG2E * DIFF_HEAD_DIM ** -0.5, F32),
                                      jnp.ones((2 * d,), F32)])
            w_in = (attn_w_in[j] * qscale).astype(BF16)
            qkv = _norm_mod_matmul(x, norm_mix[i].reshape(1, d), sh1, sc1, w_in, tm=512)
            lam_init = 0.8 - 0.6 * math.exp(-0.3 * i)
            o = _diff_attention(qkv, slopes,
                                attn_lam_q1[j].reshape(1, -1), attn_lam_k1[j].reshape(1, -1),
                                attn_lam_q2[j].reshape(1, -1), attn_lam_k2[j].reshape(1, -1),
                                attn_subln[j].reshape(-1, 1), lam_init)
            x = _proj_residual(x, o, attn_w_out[j].astype(BF16), g1, tm=512)
        else:
            u = _norm_mod_matmul(x, norm_mix[i].reshape(1, d), sh1, sc1,
                                 fourier_w_in[j].astype(BF16), tm=512)
            y = _fft_stage1(u.reshape(b, FFT_N1, FFT_N2 * d), w1, twr, twi, d)
            x = _fft2_mix(y.reshape(b, 2, FFT_N1, FFT_N2, d), gmat, x.reshape(b, FFT_N2, FFT_N1, d),
                          cc, sc, fourier_w_out[j].astype(BF16), g1).reshape(b, s, d)

        x = moe_layer(x, norm_ffn[i], sh2, sc2, g2, router_group_w[i], router_group_b[i],
                      router_expert_w[i], router_expert_b[i], expert_w_gate, expert_w_up,
                      expert_w_down, i, norm_final, i == depth - 1)
    return x
```
